```python
import math
import jax, jax.numpy as jnp
from jax import lax
import numpy as np

D_MODEL = 1024
BATCH = 4
SEQ = 4096
DEPTH = 4
DEC_BATCH = 16
DEC_SEQ = 4096
PAST_LEN = 128

N_META = 16
N_MIXERS = 2
N_ATTN_LAYERS = (DEPTH + 1) // 2
N_SSD_LAYERS = DEPTH // 2
EPS = 1e-6
MLA_HEADS = 16
QK_NOPE_DIM = 64
QK_ROPE_DIM = 32
QK_HEAD_DIM = QK_NOPE_DIM + QK_ROPE_DIM
V_HEAD_DIM = 64
Q_LORA_RANK = 384
KV_LORA_RANK = 256
ROPE_BASE = 10000.0
Q_BLOCK = 128
D_INNER = 2 * D_MODEL
SSD_HEAD_DIM = 64
SSD_HEADS = D_INNER // SSD_HEAD_DIM
SSD_GROUPS = 4
HEADS_PER_GROUP = SSD_HEADS // SSD_GROUPS
D_STATE = 128
D_CONV = 5
CONV_DIM = D_INNER + 2 * SSD_GROUPS * D_STATE
CHUNK = 128
META_PAD = CHUNK - N_META
FFN_HIDDEN = -(-8 * D_MODEL // (3 * 256)) * 256

kernel_name = 'hybrid_mla_ssd_encoder'


def rmsnorm(x, w):
    xf = x.astype(jnp.float32)
    y = xf * lax.rsqrt(jnp.mean(xf * xf, axis=-1, keepdims=True) + EPS) * w.astype(jnp.float32)
    return y.astype(x.dtype)


def apply_rope(x, pos):
    x_pass, x_rot = jnp.split(x, [QK_NOPE_DIM], axis=-1)
    half = QK_ROPE_DIM // 2
    inv_freq = ROPE_BASE ** (-jnp.arange(half, dtype=jnp.float32) / half)
    ang = pos[:, None] * inv_freq[None, :]
    cos = jnp.cos(ang)[None, :, None, :]
    sin = jnp.sin(ang)[None, :, None, :]
    xr = x_rot.astype(jnp.float32)
    x1, x2 = xr[..., :half], xr[..., half:]
    rot = jnp.concatenate([x1 * cos - x2 * sin, x1 * sin + x2 * cos], axis=-1)
    return jnp.concatenate([x_pass, rot.astype(x.dtype)], axis=-1)


def block_attention(q, k, v):
    b, L, H, dq = q.shape
    nb = -(-L // Q_BLOCK)
    pad = nb * Q_BLOCK - L
    qp = jnp.pad(q, ((0, 0), (0, pad), (0, 0), (0, 0)))
    qp = jnp.moveaxis(qp.reshape(b, nb, Q_BLOCK, H, dq), 1, 0)
    scale = 1.0 / math.sqrt(QK_HEAD_DIM)

    def one_block(qb):
        s = jnp.einsum('bqhd,bkhd->bhqk', qb, k, preferred_element_type=jnp.float32) * scale
        p = jax.nn.softmax(s, axis=-1)
        return jnp.einsum('bhqk,bkhd->bqhd', p.astype(v.dtype), v)

    out = lax.map(one_block, qp)
    out = jnp.moveaxis(out, 0, 1).reshape(b, nb * Q_BLOCK, H, V_HEAD_DIM)
    return out[:, :L]


def mla_mixer(u, w_in, q_norm, kv_norm, w_q_up, w_kv_up, q_head_norm, k_head_norm, w_out):
    b, L, _ = u.shape
    c_q, c_kv, k_rope = jnp.split(u @ w_in, [Q_LORA_RANK, Q_LORA_RANK + KV_LORA_RANK], axis=-1)
    q = (rmsnorm(c_q, q_norm) @ w_q_up).reshape(b, L, MLA_HEADS, QK_HEAD_DIM)
    kv = (rmsnorm(c_kv, kv_norm) @ w_kv_up).reshape(b, L, MLA_HEADS, QK_NOPE_DIM + V_HEAD_DIM)
    k_nope, v = jnp.split(kv, [QK_NOPE_DIM], axis=-1)
    k = jnp.concatenate(
        [k_nope, jnp.broadcast_to(k_rope[:, :, None, :], (b, L, MLA_HEADS, QK_ROPE_DIM))], axis=-1)
    q = rmsnorm(q, q_head_norm)
    k = rmsnorm(k, k_head_norm)
    pos = jnp.arange(L, dtype=jnp.float32)
    q = apply_rope(q, pos)
    k = apply_rope(k, pos)
    o = block_attention(q, k, v)
    return o.reshape(b, L, MLA_HEADS * V_HEAD_DIM) @ w_out


def depthwise_conv(x, w, bias):
    y = lax.conv_general_dilated(
        x, w[:, None, :].astype(x.dtype), window_strides=(1,),
        padding=[(D_CONV // 2, D_CONV // 2)],
        dimension_numbers=('NWC', 'WIO', 'NWC'), feature_group_count=x.shape[-1])
    return y + bias.astype(x.dtype)


def ssd_chunked(X, A, B, C):
    b, T = X.shape[0], X.shape[1]
    nc = T // CHUNK
    X = X.reshape(b, nc, CHUNK, SSD_GROUPS, HEADS_PER_GROUP, SSD_HEAD_DIM)
    A = A.reshape(b, nc, CHUNK, SSD_GROUPS, HEADS_PER_GROUP)
    B = B.reshape(b, nc, CHUNK, SSD_GROUPS, D_STATE)
    C = C.reshape(b, nc, CHUNK, SSD_GROUPS, D_STATE)
    a_cs = jnp.cumsum(A, axis=2)
    seg = a_cs[:, :, :, None] - a_cs[:, :, None, :]
    mask = jnp.tril(jnp.ones((CHUNK, CHUNK), dtype=bool))[:, :, None, None]
    l_mat = jnp.exp(jnp.where(mask, seg, -jnp.inf))
    y_diag = jnp.einsum('bclgn,bcsgn,bclsgj,bcsgjp->bclgjp', C, B, l_mat, X)
    decay_states = jnp.exp(a_cs[:, :, -1:] - a_cs)
    states = jnp.einsum('bclgn,bclgj,bclgjp->bcgjpn', B, decay_states, X)
    chunk_decay = jnp.exp(a_cs[:, :, -1])

    def step(carry, inp):
        st, dc = inp
        return carry * dc[..., None, None] + st, carry

    init = jnp.zeros((b, SSD_GROUPS, HEADS_PER_GROUP, SSD_HEAD_DIM, D_STATE), jnp.float32)
    _, prev = lax.scan(step, init, (jnp.moveaxis(states, 1, 0), jnp.moveaxis(chunk_decay, 1, 0)))
    prev = jnp.moveaxis(prev, 0, 1)
    y_off = jnp.einsum('bclgn,bcgjpn,bclgj->bclgjp', C, prev, jnp.exp(a_cs))
    return (y_diag + y_off).reshape(b, T, SSD_GROUPS, HEADS_PER_GROUP, SSD_HEAD_DIM)


def ssd_mixer(u, w_in, conv_w, conv_b, dt_bias, a_log, d_skip, norm_w, w_out):
    b, L, _ = u.shape
    z, xbc, dt_raw = jnp.split(u @ w_in, [D_INNER, D_INNER + CONV_DIM], axis=-1)
    xbc = jax.nn.silu(depthwise_conv(xbc, conv_w, conv_b))
    gn = SSD_GROUPS * D_STATE
    xs, bm, cm = jnp.split(xbc.astype(jnp.float32), [D_INNER, D_INNER + gn], axis=-1)
    xs = xs.reshape(b, L, SSD_GROUPS, HEADS_PER_GROUP, SSD_HEAD_DIM)
    bm = bm.reshape(b, L, SSD_GROUPS, D_STATE)
    cm = cm.reshape(b, L, SSD_GROUPS, D_STATE)
    dt = jax.nn.softplus(dt_raw.astype(jnp.float32).reshape(b, L, 2, SSD_GROUPS, HEADS_PER_GROUP)
                         + dt_bias.astype(jnp.float32).reshape(2, SSD_GROUPS, HEADS_PER_GROUP))
    a = -jnp.exp(a_log.astype(jnp.float32)).reshape(2, SSD_GROUPS, HEADS_PER_GROUP)

    def pad_front(t):
        return jnp.pad(t, ((0, 0), (META_PAD, 0)) + ((0, 0),) * (t.ndim - 2))

    xs_p, b_p, c_p, dt_p = pad_front(xs), pad_front(bm), pad_front(cm), pad_front(dt)

    def direction(d, reverse):
        fl = (lambda t: jnp.flip(t, axis=1)) if reverse else (lambda t: t)
        dt_d = fl(dt_p[:, :, d])
        y_d = ssd_chunked(fl(xs_p) * dt_d[..., None], dt_d * a[d], fl(b_p), fl(c_p))
        return fl(y_d)

    y = direction(0, False) + direction(1, True)
    y = y[:, META_PAD:] + d_skip.astype(jnp.float32).reshape(SSD_GROUPS, HEADS_PER_GROUP)[..., None] * xs
    y = y.reshape(b, L, D_INNER) * jax.nn.silu(z.astype(jnp.float32))
    y = rmsnorm(y, norm_w).astype(u.dtype)
    return y @ w_out


def swiglu(u, w_gate, w_up, w_down):
    return (jax.nn.silu(u @ w_gate) * (u @ w_up)) @ w_down


def encode(x, meta_tokens, mix_norm, ffn_norm, attn_p, ssd_p, ffn_p, final_norm):
    b = x.shape[0]
    meta = jnp.broadcast_to(meta_tokens.astype(x.dtype)[None], (b, N_META, D_MODEL))
    h = jnp.concatenate([meta, x], axis=1)
    ia, ib = 0, 0
    for layer in range(DEPTH):
        u = rmsnorm(h, mix_norm[layer])
        if layer % N_MIXERS == 0:
            h = h + mla_mixer(u, *[p[ia] for p in attn_p])
            ia += 1
        else:
            h = h + ssd_mixer(u, *[p[ib] for p in ssd_p])
            ib += 1
        u = rmsnorm(h, ffn_norm[layer])
        h = h + swiglu(u, *[p[layer] for p in ffn_p])
    h = rmsnorm(h, final_norm)
    return h[:, N_META:]


def setup_inputs(seed: int = 0) -> dict:
    key = jax.random.key(seed)
    ks = iter(jax.random.split(key, 40))

    def nrm(shape, scale):
        return jax.random.normal(next(ks), shape, jnp.float32) * scale

    def gain(shape):
        return 1.0 + nrm(shape, 0.02)

    out_scale = (2.0 * DEPTH) ** -0.5
    na, ns = N_ATTN_LAYERS, N_SSD_LAYERS
    dt0 = jnp.exp(jax.random.uniform(next(ks), (ns, 2, SSD_HEADS), jnp.float32,
                                     math.log(1e-3), math.log(1e-1)))
    dt_bias = dt0 + jnp.log(-jnp.expm1(-dt0))
    a_log = jnp.log(jax.random.uniform(next(ks), (ns, 2, SSD_HEADS), jnp.float32, 1.0, 16.0))
    return {
        'x_prompt': nrm((BATCH, SEQ, D_MODEL), 1.0),
        'x_sample': nrm((DEC_BATCH, DEC_SEQ, D_MODEL), 1.0),
        'meta_tokens': nrm((N_META, D_MODEL), 1.0),
        'mix_norm': gain((DEPTH, D_MODEL)),
        'ffn_norm': gain((DEPTH, D_MODEL)),
        'mla_w_in': nrm((na, D_MODEL, Q_LORA_RANK + KV_LORA_RANK + QK_ROPE_DIM), D_MODEL ** -0.5),
        'mla_q_norm': gain((na, Q_LORA_RANK)),
        'mla_kv_norm': gain((na, KV_LORA_RANK)),
        'mla_w_q_up': nrm((na, Q_LORA_RANK, MLA_HEADS * QK_HEAD_DIM), Q_LORA_RANK ** -0.5),
        'mla_w_kv_up': nrm((na, KV_LORA_RANK, MLA_HEADS * (QK_NOPE_DIM + V_HEAD_DIM)), KV_LORA_RANK ** -0.5),
        'mla_q_head_norm': gain((na, QK_HEAD_DIM)),
        'mla_k_head_norm': gain((na, QK_HEAD_DIM)),
        'mla_w_out': nrm((na, MLA_HEADS * V_HEAD_DIM, D_MODEL), (MLA_HEADS * V_HEAD_DIM) ** -0.5 * out_scale),
        'ssd_w_in': nrm((ns, D_MODEL, 2 * D_INNER + 2 * SSD_GROUPS * D_STATE + 2 * SSD_HEADS), D_MODEL ** -0.5),
        'ssd_conv_w': nrm((ns, D_CONV, CONV_DIM), D_CONV ** -0.5),
        'ssd_conv_b': nrm((ns, CONV_DIM), 0.01),
        'ssd_dt_bias': dt_bias,
        'ssd_a_log': a_log,
        'ssd_d': gain((ns, SSD_HEADS)),
        'ssd_norm': gain((ns, D_INNER)),
        'ssd_w_out': nrm((ns, D_INNER, D_MODEL), D_INNER ** -0.5 * out_scale),
        'ffn_w_gate': nrm((DEPTH, D_MODEL, FFN_HIDDEN), D_MODEL ** -0.5),
        'ffn_w_up': nrm((DEPTH, D_MODEL, FFN_HIDDEN), D_MODEL ** -0.5),
        'ffn_w_down': nrm((DEPTH, FFN_HIDDEN, D_MODEL), FFN_HIDDEN ** -0.5 * out_scale),
        'final_norm': gain((D_MODEL,)),
    }


def reference(x_prompt, x_sample, meta_tokens, mix_norm, ffn_norm,
              mla_w_in, mla_q_norm, mla_kv_norm, mla_w_q_up, mla_w_kv_up,
              mla_q_head_norm, mla_k_head_norm, mla_w_out,
              ssd_w_in, ssd_conv_w, ssd_conv_b, ssd_dt_bias, ssd_a_log, ssd_d, ssd_norm, ssd_w_out,
              ffn_w_gate, ffn_w_up, ffn_w_down, final_norm):
    attn_p = (mla_w_in, mla_q_norm, mla_kv_norm, mla_w_q_up, mla_w_kv_up,
              mla_q_head_norm, mla_k_head_norm, mla_w_out)
    ssd_p = (ssd_w_in, ssd_conv_w, ssd_conv_b, ssd_dt_bias, ssd_a_log, ssd_d, ssd_norm, ssd_w_out)
    ffn_p = (ffn_w_gate, ffn_w_up, ffn_w_down)
    y_prompt = encode(x_prompt, meta_tokens, mix_norm, ffn_norm, attn_p, ssd_p, ffn_p, final_norm)
    y_sample = encode(x_sample, meta_tokens, mix_norm, ffn_norm, attn_p, ssd_p, ffn_p, final_norm)
    return (y_prompt, y_sample)
```

```python
import functools
import math

import jax
import jax.numpy as jnp
from jax import lax
from jax.experimental import pallas as pl
from jax.experimental.pallas import tpu as pltpu

F32 = jnp.float32
BF16 = jnp.bfloat16

D_MODEL = 1024
N_META = 16
EPS = 1e-6
MLA_HEADS = 16
QK_NOPE_DIM = 64
QK_ROPE_DIM = 32
QK_HEAD_DIM = QK_NOPE_DIM + QK_ROPE_DIM
V_HEAD_DIM = 64
Q_LORA_RANK = 384
KV_LORA_RANK = 256
ROPE_BASE = 10000.0
D_INNER = 2 * D_MODEL
SSD_HEAD_DIM = 64
SSD_HEADS = D_INNER // SSD_HEAD_DIM
SSD_GROUPS = 4
HEADS_PER_GROUP = SSD_HEADS // SSD_GROUPS
D_STATE = 128
D_CONV = 5
CONV_DIM = D_INNER + 2 * SSD_GROUPS * D_STATE
CHUNK = 128
FFN_HIDDEN = -(-8 * D_MODEL // (3 * 256)) * 256

LANES = 128
FRONT_PAD = CHUNK - N_META
HEAD_PAD = LANES
ROPE_LANE0 = QK_NOPE_DIM
MASK_LANE = QK_HEAD_DIM
MASK_VALUE = -1e30
HEAD_GROUP = 4
GROUP_W = HEADS_PER_GROUP * SSD_HEAD_DIM
VMEM_LIMIT = 56 * 1024 * 1024


def _pick_tile(n, candidates):
    for c in candidates:
        if n % c == 0:
            return c
    return n


def _params(sem, vmem=VMEM_LIMIT):
    return pltpu.CompilerParams(dimension_semantics=sem, vmem_limit_bytes=vmem)


def _const_spec(shape):
    nd = len(shape)
    return pl.BlockSpec(shape, lambda *_: (0,) * nd, pipeline_mode=pl.Buffered(1))


def _rms(x, w):
    return x * lax.rsqrt(jnp.mean(x * x, axis=-1, keepdims=True) + EPS) * w


def _silu(x):
    return x * (1.0 / (1.0 + jnp.exp(-x)))


def _mla_proj_kernel(h_ref, nw_ref, win_ref, qn_ref, kvn_ref, wq_ref, wk_ref, wv_ref,
                     gq_ref, gkn_ref, gkr_ref, cos_ref, sin_ref, bias_ref, one_ref,
                     q_ref, k_ref, v_ref):
    u = _rms(h_ref[0], nw_ref[...]).astype(BF16)
    c = jnp.dot(u, win_ref[...], preferred_element_type=F32)
    cq = _rms(c[:, :Q_LORA_RANK], qn_ref[...]).astype(BF16)
    ckv = _rms(c[:, Q_LORA_RANK:Q_LORA_RANK + KV_LORA_RANK], kvn_ref[...]).astype(BF16)
    k_rope = c[:, Q_LORA_RANK + KV_LORA_RANK:]
    q = jnp.dot(cq, wq_ref[...], preferred_element_type=F32)
    k_nope = jnp.dot(ckv, wk_ref[...], preferred_element_type=F32)
    v_ref[0] = jnp.dot(ckv, wv_ref[...], preferred_element_type=F32).astype(BF16)

    cos = cos_ref[...]
    sin = sin_ref[...]
    lane = lax.broadcasted_iota(jnp.int32, cos.shape, 1)
    half = QK_ROPE_DIM // 2
    first_half = (lane >= ROPE_LANE0) & (lane < ROPE_LANE0 + half)

    def rope(x):
        swapped = jnp.where(first_half, pltpu.roll(x, HEAD_PAD - half, 1), pltpu.roll(x, half, 1))
        return x * cos + swapped * sin

    inv_d = 1.0 / QK_HEAD_DIM
    kr = rope(k_rope * gkr_ref[...])
    kr_ss = jnp.sum(k_rope * k_rope, axis=-1, keepdims=True)
    gq = gq_ref[...]
    gkn = gkn_ref[...]
    bias = bias_ref[...]
    one = one_ref[...]
    for hd in range(MLA_HEADS):
        sl = slice(hd * HEAD_PAD, (hd + 1) * HEAD_PAD)
        qh = q[:, sl]
        rq = lax.rsqrt(jnp.sum(qh * qh, axis=-1, keepdims=True) * inv_d + EPS)
        q_ref[0, hd] = (rope(qh * rq * gq) + one).astype(BF16)
        kh = k_nope[:, sl]
        rk = lax.rsqrt((jnp.sum(kh * kh, axis=-1, keepdims=True) + kr_ss) * inv_d + EPS)
        k_ref[0, hd] = ((kh * gkn + kr) * rk + bias).astype(BF16)


def _mla_proj(h, nw, p, tables):
    B, Tp, _ = h.shape
    tm = _pick_tile(Tp, (384, 128))
    row = lambda b, i: (b, i, 0)
    tab = lambda b, i: (i, 0)
    qk_shape = jax.ShapeDtypeStruct((B, MLA_HEADS, Tp, HEAD_PAD), BF16)
    qk_spec = pl.BlockSpec((1, MLA_HEADS, tm, HEAD_PAD), lambda b, i: (b, 0, i, 0))
    consts = (nw, p['w_in'], p['q_norm'], p['kv_norm'], p['w_q'], p['w_k'], p['w_v'],
              p['gq'], p['gkn'], p['gkr'])
    return pl.pallas_call(
        _mla_proj_kernel,
        grid=(B, Tp // tm),
        in_specs=[pl.BlockSpec((1, tm, D_MODEL), row)]
        + [_const_spec(a.shape) for a in consts]
        + [pl.BlockSpec((tm, HEAD_PAD), tab)] * 3
        + [_const_spec(tables['one'].shape)],
        out_specs=[qk_spec, qk_spec, pl.BlockSpec((1, tm, MLA_HEADS * V_HEAD_DIM), row)],
        out_shape=[qk_shape, qk_shape, jax.ShapeDtypeStruct((B, Tp, MLA_HEADS * V_HEAD_DIM), BF16)],
        compiler_params=_params(("parallel", "parallel")),
        name="mla_proj",
    )(h, *consts, tables['cos'], tables['sin'], tables['bias'], tables['one'])


def _flash_kernel(q_ref, k_ref, v_ref, o_ref):
    v = v_ref[0]
    head_of_lane = lax.broadcasted_iota(jnp.int32, o_ref.shape[1:], 1) // V_HEAD_DIM
    out = jnp.zeros(o_ref.shape[1:], F32)
    for j in range(HEAD_GROUP):
        s = lax.dot_general(q_ref[0, j], k_ref[0, j], (((1,), (1,)), ((), ())),
                            preferred_element_type=F32)
        m = jnp.max(s, axis=-1, keepdims=True)
        p = jnp.exp(s - m)
        l = jnp.sum(p, axis=-1, keepdims=True)
        o = jnp.dot(p.astype(BF16), v, preferred_element_type=F32)
        out = jnp.where(head_of_lane == j, o * (1.0 / l), out)
    o_ref[0] = out.astype(BF16)


def _flash(q, k, v):
    B, _, Tp, _ = q.shape
    tq = _pick_tile(Tp, (384, 128))
    gw = HEAD_GROUP * V_HEAD_DIM
    return pl.pallas_call(
        _flash_kernel,
        grid=(B, MLA_HEADS // HEAD_GROUP, Tp // tq),
        in_specs=[pl.BlockSpec((1, HEAD_GROUP, tq, HEAD_PAD), lambda b, g, i: (b, g, i, 0)),
                  pl.BlockSpec((1, HEAD_GROUP, Tp, HEAD_PAD), lambda b, g, i: (b, g, 0, 0)),
                  pl.BlockSpec((1, Tp, gw), lambda b, g, i: (b, 0, g))],
        out_specs=pl.BlockSpec((1, tq, gw), lambda b, g, i: (b, i, g)),
        out_shape=jax.ShapeDtypeStruct((B, Tp, MLA_HEADS * V_HEAD_DIM), BF16),
        compiler_params=_params(("parallel", "parallel", "parallel")),
        name="flash",
    )(q, k, v)


def _linear_residual_kernel(x_ref, w_ref, h_ref, o_ref):
    o_ref[...] = h_ref[...] + jnp.dot(x_ref[...], w_ref[...], preferred_element_type=F32)


def _linear_residual(x, w, h):
    R, K = x.shape
    tm = _pick_tile(R, (768, 512, 384, 256, 128))
    return pl.pallas_call(
        _linear_residual_kernel,
        grid=(R // tm,),
        in_specs=[pl.BlockSpec((tm, K), lambda i: (i, 0)), _const_spec(w.shape),
                  pl.BlockSpec((tm, D_MODEL), lambda i: (i, 0))],
        out_specs=pl.BlockSpec((tm, D_MODEL), lambda i: (i, 0)),
        out_shape=jax.ShapeDtypeStruct((R, D_MODEL), F32),
        compiler_params=_params(("parallel",)),
        name="linear_residual",
    )(x, w, h)


def _hidden_chunks():
    bounds, start = [], 0
    while start < FFN_HIDDEN:
        stop = min(start + 1536, FFN_HIDDEN)
        bounds.append((start, stop))
        start = stop
    return bounds


def _ffn_kernel(h_ref, nw_ref, wg_ref, wu_ref, wd_ref, fw_ref, o_ref, *, final):
    h = h_ref[...]
    u = _rms(h, nw_ref[...]).astype(BF16)
    acc = h
    for lo, hi in _hidden_chunks():
        g = jnp.dot(u, wg_ref[:, lo:hi], preferred_element_type=F32)
        up = jnp.dot(u, wu_ref[:, lo:hi], preferred_element_type=F32)
        a = (_silu(g) * up).astype(BF16)
        acc = acc + jnp.dot(a, wd_ref[lo:hi, :], preferred_element_type=F32)
    if final:
        acc = _rms(acc, fw_ref[...])
    o_ref[...] = acc


def _ffn(h, nw, wg, wu, wd, fw, final):
    R = h.shape[0]
    tm = _pick_tile(R, (512, 384, 256, 128))
    row = pl.BlockSpec((tm, D_MODEL), lambda i: (i, 0))
    return pl.pallas_call(
        functools.partial(_ffn_kernel, final=final),
        grid=(R // tm,),
        in_specs=[row] + [_const_spec(a.shape) for a in (nw, wg, wu, wd, fw)],
        out_specs=row,
        out_shape=jax.ShapeDtypeStruct((R, D_MODEL), F32),
        compiler_params=_params(("parallel",)),
        name="ffn",
    )(h, nw, wg, wu, wd, fw)


def _ssd_inproj_kernel(h_ref, nw_ref, w_ref, dtb_ref, z_ref, xbc_ref, dt_ref):
    tm = h_ref.shape[1]
    u = _rms(h_ref[0], nw_ref[...]).astype(BF16)
    y = jnp.dot(u, w_ref[...], preferred_element_type=F32)
    row = pl.program_id(1) * tm + lax.broadcasted_iota(jnp.int32, (tm, 1), 0)
    valid = row >= FRONT_PAD
    z_ref[0] = y[:, :D_INNER].astype(BF16)
    xbc_ref[0] = jnp.where(valid, y[:, D_INNER:D_INNER + CONV_DIM], 0.0).astype(BF16)
    dt_raw = y[:, D_INNER + CONV_DIM:] + dtb_ref[...]
    dt = jnp.maximum(dt_raw, 0.0) + jnp.log(1.0 + jnp.exp(-jnp.abs(dt_raw)))
    dt_ref[0] = jnp.where(valid, dt, 0.0)


def _ssd_inproj(h, nw, w, dtb):
    B, Tp, _ = h.shape
    tm = _pick_tile(Tp, (384, 128))
    row = lambda b, i: (b, i, 0)
    return pl.pallas_call(
        _ssd_inproj_kernel,
        grid=(B, Tp // tm),
        in_specs=[pl.BlockSpec((1, tm, D_MODEL), row)] + [_const_spec(a.shape) for a in (nw, w, dtb)],
        out_specs=[pl.BlockSpec((1, tm, D_INNER), row), pl.BlockSpec((1, tm, CONV_DIM), row),
                   pl.BlockSpec((1, tm, LANES), row)],
        out_shape=[jax.ShapeDtypeStruct((B, Tp, D_INNER), BF16),
                   jax.ShapeDtypeStruct((B, Tp, CONV_DIM), BF16),
                   jax.ShapeDtypeStruct((B, Tp, LANES), F32)],
        compiler_params=_params(("parallel", "parallel")),
        name="ssd_inproj",
    )(h, nw, w, dtb)


CONV_MARGIN = 8


def _conv_kernel(x_ref, w_ref, b_ref, o_ref, xs_ref, *, row_chunk):
    Tp, cw = x_ref.shape[1], x_ref.shape[2]
    zeros = jnp.zeros((CONV_MARGIN, cw), F32)
    xs_ref[0:CONV_MARGIN, :] = zeros
    xs_ref[CONV_MARGIN + Tp:, :] = zeros
    xs_ref[CONV_MARGIN:CONV_MARGIN + Tp, :] = x_ref[0].astype(F32)
    w = w_ref[...]
    b = b_ref[...]
    for r0 in range(0, Tp, row_chunk):
        acc = b
        for tap in range(D_CONV):
            lo = r0 + CONV_MARGIN + tap - D_CONV // 2
            acc = acc + w[tap:tap + 1, :] * xs_ref[lo:lo + row_chunk, :]
        o_ref[0, r0:r0 + row_chunk, :] = _silu(acc).astype(BF16)


def _ssd_conv(xbc, w, b):
    B, Tp, C = xbc.shape
    cw = 512
    blk = pl.BlockSpec((1, Tp, cw), lambda bi, ci: (bi, 0, ci))
    return pl.pallas_call(
        functools.partial(_conv_kernel, row_chunk=_pick_tile(Tp, (384, 128))),
        grid=(B, C // cw),
        in_specs=[blk, pl.BlockSpec((D_CONV, cw), lambda bi, ci: (0, ci)),
                  pl.BlockSpec((1, cw), lambda bi, ci: (0, ci))],
        out_specs=blk,
        out_shape=jax.ShapeDtypeStruct((B, Tp, C), BF16),
        scratch_shapes=[pltpu.VMEM((Tp + 2 * CONV_MARGIN, cw), F32)],
        compiler_params=_params(("parallel", "parallel")),
        name="ssd_conv",
    )(xbc, w, b)


def _ssd_scan_kernel(x_ref, b_ref, c_ref, dt_ref, alog_ref, y_ref, st_ref, *, reverse):
    @pl.when(pl.program_id(1) == 0)
    def _():
        st_ref[...] = jnp.zeros(st_ref.shape, F32)

    a_step = dt_ref[0] * (-jnp.exp(alog_ref[...]))
    row = lax.broadcasted_iota(jnp.int32, (CHUNK, CHUNK), 0)
    col = lax.broadcasted_iota(jnp.int32, (CHUNK, CHUNK), 1)
    causal = (col >= row) if reverse else (col <= row)
    cs = jnp.dot(causal.astype(F32), a_step, preferred_element_type=F32,
                 precision=lax.Precision.HIGHEST)
    cs_t = cs.T
    total = cs[0:1, :] if reverse else cs[CHUNK - 1:CHUNK, :]
    e_in = jnp.exp(cs)
    e_out = jnp.exp(total - cs)
    e_tot = jnp.exp(total)
    dt = dt_ref[0]
    low = lax.broadcasted_iota(jnp.int32, (CHUNK, LANES), 1) < SSD_HEAD_DIM
    low_row = low[0:1, :]
    head0 = SSD_HEADS if reverse else 0

    def expand(arr, hd, shape, sel):
        return jnp.where(sel, jnp.broadcast_to(arr[:, hd:hd + 1], shape),
                         jnp.broadcast_to(arr[:, hd + 1:hd + 2], shape))

    pairs = HEADS_PER_GROUP // 2
    for g in range(SSD_GROUPS):
        bg = b_ref[0, :, g * D_STATE:(g + 1) * D_STATE]
        cg = c_ref[0, :, g * D_STATE:(g + 1) * D_STATE]
        gram = lax.dot_general(cg, bg, (((1,), (1,)), ((), ())), preferred_element_type=F32)
        heads = [head0 + g * HEADS_PER_GROUP + 2 * pr for pr in range(pairs)]
        full = (CHUNK, LANES)
        dt_e = jnp.concatenate([expand(dt, hd, full, low) for hd in heads], axis=1)
        in_e = jnp.concatenate([expand(e_in, hd, full, low) for hd in heads], axis=1)
        out_e = jnp.concatenate([expand(e_out, hd, full, low) for hd in heads], axis=1)
        tot_e = jnp.concatenate([expand(e_tot, hd, (1, LANES), low_row) for hd in heads], axis=1)
        xdt = x_ref[0, :, g * GROUP_W:(g + 1) * GROUP_W].astype(F32) * dt_e
        xdt_b = xdt.astype(BF16)
        state = st_ref[g]
        y_off = jnp.dot(cg, state.astype(BF16), preferred_element_type=F32) * in_e
        for pr, hd in enumerate(heads):
            ms = []
            for k in range(2):
                seg = cs[:, hd + k:hd + k + 1] - cs_t[hd + k:hd + k + 1, :]
                ms.append((gram * jnp.exp(jnp.where(causal, seg, -jnp.inf))).astype(BF16))
            xp = xdt_b[:, pr * LANES:(pr + 1) * LANES]
            y_diag = jnp.where(low, jnp.dot(ms[0], xp, preferred_element_type=F32),
                               jnp.dot(ms[1], xp, preferred_element_type=F32))
            lo = g * GROUP_W + pr * LANES
            y_ref[0, :, lo:lo + LANES] = y_diag + y_off[:, pr * LANES:(pr + 1) * LANES]
        bg_t = bg.astype(F32).T.astype(BF16)
        st_ref[g] = state * tot_e + jnp.dot(bg_t, (xdt * out_e).astype(BF16), preferred_element_type=F32)


def _ssd_scan(xbc, dt, a_log, reverse):
    B, Tp, _ = xbc.shape
    nc = Tp // CHUNK
    pos = (lambda c: nc - 1 - c) if reverse else (lambda c: c)
    bc_w = SSD_GROUPS * D_STATE
    return pl.pallas_call(
        functools.partial(_ssd_scan_kernel, reverse=reverse),
        grid=(B, nc),
        in_specs=[pl.BlockSpec((1, CHUNK, D_INNER), lambda b, c: (b, pos(c), 0)),
                  pl.BlockSpec((1, CHUNK, bc_w), lambda b, c: (b, pos(c), D_INNER // bc_w)),
                  pl.BlockSpec((1, CHUNK, bc_w), lambda b, c: (b, pos(c), D_INNER // bc_w + 1)),
                  pl.BlockSpec((1, CHUNK, LANES), lambda b, c: (b, pos(c), 0)),
                  pl.BlockSpec((1, LANES), lambda b, c: (0, 0))],
        out_specs=pl.BlockSpec((1, CHUNK, D_INNER), lambda b, c: (b, pos(c), 0)),
        out_shape=jax.ShapeDtypeStruct((B, Tp, D_INNER), F32),
        scratch_shapes=[pltpu.VMEM((SSD_GROUPS, D_STATE, GROUP_W), F32)],
        compiler_params=_params(("parallel", "arbitrary")),
        name="ssd_scan_bwd" if reverse else "ssd_scan_fwd",
    )(xbc, xbc, xbc, dt, a_log)


def _ssd_out_kernel(yf_ref, yb_ref, x_ref, z_ref, d_ref, nw_ref, w_ref, h_ref, o_ref):
    y = yf_ref[...] + yb_ref[...] + d_ref[...] * x_ref[...].astype(F32)
    y = y * _silu(z_ref[...].astype(F32))
    yn = _rms(y, nw_ref[...]).astype(BF16)
    o_ref[...] = h_ref[...] + jnp.dot(yn, w_ref[...], preferred_element_type=F32)


def _ssd_out(yf, yb, xbc, z, d_e, nw, w, h):
    R = h.shape[0]
    tm = _pick_tile(R, (256, 128))
    wide = pl.BlockSpec((tm, D_INNER), lambda i: (i, 0))
    row = pl.BlockSpec((tm, D_MODEL), lambda i: (i, 0))
    return pl.pallas_call(
        _ssd_out_kernel,
        grid=(R // tm,),
        in_specs=[wide, wide, wide, wide] + [_const_spec(a.shape) for a in (d_e, nw, w)] + [row],
        out_specs=row,
        out_shape=jax.ShapeDtypeStruct((R, D_MODEL), F32),
        compiler_params=_params(("parallel",)),
        name="ssd_out",
    )(yf, yb, xbc, z, d_e, nw, w, h)


def _row(v, width=None):
    v = v.astype(F32).reshape(1, -1)
    if width is not None and v.shape[1] < width:
        v = jnp.pad(v, ((0, 0), (0, width - v.shape[1])))
    return v


def _prep_mla(w_in, q_norm, kv_norm, w_q_up, w_kv_up, q_head_norm, k_head_norm, w_out):
    lat = Q_LORA_RANK + KV_LORA_RANK
    rope_cols = jnp.pad(w_in[:, lat:], ((0, 0), (ROPE_LANE0, HEAD_PAD - QK_HEAD_DIM)))
    w_in_p = jnp.concatenate([w_in[:, :lat], rope_cols], axis=1).astype(BF16)
    w_q = w_q_up.reshape(Q_LORA_RANK, MLA_HEADS, QK_HEAD_DIM)
    w_q = jnp.pad(w_q, ((0, 0), (0, 0), (0, HEAD_PAD - QK_HEAD_DIM))).reshape(Q_LORA_RANK, -1).astype(BF16)
    w_kv = w_kv_up.reshape(KV_LORA_RANK, MLA_HEADS, QK_NOPE_DIM + V_HEAD_DIM)
    w_k = jnp.pad(w_kv[:, :, :QK_NOPE_DIM], ((0, 0), (0, 0), (0, HEAD_PAD - QK_NOPE_DIM)))
    w_k = w_k.reshape(KV_LORA_RANK, -1).astype(BF16)
    w_v = w_kv[:, :, QK_NOPE_DIM:].reshape(KV_LORA_RANK, -1).astype(BF16)
    scale = 1.0 / math.sqrt(QK_HEAD_DIM)
    gk = k_head_norm.astype(F32)
    return dict(
        w_in=w_in_p, q_norm=_row(q_norm), kv_norm=_row(kv_norm), w_q=w_q, w_k=w_k, w_v=w_v,
        gq=_row(q_head_norm.astype(F32) * scale, HEAD_PAD),
        gkn=_row(gk[:QK_NOPE_DIM], HEAD_PAD),
        gkr=_row(jnp.pad(gk[QK_NOPE_DIM:], (ROPE_LANE0, 0)), HEAD_PAD),
        w_out=w_out.astype(BF16))


def _attention_tables(Tp):
    half = QK_ROPE_DIM // 2
    pos = jnp.arange(Tp, dtype=F32) - FRONT_PAD
    inv_freq = ROPE_BASE ** (-jnp.arange(half, dtype=F32) / half)
    ang = pos[:, None] * inv_freq[None, :]
    cos, sin = jnp.cos(ang), jnp.sin(ang)
    tail = HEAD_PAD - QK_HEAD_DIM
    cos_t = jnp.concatenate([jnp.ones((Tp, ROPE_LANE0), F32), cos, cos, jnp.ones((Tp, tail), F32)], axis=1)
    sin_t = jnp.concatenate([jnp.zeros((Tp, ROPE_LANE0), F32), -sin, sin, jnp.zeros((Tp, tail), F32)], axis=1)
    is_pad = (jnp.arange(Tp) < FRONT_PAD)[:, None]
    lane = jnp.arange(HEAD_PAD)[None, :]
    bias = jnp.where(is_pad & (lane == MASK_LANE), MASK_VALUE, 0.0).astype(F32)
    one = (lane == MASK_LANE).astype(F32)
    return dict(cos=cos_t, sin=sin_t, bias=bias, one=one)


def _prep_ssd(w_in, conv_w, conv_b, dt_bias, a_log, d_skip, norm_w, w_out):
    w_in_p = jnp.pad(w_in, ((0, 0), (0, LANES - 2 * SSD_HEADS))).astype(BF16)
    return dict(
        w_in=w_in_p, conv_w=conv_w.astype(F32), conv_b=_row(conv_b),
        dt_bias=_row(dt_bias, LANES), a_log=_row(a_log, LANES),
        d_e=_row(jnp.repeat(d_skip.astype(F32), SSD_HEAD_DIM)),
        norm=_row(norm_w), w_out=w_out.astype(BF16))


def kernel(x_prompt, x_sample, meta_tokens, mix_norm, ffn_norm, mla_w_in, mla_q_norm, mla_kv_norm, mla_w_q_up, mla_w_kv_up, mla_q_head_norm, mla_k_head_norm, mla_w_out, ssd_w_in, ssd_conv_w, ssd_conv_b, ssd_dt_bias, ssd_a_log, ssd_d, ssd_norm, ssd_w_out, ffn_w_gate, ffn_w_up, ffn_w_down, final_norm):
    assert x_prompt.shape[1:] == x_sample.shape[1:] and x_prompt.shape[2] == D_MODEL
    assert x_prompt.shape[1] % CHUNK == 0
    depth = mix_norm.shape[0]
    n_prompt = x_prompt.shape[0]
    B = n_prompt + x_sample.shape[0]
    S = x_prompt.shape[1]
    Tp = FRONT_PAD + N_META + S
    head = jnp.concatenate([jnp.zeros((FRONT_PAD, D_MODEL), F32), meta_tokens.astype(F32)], axis=0)
    h = jnp.concatenate([jnp.broadcast_to(head[None], (B, CHUNK, D_MODEL)),
                         jnp.concatenate([x_prompt, x_sample], axis=0)], axis=1)
    tables = _attention_tables(Tp)
    attn_p = (mla_w_in, mla_q_norm, mla_kv_norm, mla_w_q_up, mla_w_kv_up,
              mla_q_head_norm, mla_k_head_norm, mla_w_out)
    ssd_p = (ssd_w_in, ssd_conv_w, ssd_conv_b, ssd_dt_bias, ssd_a_log, ssd_d, ssd_norm, ssd_w_out)
    flat = lambda t: t.reshape(B * Tp, t.shape[-1])
    ia = ib = 0
    for layer in range(depth):
        nw = _row(mix_norm[layer])
        if layer % 2 == 0:
            p = _prep_mla(*[t[ia] for t in attn_p])
            ia += 1
            q, k, v = _mla_proj(h, nw, p, tables)
            o = _flash(q, k, v)
            h2 = _linear_residual(flat(o), p['w_out'], flat(h))
        else:
            p = _prep_ssd(*[t[ib] for t in ssd_p])
            ib += 1
            z, xbc, dt = _ssd_inproj(h, nw, p['w_in'], p['dt_bias'])
            xbc = _ssd_conv(xbc, p['conv_w'], p['conv_b'])
            yf = _ssd_scan(xbc, dt, p['a_log'], reverse=False)
            yb = _ssd_scan(xbc, dt, p['a_log'], reverse=True)
            h2 = _ssd_out(flat(yf), flat(yb), flat(xbc), flat(z), p['d_e'], p['norm'], p['w_out'], flat(h))
        h2 = _ffn(h2, _row(ffn_norm[layer]), ffn_w_gate[layer].astype(BF16), ffn_w_up[layer].astype(BF16),
                  ffn_w_down[layer].astype(BF16), _row(final_norm), final=(layer == depth - 1))
        h = h2.reshape(B, Tp, D_MODEL)
    y = h[:, CHUNK:]
    return (y[:n_prompt], y[n_prompt:])
```

```python
import functools
import math

import jax
import jax.numpy as jnp
from jax import lax
from jax.experimental import pallas as pl
from jax.experimental.pallas import tpu as pltpu

F32 = jnp.float32
BF16 = jnp.bfloat16

D_MODEL = 1024
N_META = 16
EPS = 1e-6
MLA_HEADS = 16
QK_NOPE_DIM = 64
QK_ROPE_DIM = 32
QK_HEAD_DIM = QK_NOPE_DIM + QK_ROPE_DIM
V_HEAD_DIM = 64
Q_LORA_RANK = 384
KV_LORA_RANK = 256
ROPE_BASE = 10000.0
D_INNER = 2 * D_MODEL
SSD_HEAD_DIM = 64
SSD_HEADS = D_INNER // SSD_HEAD_DIM
SSD_GROUPS = 4
HEADS_PER_GROUP = SSD_HEADS // SSD_GROUPS
D_STATE = 128
D_CONV = 5
CONV_DIM = D_INNER + 2 * SSD_GROUPS * D_STATE
CHUNK = 128
FFN_HIDDEN = -(-8 * D_MODEL // (3 * 256)) * 256

LANES = 128
FRONT_PAD = CHUNK - N_META
HEAD_PAD = LANES
ROPE_LANE0 = QK_NOPE_DIM
MASK_LANE = QK_HEAD_DIM
MASK_VALUE = -1e30
LOG2E = 1.0 / math.log(2.0)
SOFTMAX_BOUND_LIMIT = 40.0
HEAD_GROUP = 4
GROUP_W = HEADS_PER_GROUP * SSD_HEAD_DIM
VMEM_LIMIT = 56 * 1024 * 1024


def _pick_tile(n, candidates):
    for c in candidates:
        if n % c == 0:
            return c
    return n


def _params(sem, vmem=VMEM_LIMIT):
    return pltpu.CompilerParams(dimension_semantics=sem, vmem_limit_bytes=vmem)


def _const_spec(shape):
    nd = len(shape)
    return pl.BlockSpec(shape, lambda *_: (0,) * nd, pipeline_mode=pl.Buffered(1))


def _rms(x, w):
    return x * lax.rsqrt(jnp.mean(x * x, axis=-1, keepdims=True) + EPS) * w


def _silu(x):
    return x * (1.0 / (1.0 + jnp.exp(-x)))


def _mla_proj_kernel(h_ref, nw_ref, win_ref, qn_ref, kvn_ref, wq_ref, wk_ref, wv_ref,
                     gq_ref, gkn_ref, gkr_ref, cos_ref, sin_ref, bias_ref, one_ref,
                     q_ref, k_ref, v_ref):
    u = _rms(h_ref[0], nw_ref[...]).astype(BF16)
    c = jnp.dot(u, win_ref[...], preferred_element_type=F32)
    cq = _rms(c[:, :Q_LORA_RANK], qn_ref[...]).astype(BF16)
    ckv = _rms(c[:, Q_LORA_RANK:Q_LORA_RANK + KV_LORA_RANK], kvn_ref[...]).astype(BF16)
    k_rope = c[:, Q_LORA_RANK + KV_LORA_RANK:]
    v_ref[0] =jnp.dot(ckv, wv_ref[...], preferred_element_type=F32).astype(BF16)

    cos = cos_ref[...]
    sin = sin_ref[...]
    lane = lax.broadcasted_iota(jnp.int32, cos.shape, 1)
    half = QK_ROPE_DIM // 2
    first_half = (lane >= ROPE_LANE0) & (lane < ROPE_LANE0 + half)

    def rope(x):
        swapped = jnp.where(first_half, pltpu.roll(x, HEAD_PAD - half, 1), pltpu.roll(x, half, 1))
        return x * cos + swapped * sin

    inv_d = 1.0 / QK_HEAD_DIM
    kr = rope(k_rope * gkr_ref[...])
    kr_ss = jnp.sum(k_rope * k_rope, axis=-1, keepdims=True)
    gq = gq_ref[...]
    gkn = gkn_ref[...]
    bias = bias_ref[...]
    one = one_ref[...]
    q = jnp.dot(cq, wq_ref[...], preferred_element_type=F32)
    k_nope = jnp.dot(ckv, wk_ref[...], preferred_element_type=F32)
    for hd in range(MLA_HEADS):
        sl = slice(hd * HEAD_PAD, (hd + 1) * HEAD_PAD)
        qh = q[:, sl]
        rq = lax.rsqrt(jnp.sum(qh * qh, axis=-1, keepdims=True) * inv_d + EPS)
        q_ref[0, hd] = (rope(qh * rq * gq) + one).astype(BF16)
        kh = k_nope[:, sl]
        rk = lax.rsqrt((jnp.sum(kh * kh, axis=-1, keepdims=True) + kr_ss) * inv_d + EPS)
        k_ref[0, hd] = ((kh * gkn + kr) * rk + bias).astype(BF16)


def _mla_proj(h, nw, p, tables):
    B, Tp, _ = h.shape
    tm = _pick_tile(Tp, (384, 128))
    row = lambda b, i: (b, i, 0)
    tab = lambda b, i: (i, 0)
    qk_shape = jax.ShapeDtypeStruct((B, MLA_HEADS, Tp, HEAD_PAD), BF16)
    qk_spec = pl.BlockSpec((1, MLA_HEADS, tm, HEAD_PAD), lambda b, i: (b, 0, i, 0))
    consts = (nw, p['w_in'], p['q_norm'], p['kv_norm'], p['w_q'], p['w_k'], p['w_v'],
              p['gq'], p['gkn'], p['gkr'])
    return pl.pallas_call(
        _mla_proj_kernel,
        grid=(B, Tp // tm),
        in_specs=[pl.BlockSpec((1, tm, D_MODEL), row)]
        + [_const_spec(a.shape) for a in consts]
        + [pl.BlockSpec((tm, HEAD_PAD), tab)] * 3
        + [_const_spec(tables['one'].shape)],
        out_specs=[qk_spec, qk_spec, pl.BlockSpec((1, tm, MLA_HEADS * V_HEAD_DIM), row)],
        out_shape=[qk_shape, qk_shape, jax.ShapeDtypeStruct((B, Tp, MLA_HEADS * V_HEAD_DIM), BF16)],
        compiler_params=_params(("parallel", "parallel")),
        name="mla_proj",
    )(h, *consts, tables['cos'], tables['sin'], _key_bias(tables, p['bound']), tables['one'])


def _flash_kernel(bound_ref, q_ref, k_ref, v_ref, o_ref, *, key_chunk):
    Tp = k_ref.shape[2]
    tq = q_ref.shape[2]
    head_of_lane = lax.broadcasted_iota(jnp.int32, o_ref.shape[1:], 1) // V_HEAD_DIM
    qk = lambda q, k: lax.dot_general(q, k, (((1,), (1,)), ((), ())), preferred_element_type=F32)

    def attend(bounded):
        out = jnp.zeros(o_ref.shape[1:], F32)
        for j in range(HEAD_GROUP):
            q = q_ref[0, j]
            if bounded:
                l = jnp.zeros((tq, 1), F32)
                o = jnp.zeros(o_ref.shape[1:], F32)
                for c0 in range(0, Tp, key_chunk):
                    p = jnp.exp2(qk(q, k_ref[0, j, c0:c0 + key_chunk, :]))
                    l = l + jnp.sum(p, axis=-1, keepdims=True)
                    o = o + jnp.dot(p.astype(BF16), v_ref[0, c0:c0 + key_chunk, :], preferred_element_type=F32)
            else:
                s = qk(q, k_ref[0, j])
                p = jnp.exp2(s - jnp.max(s, axis=-1, keepdims=True))
                l = jnp.sum(p, axis=-1, keepdims=True)
                o = jnp.dot(p.astype(BF16), v_ref[0], preferred_element_type=F32)
            out = jnp.where(head_of_lane == j, o * (1.0 / l), out)
        o_ref[0] = out.astype(BF16)

    bounded = bound_ref[0] <= SOFTMAX_BOUND_LIMIT

    @pl.when(bounded)
    def _():
        attend(True)

    @pl.when(jnp.logical_not(bounded))
    def _():
        attend(False)


def _flash(bound, q, k, v):
    B, _, Tp, _ = q.shape
    tq = _pick_tile(Tp, (384, 128))
    gw = HEAD_GROUP * V_HEAD_DIM
    return pl.pallas_call(
        functools.partial(_flash_kernel, key_chunk=_pick_tile(Tp, (1408, 384, 128))),
        grid=(B, MLA_HEADS // HEAD_GROUP, Tp // tq),
        in_specs=[pl.BlockSpec(memory_space=pltpu.SMEM),
                  pl.BlockSpec((1, HEAD_GROUP, tq, HEAD_PAD), lambda b, g, i: (b, g, i, 0)),
                  pl.BlockSpec((1, HEAD_GROUP, Tp, HEAD_PAD), lambda b, g, i: (b, g, 0, 0)),
                  pl.BlockSpec((1, Tp, gw), lambda b, g, i: (b, 0, g))],
        out_specs=pl.BlockSpec((1, tq, gw), lambda b, g, i: (b, i, g)),
        out_shape=jax.ShapeDtypeStruct((B, Tp, MLA_HEADS * V_HEAD_DIM), BF16),
        compiler_params=_params(("parallel", "parallel", "parallel")),
        name="flash",
    )(bound, q, k, v)


def _linear_residual_kernel(x_ref, w_ref, h_ref, o_ref):
    o_ref[...] = h_ref[...] + jnp.dot(x_ref[...], w_ref[...], preferred_element_type=F32)


def _linear_residual(x, w, h):
    R, K = x.shape
    tm = _pick_tile(R, (768, 512, 384, 256, 128))
    return pl.pallas_call(
        _linear_residual_kernel,
        grid=(R // tm,),
        in_specs=[pl.BlockSpec((tm, K), lambda i: (i, 0)), _const_spec(w.shape),
                  pl.BlockSpec((tm, D_MODEL), lambda i: (i, 0))],
        out_specs=pl.BlockSpec((tm, D_MODEL), lambda i: (i, 0)),
        out_shape=jax.ShapeDtypeStruct((R, D_MODEL), F32),
        compiler_params=_params(("parallel",)),
        name="linear_residual",
    )(x, w, h)


def _hidden_chunks():
    bounds, start = [], 0
    while start < FFN_HIDDEN:
        stop = min(start + 1536, FFN_HIDDEN)
        bounds.append((start, stop))
        start = stop
    return bounds


def _ffn_kernel(h_ref, nw_ref, wg_ref, wu_ref, wd_ref, fw_ref, o_ref, *, final):
    h = h_ref[...]
    u = _rms(h, nw_ref[...]).astype(BF16)
    acc = h
    for lo, hi in _hidden_chunks():
        g = jnp.dot(u, wg_ref[:, lo:hi], preferred_element_type=F32)
        up = jnp.dot(u, wu_ref[:, lo:hi], preferred_element_type=F32)
        a = (_silu(g) * up).astype(BF16)
        acc = acc + jnp.dot(a, wd_ref[lo:hi, :], preferred_element_type=F32)
    if final:
        acc = _rms(acc, fw_ref[...])
    o_ref[...] = acc


def _ffn(h, nw, wg, wu, wd, fw, final):
    R = h.shape[0]
    tm = _pick_tile(R, (512, 384, 256, 128))
    row = pl.BlockSpec((tm, D_MODEL), lambda i: (i, 0))
    return pl.pallas_call(
        functools.partial(_ffn_kernel, final=final),
        grid=(R // tm,),
        in_specs=[row] + [_const_spec(a.shape) for a in (nw, wg, wu, wd, fw)],
        out_specs=row,
        out_shape=jax.ShapeDtypeStruct((R, D_MODEL), F32),
        compiler_params=_params(("parallel",)),
        name="ffn",
    )(h, nw, wg, wu, wd, fw)


def _ssd_inproj_kernel(h_ref, nw_ref, w_ref, dtb_ref, z_ref, xbc_ref, dt_ref):
    tm = h_ref.shape[1]
    u = _rms(h_ref[0], nw_ref[...]).astype(BF16)
    y = jnp.dot(u, w_ref[...], preferred_element_type=F32)
    row = pl.program_id(1) * tm + lax.broadcasted_iota(jnp.int32, (tm, 1), 0)
    valid = row >= FRONT_PAD
    z_ref[0] = y[:, :D_INNER].astype(BF16)
    xbc_ref[0] = jnp.where(valid, y[:, D_INNER:D_INNER + CONV_DIM], 0.0).astype(BF16)
    dt_raw = y[:, D_INNER + CONV_DIM:] + dtb_ref[...]
    dt = jnp.maximum(dt_raw, 0.0) + jnp.log(1.0 + jnp.exp(-jnp.abs(dt_raw)))
    dt_ref[0] = jnp.where(valid, dt, 0.0)


def _ssd_inproj(h, nw, w, dtb):
    B, Tp, _ = h.shape
    tm = _pick_tile(Tp, (384, 128))
    row = lambda b, i: (b, i, 0)
    return pl.pallas_call(
        _ssd_inproj_kernel,
        grid=(B, Tp // tm),
        in_specs=[pl.BlockSpec((1, tm, D_MODEL), row)] + [_const_spec(a.shape) for a in (nw, w, dtb)],
        out_specs=[pl.BlockSpec((1, tm, D_INNER), row), pl.BlockSpec((1, tm, CONV_DIM), row),
                   pl.BlockSpec((1, tm, LANES), row)],
        out_shape=[jax.ShapeDtypeStruct((B, Tp, D_INNER), BF16),
                   jax.ShapeDtypeStruct((B, Tp, CONV_DIM), BF16),
                   jax.ShapeDtypeStruct((B, Tp, LANES), F32)],
        compiler_params=_params(("parallel", "parallel")),
        name="ssd_inproj",
    )(h, nw, w, dtb)


CONV_MARGIN = 8


def _conv_kernel(x_ref, w_ref, b_ref, o_ref, xs_ref, *, row_chunk):
    Tp, cw = x_ref.shape[1], x_ref.shape[2]
    zeros = jnp.zeros((CONV_MARGIN, cw), F32)
    xs_ref[0:CONV_MARGIN, :] = zeros
    xs_ref[CONV_MARGIN + Tp:, :] = zeros
    xs_ref[CONV_MARGIN:CONV_MARGIN + Tp, :] = x_ref[0].astype(F32)
    w = w_ref[...]
    b = b_ref[...]
    for r0 in range(0, Tp, row_chunk):
        acc = b
        for tap in range(D_CONV):
            lo = r0 + CONV_MARGIN + tap - D_CONV // 2
            acc = acc + w[tap:tap + 1, :] * xs_ref[lo:lo + row_chunk, :]
        o_ref[0, r0:r0 + row_chunk, :] = _silu(acc).astype(BF16)


def _ssd_conv(xbc, w, b):
    B, Tp, C = xbc.shape
    cw = 512
    blk = pl.BlockSpec((1, Tp, cw), lambda bi, ci: (bi, 0, ci))
    return pl.pallas_call(
        functools.partial(_conv_kernel, row_chunk=_pick_tile(Tp, (384, 128))),
        grid=(B, C // cw),
        in_specs=[blk, pl.BlockSpec((D_CONV, cw), lambda bi, ci: (0, ci)),
                  pl.BlockSpec((1, cw), lambda bi, ci: (0, ci))],
        out_specs=blk,
        out_shape=jax.ShapeDtypeStruct((B, Tp, C), BF16),
        scratch_shapes=[pltpu.VMEM((Tp + 2 * CONV_MARGIN, cw), F32)],
        compiler_params=_params(("parallel", "parallel")),
        name="ssd_conv",
    )(xbc, w, b)


def _ssd_chunk(x_ref, b_ref, c_ref, dt_ref, neg_a, y_ref, st_ref, reverse):
    dt = dt_ref[0]
    row = lax.broadcasted_iota(jnp.int32, (CHUNK, CHUNK), 0)
    col = lax.broadcasted_iota(jnp.int32, (CHUNK, CHUNK), 1)
    causal = (col >= row) if reverse else (col <= row)
    cs = jnp.dot(causal.astype(F32), dt * neg_a, preferred_element_type=F32,
                 precision=lax.Precision.HIGHEST)
    total = cs[0:1, :] if reverse else cs[CHUNK - 1:CHUNK, :]
    src_t = (cs - jnp.log2(dt)).T
    out_w_t = (jnp.exp2(total - cs) * dt).T
    e_tot = jnp.exp2(total)
    low = lax.broadcasted_iota(jnp.int32, (CHUNK, LANES), 1) < SSD_HEAD_DIM
    head0 = SSD_HEADS if reverse else 0
    pairs_per_group = HEADS_PER_GROUP // 2
    for g in range(SSD_GROUPS):
        bg = b_ref[0, :, g * D_STATE:(g + 1) * D_STATE]
        cg = c_ref[0, :, g * D_STATE:(g + 1) * D_STATE]
        gram = lax.dot_general(cg, bg, (((1,), (1,)), ((), ())), preferred_element_type=F32)
        cg_f = cg.astype(F32)
        bg_t = bg.astype(F32).T
        for pr in range(pairs_per_group):
            pair = g * pairs_per_group + pr
            lanes = slice(pair * LANES, (pair + 1) * LANES)
            xp = x_ref[0, :, lanes]
            state = st_ref[pair]
            rhs = jnp.concatenate([xp, state.astype(BF16)], axis=0)
            ys, ups = [], []
            for k in range(2):
                hd = head0 + 2 * pair + k
                cs_l = jnp.broadcast_to(cs[:, hd:hd + 1], (CHUNK, CHUNK))
                seg = cs_l - src_t[hd:hd + 1, :]
                within = gram * jnp.exp2(jnp.where(causal, seg, -jnp.inf))
                from_state = cg_f * jnp.exp2(cs_l)
                lhs = jnp.concatenate([within, from_state], axis=1).astype(BF16)
                ys.append(jnp.dot(lhs, rhs, preferred_element_type=F32))
                ups.append(jnp.dot((bg_t * out_w_t[hd:hd + 1, :]).astype(BF16), xp, preferred_element_type=F32))
            y_ref[0, :, lanes] = jnp.where(low, ys[0], ys[1]).astype(y_ref.dtype)
            hd0 = head0 + 2 * pair
            keep = jnp.where(low[0:1, :], jnp.broadcast_to(e_tot[:, hd0:hd0 + 1], (1, LANES)),
                             jnp.broadcast_to(e_tot[:, hd0 + 1:hd0 + 2], (1, LANES)))
            st_ref[pair] = state * keep + jnp.where(low, ups[0], ups[1])


def _ssd_scan_kernel(xf_ref, bf_ref, cf_ref, dtf_ref, xb_ref, bb_ref, cb_ref, dtb_ref, alog_ref,
                     yf_ref, yb_ref, stf_ref, stb_ref):
    @pl.when(pl.program_id(1) == 0)
    def _():
        stf_ref[...] = jnp.zeros(stf_ref.shape, F32)
        stb_ref[...] = jnp.zeros(stb_ref.shape, F32)

    neg_a = -jnp.exp(alog_ref[...]) * LOG2E
    _ssd_chunk(xf_ref, bf_ref, cf_ref, dtf_ref, neg_a, yf_ref, stf_ref, reverse=False)
    _ssd_chunk(xb_ref, bb_ref, cb_ref, dtb_ref, neg_a, yb_ref, stb_ref, reverse=True)


def _ssd_scan(xbc, dt, a_log):
    B, Tp, _ = xbc.shape
    nc = Tp // CHUNK
    bc_w = SSD_GROUPS * D_STATE
    b_blk = D_INNER // bc_w

    def specs(pos):
        return [pl.BlockSpec((1, CHUNK, D_INNER), lambda b, c: (b, pos(c), 0)),
                pl.BlockSpec((1, CHUNK, bc_w), lambda b, c: (b, pos(c), b_blk)),
                pl.BlockSpec((1, CHUNK, bc_w), lambda b, c: (b, pos(c), b_blk + 1)),
                pl.BlockSpec((1, CHUNK, LANES), lambda b, c: (b, pos(c), 0))]

    fwd = lambda c: c
    bwd = lambda c: nc - 1 - c
    y_shape = jax.ShapeDtypeStruct((B, Tp, D_INNER), BF16)
    state = pltpu.VMEM((SSD_HEADS // 2, D_STATE, LANES), F32)
    return pl.pallas_call(
        _ssd_scan_kernel,
        grid=(B, nc),
        in_specs=specs(fwd) + specs(bwd) + [pl.BlockSpec((1, LANES), lambda b, c: (0, 0))],
        out_specs=[pl.BlockSpec((1, CHUNK, D_INNER), lambda b, c: (b, fwd(c), 0)),
                   pl.BlockSpec((1, CHUNK, D_INNER), lambda b, c: (b, bwd(c), 0))],
        out_shape=[y_shape, y_shape],
        scratch_shapes=[state, state],
        compiler_params=_params(("parallel", "arbitrary")),
        name="ssd_scan",
    )(xbc, xbc, xbc, dt, xbc, xbc, xbc, dt, a_log)


def _ssd_out_kernel(yf_ref, yb_ref, x_ref, z_ref, d_ref, nw_ref, w_ref, h_ref, o_ref):
    y = yf_ref[...].astype(F32) + yb_ref[...].astype(F32) + d_ref[...] * x_ref[...].astype(F32)
    y = y * _silu(z_ref[...].astype(F32))
    yn = _rms(y, nw_ref[...]).astype(BF16)
    o_ref[...] = h_ref[...] + jnp.dot(yn, w_ref[...], preferred_element_type=F32)


def _ssd_out(yf, yb, xbc, z, d_e, nw, w, h):
    R = h.shape[0]
    tm = _pick_tile(R, (512, 384, 256, 128))
    wide = pl.BlockSpec((tm, D_INNER), lambda i: (i, 0))
    row = pl.BlockSpec((tm, D_MODEL), lambda i: (i, 0))
    return pl.pallas_call(
        _ssd_out_kernel,
        grid=(R // tm,),
        in_specs=[wide, wide, wide, wide] + [_const_spec(a.shape) for a in (d_e, nw, w)] + [row],
        out_specs=row,
        out_shape=jax.ShapeDtypeStruct((R, D_MODEL), F32),
        compiler_params=_params(("parallel",)),
        name="ssd_out",
    )(yf, yb, xbc, z, d_e, nw, w, h)


def _row(v, width=None):
    v = v.astype(F32).reshape(1, -1)
    if width is not None and v.shape[1] < width:
        v = jnp.pad(v, ((0, 0), (0, width - v.shape[1])))
    return v


def _prep_mla(w_in, q_norm, kv_norm, w_q_up, w_kv_up, q_head_norm, k_head_norm, w_out):
    lat = Q_LORA_RANK + KV_LORA_RANK
    rope_cols = jnp.pad(w_in[:, lat:], ((0, 0), (ROPE_LANE0, HEAD_PAD - QK_HEAD_DIM)))
    w_in_p = jnp.concatenate([w_in[:, :lat], rope_cols], axis=1).astype(BF16)
    w_q = w_q_up.reshape(Q_LORA_RANK, MLA_HEADS, QK_HEAD_DIM)
    w_q = jnp.pad(w_q, ((0, 0), (0, 0), (0, HEAD_PAD - QK_HEAD_DIM))).reshape(Q_LORA_RANK, -1).astype(BF16)
    w_kv = w_kv_up.reshape(KV_LORA_RANK, MLA_HEADS, QK_NOPE_DIM + V_HEAD_DIM)
    w_k = jnp.pad(w_kv[:, :, :QK_NOPE_DIM], ((0, 0), (0, 0), (0, HEAD_PAD - QK_NOPE_DIM)))
    w_k = w_k.reshape(KV_LORA_RANK, -1).astype(BF16)
    w_v = w_kv[:, :, QK_NOPE_DIM:].reshape(KV_LORA_RANK, -1).astype(BF16)
    scale = LOG2E / math.sqrt(QK_HEAD_DIM)
    gq = q_head_norm.astype(F32)
    gk = k_head_norm.astype(F32)
    bound = 1.02 * scale * QK_HEAD_DIM * jnp.max(jnp.abs(gq)) * jnp.max(jnp.abs(gk))
    return dict(
        w_in=w_in_p, q_norm=_row(q_norm), kv_norm=_row(kv_norm), w_q=w_q, w_k=w_k, w_v=w_v,
        bound=bound.reshape(1),
        gq=_row(gq * scale, HEAD_PAD),
        gkn=_row(gk[:QK_NOPE_DIM], HEAD_PAD),
        gkr=_row(jnp.pad(gk[QK_NOPE_DIM:], (ROPE_LANE0, 0)), HEAD_PAD),
        w_out=w_out.astype(BF16))


def _attention_tables(Tp):
    half = QK_ROPE_DIM // 2
    pos = jnp.arange(Tp, dtype=F32) - FRONT_PAD
    inv_freq = ROPE_BASE ** (-jnp.arange(half, dtype=F32) / half)
    ang = pos[:, None] * inv_freq[None, :]
    cos, sin = jnp.cos(ang), jnp.sin(ang)
    tail = HEAD_PAD - QK_HEAD_DIM
    cos_t = jnp.concatenate([jnp.ones((Tp, ROPE_LANE0), F32), cos, cos, jnp.ones((Tp, tail), F32)], axis=1)
    sin_t = jnp.concatenate([jnp.zeros((Tp, ROPE_LANE0), F32), -sin, sin, jnp.zeros((Tp, tail), F32)], axis=1)
    is_pad = (jnp.arange(Tp) < FRONT_PAD)[:, None]
    lane = jnp.arange(HEAD_PAD)[None, :]
    one = (lane == MASK_LANE).astype(F32)
    return dict(cos=cos_t, sin=sin_t, pad_mask=jnp.where(is_pad, MASK_VALUE, 0.0).astype(F32) * one, one=one)


def _key_bias(tables, bound):
    return jnp.where(tables['pad_mask'] < 0, tables['pad_mask'], -bound * tables['one'])


def _prep_ssd(w_in, conv_w, conv_b, dt_bias, a_log, d_skip, norm_w, w_out):
    w_in_p = jnp.pad(w_in, ((0, 0), (0, LANES - 2 * SSD_HEADS))).astype(BF16)
    return dict(
        w_in=w_in_p, conv_w=conv_w.astype(F32), conv_b=_row(conv_b),
        dt_bias=_row(dt_bias, LANES), a_log=_row(a_log, LANES),
        d_e=_row(jnp.repeat(d_skip.astype(F32), SSD_HEAD_DIM)),
        norm=_row(norm_w), w_out=w_out.astype(BF16))


def kernel(x_prompt, x_sample, meta_tokens, mix_norm, ffn_norm, mla_w_in, mla_q_norm, mla_kv_norm, mla_w_q_up, mla_w_kv_up, mla_q_head_norm, mla_k_head_norm, mla_w_out, ssd_w_in, ssd_conv_w, ssd_conv_b, ssd_dt_bias, ssd_a_log, ssd_d, ssd_norm, ssd_w_out, ffn_w_gate, ffn_w_up, ffn_w_down, final_norm):
    assert x_prompt.shape[1:] == x_sample.shape[1:] and x_prompt.shape[2] == D_MODEL
    assert x_prompt.shape[1] % CHUNK == 0
    depth = mix_norm.shape[0]
    n_prompt = x_prompt.shape[0]
    B = n_prompt + x_sample.shape[0]
    S = x_prompt.shape[1]
    Tp = FRONT_PAD + N_META + S
    head = jnp.concatenate([jnp.zeros((FRONT_PAD, D_MODEL), F32), meta_tokens.astype(F32)], axis=0)
    h = jnp.concatenate([jnp.broadcast_to(head[None], (B, CHUNK, D_MODEL)),
                         jnp.concatenate([x_prompt, x_sample], axis=0)], axis=1)
    tables = _attention_tables(Tp)
    attn_p = (mla_w_in, mla_q_norm, mla_kv_norm, mla_w_q_up, mla_w_kv_up,
              mla_q_head_norm, mla_k_head_norm, mla_w_out)
    ssd_p = (ssd_w_in, ssd_conv_w, ssd_conv_b, ssd_dt_bias, ssd_a_log, ssd_d, ssd_norm, ssd_w_out)
    flat = lambda t: t.reshape(B * Tp, t.shape[-1])
    ia = ib = 0
    for layer in range(depth):
        nw = _row(mix_norm[layer])
        if layer % 2 == 0:
            p = _prep_mla(*[t[ia] for t in attn_p])
            ia += 1
            q, k, v = _mla_proj(h, nw, p, tables)
            o = _flash(p['bound'], q, k, v)
            h2 = _linear_residual(flat(o), p['w_out'], flat(h))
        else:
            p = _prep_ssd(*[t[ib] for t in ssd_p])
            ib += 1
            z, xbc, dt = _ssd_inproj(h, nw, p['w_in'], p['dt_bias'])
            xbc = _ssd_conv(xbc, p['conv_w'], p['conv_b'])
            yf, yb = _ssd_scan(xbc, dt, p['a_log'])
            h2 = _ssd_out(flat(yf), flat(yb), flat(xbc), flat(z), p['d_e'], p['norm'], p['w_out'], flat(h))
        h2 = _ffn(h2, _row(ffn_norm[layer]), ffn_w_gate[layer].astype(BF16), ffn_w_up[layer].astype(BF16),
                  ffn_w_down[layer].astype(BF16), _row(final_norm), final=(layer == depth - 1))
        h = h2.reshape(B, Tp, D_MODEL)
    y = h[:, CHUNK:]
    return (y[:n_prompt], y[n_prompt:])
```

```python
import functools
import math

import jax
import jax.numpy as jnp
from jax import lax
from jax.experimental import pallas as pl
from jax.experimental.pallas import tpu as pltpu

F32 = jnp.float32
BF16 = jnp.bfloat16

D_MODEL = 1024
N_META = 16
EPS = 1e-6
MLA_HEADS = 16
QK_NOPE_DIM = 64
QK_ROPE_DIM = 32
QK_HEAD_DIM = QK_NOPE_DIM + QK_ROPE_DIM
V_HEAD_DIM = 64
Q_LORA_RANK = 384
KV_LORA_RANK = 256
ROPE_BASE = 10000.0
D_INNER = 2 * D_MODEL
SSD_HEAD_DIM = 64
SSD_HEADS = D_INNER // SSD_HEAD_DIM
SSD_GROUPS = 4
HEADS_PER_GROUP = SSD_HEADS // SSD_GROUPS
D_STATE = 128
D_CONV = 5
CONV_DIM = D_INNER + 2 * SSD_GROUPS * D_STATE
CHUNK = 128
FFN_HIDDEN = -(-8 * D_MODEL // (3 * 256)) * 256

LANES = 128
FRONT_PAD = CHUNK - N_META
HEAD_PAD = LANES
ROPE_LANE0 = QK_NOPE_DIM
MASK_LANE = QK_HEAD_DIM
MASK_VALUE = -1e30
LOG2E = 1.0 / math.log(2.0)
SOFTMAX_BOUND_LIMIT = 40.0
HEAD_GROUP = 4
GROUP_W = HEADS_PER_GROUP * SSD_HEAD_DIM
VMEM_LIMIT = 56 * 1024 * 1024


def _pick_tile(n, candidates):
    for c in candidates:
        if n % c == 0:
            return c
    return n


def _params(sem, vmem=VMEM_LIMIT):
    return pltpu.CompilerParams(dimension_semantics=sem, vmem_limit_bytes=vmem)


def _const_spec(shape):
    nd = len(shape)
    return pl.BlockSpec(shape, lambda *_: (0,) * nd, pipeline_mode=pl.Buffered(1))


def _rms(x, w):
    return x * lax.rsqrt(jnp.mean(x * x, axis=-1, keepdims=True) + EPS) * w


def _silu(x):
    return x * (1.0 / (1.0 + jnp.exp(-x)))


def _mla_proj_kernel(h_ref, nw_ref, win_ref, qn_ref, kvn_ref, wq_ref, wk_ref, wv_ref,
                     gq_ref, gkn_ref, gkr_ref, cos_ref, sin_ref, bias_ref, one_ref,
                     q_ref, k_ref, v_ref):
    u = _rms(h_ref[0], nw_ref[...]).astype(BF16)
    c = jnp.dot(u, win_ref[...], preferred_element_type=F32)
    cq = _rms(c[:, :Q_LORA_RANK], qn_ref[...]).astype(BF16)
    ckv = _rms(c[:, Q_LORA_RANK:Q_LORA_RANK + KV_LORA_RANK], kvn_ref[...]).astype(BF16)
    k_rope = c[:, Q_LORA_RANK + KV_LORA_RANK:]
    q = jnp.dot(cq, wq_ref[...], preferred_element_type=F32)
    k_nope = jnp.dot(ckv, wk_ref[...], preferred_element_type=F32)
    v_ref[0] = lax.dot_general(wv_ref[...], ckv, (((1,), (1,)), ((), ())),
                               preferred_element_type=F32).astype(BF16)

    cos = cos_ref[...]
    sin = sin_ref[...]
    lane = lax.broadcasted_iota(jnp.int32, cos.shape, 1)
    half = QK_ROPE_DIM // 2
    first_half = (lane >= ROPE_LANE0) & (lane < ROPE_LANE0 + half)

    def rope(x):
        swapped = jnp.where(first_half, pltpu.roll(x, HEAD_PAD - half, 1), pltpu.roll(x, half, 1))
        return x * cos + swapped * sin

    inv_d = 1.0 / QK_HEAD_DIM
    kr = rope(k_rope * gkr_ref[...])
    kr_ss = jnp.sum(k_rope * k_rope, axis=-1, keepdims=True)
    gq = gq_ref[...]
    gkn = gkn_ref[...]
    bias = bias_ref[...]
    one = one_ref[...]
    for hd in range(MLA_HEADS):
        sl = slice(hd * HEAD_PAD, (hd + 1) * HEAD_PAD)
        qh = q[:, sl]
        rq = lax.rsqrt(jnp.sum(qh * qh, axis=-1, keepdims=True) * inv_d + EPS)
        q_ref[0, hd] = (rope(qh * rq * gq) + one).astype(BF16)
        kh = k_nope[:, sl]
        rk = lax.rsqrt((jnp.sum(kh * kh, axis=-1, keepdims=True) + kr_ss) * inv_d + EPS)
        k_ref[0, hd] = ((kh * gkn + kr) * rk + bias).astype(BF16)


def _mla_proj(h, nw, p, tables):
    B, Tp, _ = h.shape
    tm = _pick_tile(Tp, (384, 128))
    row = lambda b, i: (b, i, 0)
    tab = lambda b, i: (i, 0)
    qk_shape = jax.ShapeDtypeStruct((B, MLA_HEADS, Tp, HEAD_PAD), BF16)
    qk_spec = pl.BlockSpec((1, MLA_HEADS, tm, HEAD_PAD), lambda b, i: (b, 0, i, 0))
    consts = (nw, p['w_in'], p['q_norm'], p['kv_norm'], p['w_q'], p['w_k'], p['w_v'],
              p['gq'], p['gkn'], p['gkr'])
    return pl.pallas_call(
        _mla_proj_kernel,
        grid=(B, Tp // tm),
        in_specs=[pl.BlockSpec((1, tm, D_MODEL), row)]
        + [_const_spec(a.shape) for a in consts]
        + [pl.BlockSpec((tm, HEAD_PAD), tab)] * 3
        + [_const_spec(tables['one'].shape)],
        out_specs=[qk_spec, qk_spec, pl.BlockSpec((1, MLA_HEADS * V_HEAD_DIM, tm), lambda b, i: (b, 0, i))],
        out_shape=[qk_shape, qk_shape, jax.ShapeDtypeStruct((B, MLA_HEADS * V_HEAD_DIM, Tp), BF16)],
        compiler_params=_params(("parallel", "parallel")),
        name="mla_proj",
    )(h, *consts, tables['cos'], tables['sin'], _key_bias(tables, p['bound']), tables['one'])


def _flash_kernel(bound_ref, q_ref, k_ref, v_ref, o_ref, *, key_chunk):
    Tp = k_ref.shape[2]
    tq = q_ref.shape[2]
    kq = lambda k, q: lax.dot_general(k, q, (((1,), (1,)), ((), ())), preferred_element_type=F32)

    def attend(bounded):
        for j in range(HEAD_GROUP):
            q = q_ref[0, j]
            rows = slice(j * V_HEAD_DIM, (j + 1) * V_HEAD_DIM)
            if bounded:
                l = jnp.zeros((1, tq), F32)
                o = jnp.zeros((V_HEAD_DIM, tq), F32)
                for c0 in range(0, Tp, key_chunk):
                    p = jnp.exp2(kq(k_ref[0, j, c0:c0 + key_chunk, :], q))
                    l = l + jnp.sum(p, axis=0, keepdims=True)
                    o = o + jnp.dot(v_ref[0, rows, c0:c0 + key_chunk], p.astype(BF16),
                                    preferred_element_type=F32)
            else:
                s = kq(k_ref[0, j], q)
                p = jnp.exp2(s - jnp.max(s, axis=0, keepdims=True))
                l = jnp.sum(p, axis=0, keepdims=True)
                o = jnp.dot(v_ref[0, rows, :], p.astype(BF16), preferred_element_type=F32)
            o_ref[0, rows, :] = (o * (1.0 / l)).astype(BF16)

    bounded = bound_ref[0] <= SOFTMAX_BOUND_LIMIT

    @pl.when(bounded)
    def _():
        attend(True)

    @pl.when(jnp.logical_not(bounded))
    def _():
        attend(False)


def _flash(bound, q, k, v):
    B, _, Tp, _ = q.shape
    tq = _pick_tile(Tp, (384, 128))
    gw = HEAD_GROUP * V_HEAD_DIM
    return pl.pallas_call(
        functools.partial(_flash_kernel, key_chunk=_pick_tile(Tp, (1408, 384, 128))),
        grid=(B, MLA_HEADS // HEAD_GROUP, Tp // tq),
        in_specs=[pl.BlockSpec(memory_space=pltpu.SMEM),
                  pl.BlockSpec((1, HEAD_GROUP, tq, HEAD_PAD), lambda b, g, i: (b, g, i, 0)),
                  pl.BlockSpec((1, HEAD_GROUP, Tp, HEAD_PAD), lambda b, g, i: (b, g, 0, 0)),
                  pl.BlockSpec((1, gw, Tp), lambda b, g, i: (b, g, 0))],
        out_specs=pl.BlockSpec((1, gw, tq), lambda b, g, i: (b, g, i)),
        out_shape=jax.ShapeDtypeStruct((B, MLA_HEADS * V_HEAD_DIM, Tp), BF16),
        compiler_params=_params(("parallel", "parallel", "parallel")),
        name="flash",
    )(bound, q, k, v)


def _attn_out_kernel(ot_ref, w_ref, h_ref, o_ref):
    proj = lax.dot_general(ot_ref[0], w_ref[...], (((0,), (0,)), ((), ())), preferred_element_type=F32)
    o_ref[0] = h_ref[0] + proj


def _attn_out(o_t, w, h):
    B, Tp, _ = h.shape
    tm = _pick_tile(Tp, (384, 128))
    row = pl.BlockSpec((1, tm, D_MODEL), lambda b, i: (b, i, 0))
    return pl.pallas_call(
        _attn_out_kernel,
        grid=(B, Tp // tm),
        in_specs=[pl.BlockSpec((1, o_t.shape[1], tm), lambda b, i: (b, 0, i)), _const_spec(w.shape), row],
        out_specs=row,
        out_shape=jax.ShapeDtypeStruct((B, Tp, D_MODEL), F32),
        compiler_params=_params(("parallel", "parallel")),
        name="attn_out",
    )(o_t, w, h)


def _hidden_chunks():
    bounds, start = [], 0
    while start < FFN_HIDDEN:
        stop = min(start + 1536, FFN_HIDDEN)
        bounds.append((start, stop))
        start = stop
    return bounds


def _ffn_kernel(h_ref, nw_ref, wg_ref, wu_ref, wd_ref, fw_ref, o_ref, *, final):
    h = h_ref[...]
    u = _rms(h, nw_ref[...]).astype(BF16)
    acc = h
    for lo, hi in _hidden_chunks():
        g = jnp.dot(u, wg_ref[:, lo:hi], preferred_element_type=F32)
        up = jnp.dot(u, wu_ref[:, lo:hi], preferred_element_type=F32)
        a = (_silu(g) * up).astype(BF16)
        acc = acc + jnp.dot(a, wd_ref[lo:hi, :], preferred_element_type=F32)
    if final:
        acc = _rms(acc, fw_ref[...])
    o_ref[...] = acc


def _ffn(h, nw, wg, wu, wd, fw, final):
    R = h.shape[0]
    tm = _pick_tile(R, (512, 384, 256, 128))
    row = pl.BlockSpec((tm, D_MODEL), lambda i: (i, 0))
    return pl.pallas_call(
        functools.partial(_ffn_kernel, final=final),
        grid=(R // tm,),
        in_specs=[row] + [_const_spec(a.shape) for a in (nw, wg, wu, wd, fw)],
        out_specs=row,
        out_shape=jax.ShapeDtypeStruct((R, D_MODEL), F32),
        compiler_params=_params(("parallel",)),
        name="ffn",
    )(h, nw, wg, wu, wd, fw)


def _ssd_inproj_kernel(h_ref, nw_ref, w_ref, dtb_ref, z_ref, xbc_ref, dt_ref):
    tm = h_ref.shape[1]
    u = _rms(h_ref[0], nw_ref[...]).astype(BF16)
    y = jnp.dot(u, w_ref[...], preferred_element_type=F32)
    row = pl.program_id(1) * tm + lax.broadcasted_iota(jnp.int32, (tm, 1), 0)
    valid = row >= FRONT_PAD
    z_ref[0] = y[:, :D_INNER].astype(BF16)
    xbc_ref[0] = jnp.where(valid, y[:, D_INNER:D_INNER + CONV_DIM], 0.0).astype(BF16)
    dt_raw = y[:, D_INNER + CONV_DIM:] + dtb_ref[...]
    dt = jnp.maximum(dt_raw, 0.0) + jnp.log(1.0 + jnp.exp(-jnp.abs(dt_raw)))
    dt_ref[0] = jnp.where(valid, dt, 0.0)


def _ssd_inproj(h, nw, w, dtb):
    B, Tp, _ = h.shape
    tm = _pick_tile(Tp, (384, 128))
    row = lambda b, i: (b, i, 0)
    return pl.pallas_call(
        _ssd_inproj_kernel,
        grid=(B, Tp // tm),
        in_specs=[pl.BlockSpec((1, tm, D_MODEL), row)] + [_const_spec(a.shape) for a in (nw, w, dtb)],
        out_specs=[pl.BlockSpec((1, tm, D_INNER), row), pl.BlockSpec((1, tm, CONV_DIM), row),
                   pl.BlockSpec((1, tm, LANES), row)],
        out_shape=[jax.ShapeDtypeStruct((B, Tp, D_INNER), BF16),
                   jax.ShapeDtypeStruct((B, Tp, CONV_DIM), BF16),
                   jax.ShapeDtypeStruct((B, Tp, LANES), F32)],
        compiler_params=_params(("parallel", "parallel")),
        name="ssd_inproj",
    )(h, nw, w, dtb)


CONV_MARGIN = 8


def _conv_kernel(x_ref, w_ref, b_ref, o_ref, xs_ref, *, row_chunk):
    Tp, cw = x_ref.shape[1], x_ref.shape[2]
    zeros = jnp.zeros((CONV_MARGIN, cw), F32)
    xs_ref[0:CONV_MARGIN, :] = zeros
    xs_ref[CONV_MARGIN + Tp:, :] = zeros
    xs_ref[CONV_MARGIN:CONV_MARGIN + Tp, :] = x_ref[0].astype(F32)
    w = w_ref[...]
    b = b_ref[...]
    for r0 in range(0, Tp, row_chunk):
        acc = b
        for tap in range(D_CONV):
            lo = r0 + CONV_MARGIN + tap - D_CONV // 2
            acc = acc + w[tap:tap + 1, :] * xs_ref[lo:lo + row_chunk, :]
        o_ref[0, r0:r0 + row_chunk, :] = _silu(acc).astype(BF16)


def _ssd_conv(xbc, w, b):
    B, Tp, C = xbc.shape
    cw = 512
    blk = pl.BlockSpec((1, Tp, cw), lambda bi, ci: (bi, 0, ci))
    return pl.pallas_call(
        functools.partial(_conv_kernel, row_chunk=_pick_tile(Tp, (384, 128))),
        grid=(B, C // cw),
        in_specs=[blk, pl.BlockSpec((D_CONV, cw), lambda bi, ci: (0, ci)),
                  pl.BlockSpec((1, cw), lambda bi, ci: (0, ci))],
        out_specs=blk,
        out_shape=jax.ShapeDtypeStruct((B, Tp, C), BF16),
        scratch_shapes=[pltpu.VMEM((Tp + 2 * CONV_MARGIN, cw), F32)],
        compiler_params=_params(("parallel", "parallel")),
        name="ssd_conv",
    )(xbc, w, b)


def _ssd_chunk(x_ref, b_ref, c_ref, dt_ref, neg_a, y_ref, st_ref, reverse):
    dt = dt_ref[0]
    row = lax.broadcasted_iota(jnp.int32, (CHUNK, CHUNK), 0)
    col = lax.broadcasted_iota(jnp.int32, (CHUNK, CHUNK), 1)
    causal = (col >= row) if reverse else (col <= row)
    cs = jnp.dot(causal.astype(F32), dt * neg_a, preferred_element_type=F32,
                 precision=lax.Precision.HIGHEST)
    total = cs[0:1, :] if reverse else cs[CHUNK - 1:CHUNK, :]
    src_t = (cs - jnp.log2(dt)).T
    out_w_t = (jnp.exp2(total - cs) * dt).T
    e_tot = jnp.exp2(total)
    low = lax.broadcasted_iota(jnp.int32, (CHUNK, LANES), 1) < SSD_HEAD_DIM
    head0 = SSD_HEADS if reverse else 0
    pairs_per_group = HEADS_PER_GROUP // 2
    for g in range(SSD_GROUPS):
        bg = b_ref[0, :, g * D_STATE:(g + 1) * D_STATE]
        cg = c_ref[0, :, g * D_STATE:(g + 1) * D_STATE]
        gram = lax.dot_general(cg, bg, (((1,), (1,)), ((), ())), preferred_element_type=F32)
        cg_f = cg.astype(F32)
        bg_t = bg.astype(F32).T
        for pr in range(pairs_per_group):
            pair = g * pairs_per_group + pr
            lanes = slice(pair * LANES, (pair + 1) * LANES)
            xp = x_ref[0, :, lanes]
            state = st_ref[pair]
            rhs = jnp.concatenate([xp, state.astype(BF16)], axis=0)
            ys, ups = [], []
            for k in range(2):
                hd = head0 + 2 * pair + k
                cs_l = jnp.broadcast_to(cs[:, hd:hd + 1], (CHUNK, CHUNK))
                seg = cs_l - src_t[hd:hd + 1, :]
                within = gram * jnp.exp2(jnp.where(causal, seg, -jnp.inf))
                from_state = cg_f * jnp.exp2(cs_l)
                lhs = jnp.concatenate([within, from_state], axis=1).astype(BF16)
                ys.append(jnp.dot(lhs, rhs, preferred_element_type=F32))
                ups.append(jnp.dot((bg_t * out_w_t[hd:hd + 1, :]).astype(BF16), xp, preferred_element_type=F32))
            y_ref[0, :, lanes] = jnp.where(low, ys[0], ys[1]).astype(y_ref.dtype)
            hd0 = head0 + 2 * pair
            keep = jnp.where(low[0:1, :], jnp.broadcast_to(e_tot[:, hd0:hd0 + 1], (1, LANES)),
                             jnp.broadcast_to(e_tot[:, hd0 + 1:hd0 + 2], (1, LANES)))
            st_ref[pair] = state * keep + jnp.where(low, ups[0], ups[1])


def _ssd_scan_kernel(xf_ref, bf_ref, cf_ref, dtf_ref, xb_ref, bb_ref, cb_ref, dtb_ref, alog_ref,
                     yf_ref, yb_ref, stf_ref, stb_ref):
    @pl.when(pl.program_id(1) == 0)
    def _():
        stf_ref[...] = jnp.zeros(stf_ref.shape, F32)
        stb_ref[...] = jnp.zeros(stb_ref.shape, F32)

    neg_a = -jnp.exp(alog_ref[...]) * LOG2E
    _ssd_chunk(xf_ref, bf_ref, cf_ref, dtf_ref, neg_a, yf_ref, stf_ref, reverse=False)
    _ssd_chunk(xb_ref, bb_ref, cb_ref, dtb_ref, neg_a, yb_ref, stb_ref, reverse=True)


def _ssd_scan(xbc, dt, a_log):
    B, Tp, _ = xbc.shape
    nc = Tp // CHUNK
    bc_w = SSD_GROUPS * D_STATE
    b_blk = D_INNER // bc_w

    def specs(pos):
        return [pl.BlockSpec((1, CHUNK, D_INNER), lambda b, c: (b, pos(c), 0)),
                pl.BlockSpec((1, CHUNK, bc_w), lambda b, c: (b, pos(c), b_blk)),
                pl.BlockSpec((1, CHUNK, bc_w), lambda b, c: (b, pos(c), b_blk + 1)),
                pl.BlockSpec((1, CHUNK, LANES), lambda b, c: (b, pos(c), 0))]

    fwd = lambda c: c
    bwd = lambda c: nc - 1 - c
    y_shape = jax.ShapeDtypeStruct((B, Tp, D_INNER), BF16)
    state = pltpu.VMEM((SSD_HEADS // 2, D_STATE, LANES), F32)
    return pl.pallas_call(
        _ssd_scan_kernel,
        grid=(B, nc),
        in_specs=specs(fwd) + specs(bwd) + [pl.BlockSpec((1, LANES), lambda b, c: (0, 0))],
        out_specs=[pl.BlockSpec((1, CHUNK, D_INNER), lambda b, c: (b, fwd(c), 0)),
                   pl.BlockSpec((1, CHUNK, D_INNER), lambda b, c: (b, bwd(c), 0))],
        out_shape=[y_shape, y_shape],
        scratch_shapes=[state, state],
        compiler_params=_params(("parallel", "arbitrary")),
        name="ssd_scan",
    )(xbc, xbc, xbc, dt, xbc, xbc, xbc, dt, a_log)


def _ssd_out_kernel(yf_ref, yb_ref, x_ref, z_ref, d_ref, nw_ref, w_ref, h_ref, o_ref):
    y = yf_ref[...].astype(F32) + yb_ref[...].astype(F32) + d_ref[...] * x_ref[...].astype(F32)
    y = y * _silu(z_ref[...].astype(F32))
    yn = _rms(y, nw_ref[...]).astype(BF16)
    o_ref[...] = h_ref[...] + jnp.dot(yn, w_ref[...], preferred_element_type=F32)


def _ssd_out(yf, yb, xbc, z, d_e, nw, w, h):
    R = h.shape[0]
    tm = _pick_tile(R, (512, 384, 256, 128))
    wide = pl.BlockSpec((tm, D_INNER), lambda i: (i, 0))
    row = pl.BlockSpec((tm, D_MODEL), lambda i: (i, 0))
    return pl.pallas_call(
        _ssd_out_kernel,
        grid=(R // tm,),
        in_specs=[wide, wide, wide, wide] + [_const_spec(a.shape) for a in (d_e, nw, w)] + [row],
        out_specs=row,
        out_shape=jax.ShapeDtypeStruct((R, D_MODEL), F32),
        compiler_params=_params(("parallel",)),
        name="ssd_out",
    )(yf, yb, xbc, z, d_e, nw, w, h)


def _row(v, width=None):
    v = v.astype(F32).reshape(1, -1)
    if width is not None and v.shape[1] < width:
        v = jnp.pad(v, ((0, 0), (0, width - v.shape[1])))
    return v


def _prep_mla(w_in, q_norm, kv_norm, w_q_up, w_kv_up, q_head_norm, k_head_norm, w_out):
    lat = Q_LORA_RANK + KV_LORA_RANK
    rope_cols = jnp.pad(w_in[:, lat:], ((0, 0), (ROPE_LANE0, HEAD_PAD - QK_HEAD_DIM)))
    w_in_p = jnp.concatenate([w_in[:, :lat], rope_cols], axis=1).astype(BF16)
    w_q = w_q_up.reshape(Q_LORA_RANK, MLA_HEADS, QK_HEAD_DIM)
    w_q = jnp.pad(w_q, ((0, 0), (0, 0), (0, HEAD_PAD - QK_HEAD_DIM))).reshape(Q_LORA_RANK, -1).astype(BF16)
    w_kv = w_kv_up.reshape(KV_LORA_RANK, MLA_HEADS, QK_NOPE_DIM + V_HEAD_DIM)
    w_k = jnp.pad(w_kv[:, :, :QK_NOPE_DIM], ((0, 0), (0, 0), (0, HEAD_PAD - QK_NOPE_DIM)))
    w_k = w_k.reshape(KV_LORA_RANK, -1).astype(BF16)
    w_v = w_kv[:, :, QK_NOPE_DIM:].reshape(KV_LORA_RANK, -1).T.astype(BF16)
    scale = LOG2E / math.sqrt(QK_HEAD_DIM)
    gq = q_head_norm.astype(F32)
    gk = k_head_norm.astype(F32)
    bound = 1.02 * scale * QK_HEAD_DIM * jnp.max(jnp.abs(gq)) * jnp.max(jnp.abs(gk))
    return dict(
        w_in=w_in_p, q_norm=_row(q_norm), kv_norm=_row(kv_norm), w_q=w_q, w_k=w_k, w_v=w_v,
        bound=bound.reshape(1),
        gq=_row(gq * scale, HEAD_PAD),
        gkn=_row(gk[:QK_NOPE_DIM], HEAD_PAD),
        gkr=_row(jnp.pad(gk[QK_NOPE_DIM:], (ROPE_LANE0, 0)), HEAD_PAD),
        w_out=w_out.astype(BF16))


def _attention_tables(Tp):
    half = QK_ROPE_DIM // 2
    pos = jnp.arange(Tp, dtype=F32) - FRONT_PAD
    inv_freq = ROPE_BASE ** (-jnp.arange(half, dtype=F32) / half)
    ang = pos[:, None] * inv_freq[None, :]
    cos, sin = jnp.cos(ang), jnp.sin(ang)
    tail = HEAD_PAD - QK_HEAD_DIM
    cos_t = jnp.concatenate([jnp.ones((Tp, ROPE_LANE0), F32), cos, cos, jnp.ones((Tp, tail), F32)], axis=1)
    sin_t = jnp.concatenate([jnp.zeros((Tp, ROPE_LANE0), F32), -sin, sin, jnp.zeros((Tp, tail), F32)], axis=1)
    is_pad = (jnp.arange(Tp) < FRONT_PAD)[:, None]
    lane = jnp.arange(HEAD_PAD)[None, :]
    one = (lane == MASK_LANE).astype(F32)
    return dict(cos=cos_t, sin=sin_t, pad_mask=jnp.where(is_pad, MASK_VALUE, 0.0).astype(F32) * one, one=one)


def _key_bias(tables, bound):
    return jnp.where(tables['pad_mask'] < 0, tables['pad_mask'], -bound * tables['one'])


def _prep_ssd(w_in, conv_w, conv_b, dt_bias, a_log, d_skip, norm_w, w_out):
    w_in_p = jnp.pad(w_in, ((0, 0), (0, LANES - 2 * SSD_HEADS))).astype(BF16)
    return dict(
        w_in=w_in_p, conv_w=conv_w.astype(F32), conv_b=_row(conv_b),
        dt_bias=_row(dt_bias, LANES), a_log=_row(a_log, LANES),
        d_e=_row(jnp.repeat(d_skip.astype(F32), SSD_HEAD_DIM)),
        norm=_row(norm_w), w_out=w_out.astype(BF16))


def kernel(x_prompt, x_sample, meta_tokens, mix_norm, ffn_norm, mla_w_in, mla_q_norm, mla_kv_norm, mla_w_q_up, mla_w_kv_up, mla_q_head_norm, mla_k_head_norm, mla_w_out, ssd_w_in, ssd_conv_w, ssd_conv_b, ssd_dt_bias, ssd_a_log, ssd_d, ssd_norm, ssd_w_out, ffn_w_gate, ffn_w_up, ffn_w_down, final_norm):
    assert x_prompt.shape[1:] == x_sample.shape[1:] and x_prompt.shape[2] == D_MODEL
    assert x_prompt.shape[1] % CHUNK == 0
    depth = mix_norm.shape[0]
    n_prompt = x_prompt.shape[0]
    B = n_prompt + x_sample.shape[0]
    S = x_prompt.shape[1]
    Tp = FRONT_PAD + N_META + S
    head = jnp.concatenate([jnp.zeros((FRONT_PAD, D_MODEL), F32), meta_tokens.astype(F32)], axis=0)
    h = jnp.concatenate([jnp.broadcast_to(head[None], (B, CHUNK, D_MODEL)),
                         jnp.concatenate([x_prompt, x_sample], axis=0)], axis=1)
    tables = _attention_tables(Tp)
    attn_p = (mla_w_in, mla_q_norm, mla_kv_norm, mla_w_q_up, mla_w_kv_up,
              mla_q_head_norm, mla_k_head_norm, mla_w_out)
    ssd_p = (ssd_w_in, ssd_conv_w, ssd_conv_b, ssd_dt_bias, ssd_a_log, ssd_d, ssd_norm, ssd_w_out)
    flat = lambda t: t.reshape(B * Tp, t.shape[-1])
    ia = ib = 0
    for layer in range(depth):
        nw = _row(mix_norm[layer])
        if layer % 2 == 0:
            p = _prep_mla(*[t[ia] for t in attn_p])
            ia += 1
            q, k, v = _mla_proj(h, nw, p, tables)
            o = _flash(p['bound'], q, k, v)
            h2 = flat(_attn_out(o, p['w_out'], h))
        else:
            p = _prep_ssd(*[t[ib] for t in ssd_p])
            ib += 1
            z, xbc, dt = _ssd_inproj(h, nw, p['w_in'], p['dt_bias'])
            xbc = _ssd_conv(xbc, p['conv_w'], p['conv_b'])
            yf, yb = _ssd_scan(xbc, dt, p['a_log'])
            h2 = _ssd_out(flat(yf), flat(yb), flat(xbc), flat(z), p['d_e'], p['norm'], p['w_out'], flat(h))
        h2 = _ffn(h2, _row(ffn_norm[layer]), ffn_w_gate[layer].astype(BF16), ffn_w_up[layer].astype(BF16),
                  ffn_w_down[layer].astype(BF16), _row(final_norm), final=(layer == depth - 1))
        h = h2.reshape(B, Tp, D_MODEL)
    y = h[:, CHUNK:]
    return (y[:n_prompt], y[n_prompt:])
```

```python
import functools
import math

import jax
import jax.numpy as jnp
from jax import lax
from jax.experimental import pallas as pl
from jax.experimental.pallas import tpu as pltpu

F32 = jnp.float32
BF16 = jnp.bfloat16

D_MODEL = 1024
N_META = 16
EPS = 1e-6
MLA_HEADS = 16
QK_NOPE_DIM = 64
QK_ROPE_DIM = 32
QK_HEAD_DIM = QK_NOPE_DIM + QK_ROPE_DIM
V_HEAD_DIM = 64
Q_LORA_RANK = 384
KV_LORA_RANK = 256
ROPE_BASE = 10000.0
D_INNER = 2 * D_MODEL
SSD_HEAD_DIM = 64
SSD_HEADS = D_INNER // SSD_HEAD_DIM
SSD_GROUPS = 4
HEADS_PER_GROUP = SSD_HEADS // SSD_GROUPS
D_STATE = 128
D_CONV = 5
CONV_DIM = D_INNER + 2 * SSD_GROUPS * D_STATE
CHUNK = 128
FFN_HIDDEN = -(-8 * D_MODEL // (3 * 256)) * 256

LANES = 128
FRONT_PAD = CHUNK - N_META
HEAD_PAD = LANES
ROPE_LANE0 = QK_NOPE_DIM
MASK_LANE = QK_HEAD_DIM
MASK_VALUE = -1e30
LOG2E = 1.0 / math.log(2.0)
SOFTMAX_BOUND_LIMIT = 40.0
HEAD_GROUP = 4
GROUP_W = HEADS_PER_GROUP * SSD_HEAD_DIM
VMEM_LIMIT = 56 * 1024 * 1024


def _pick_tile(n, candidates):
    for c in candidates:
        if n % c == 0:
            return c
    return n


def _params(sem, vmem=VMEM_LIMIT):
    return pltpu.CompilerParams(dimension_semantics=sem, vmem_limit_bytes=vmem)


def _const_spec(shape):
    nd = len(shape)
    return pl.BlockSpec(shape, lambda *_: (0,) * nd, pipeline_mode=pl.Buffered(1))


def _rms(x, w):
    return x * lax.rsqrt(jnp.mean(x * x, axis=-1, keepdims=True) + EPS) * w


def _silu(x):
    return x * (1.0 / (1.0 + jnp.exp(-x)))


def _mla_proj_kernel(h_ref, nw_ref, win_ref, qn_ref, kvn_ref, wq_ref, wk_ref, wv_ref,
                     gq_ref, gkn_ref, gkr_ref, cos_ref, sin_ref, bias_ref, one_ref,
                     q_ref, k_ref, v_ref):
    u = _rms(h_ref[0], nw_ref[...]).astype(BF16)
    c = jnp.dot(u, win_ref[...], preferred_element_type=F32)
    cq = _rms(c[:, :Q_LORA_RANK], qn_ref[...]).astype(BF16)
    ckv = _rms(c[:, Q_LORA_RANK:Q_LORA_RANK + KV_LORA_RANK], kvn_ref[...]).astype(BF16)
    k_rope = c[:, Q_LORA_RANK + KV_LORA_RANK:]
    q = jnp.dot(cq, wq_ref[...], preferred_element_type=F32)
    k_nope = jnp.dot(ckv, wk_ref[...], preferred_element_type=F32)
    v_ref[0] = lax.dot_general(wv_ref[...], ckv, (((1,), (1,)), ((), ())),
                               preferred_element_type=F32).astype(BF16)

    cos = cos_ref[...]
    sin = sin_ref[...]
    lane = lax.broadcasted_iota(jnp.int32, cos.shape, 1)
    half = QK_ROPE_DIM // 2
    first_half = (lane >= ROPE_LANE0) & (lane < ROPE_LANE0 + half)

    def rope(x):
        swapped = jnp.where(first_half, pltpu.roll(x, HEAD_PAD - half, 1), pltpu.roll(x, half, 1))
        return x * cos + swapped * sin

    inv_d = 1.0 / QK_HEAD_DIM
    kr = rope(k_rope * gkr_ref[...])
    kr_ss = jnp.sum(k_rope * k_rope, axis=-1, keepdims=True)
    gq = gq_ref[...]
    gkn = gkn_ref[...]
    bias = bias_ref[...]
    one = one_ref[...]
    for hd in range(MLA_HEADS):
        sl = slice(hd * HEAD_PAD, (hd + 1) * HEAD_PAD)
        qh = q[:, sl]
        rq = lax.rsqrt(jnp.sum(qh * qh, axis=-1, keepdims=True) * inv_d + EPS)
        q_ref[0, hd] = (rope(qh * rq * gq) + one).astype(BF16)
        kh = k_nope[:, sl]
        rk = lax.rsqrt((jnp.sum(kh * kh, axis=-1, keepdims=True) + kr_ss) * inv_d + EPS)
        k_ref[0, hd] = ((kh * gkn + kr) * rk + bias).astype(BF16)


def _mla_proj(h, nw, p, tables):
    B, Tp, _ = h.shape
    tm = _pick_tile(Tp, (384, 128))
    row = lambda b, i: (b, i, 0)
    tab = lambda b, i: (i, 0)
    qk_shape = jax.ShapeDtypeStruct((B, MLA_HEADS, Tp, HEAD_PAD), BF16)
    qk_spec = pl.BlockSpec((1, MLA_HEADS, tm, HEAD_PAD), lambda b, i: (b, 0, i, 0))
    consts = (nw, p['w_in'], p['q_norm'], p['kv_norm'], p['w_q'], p['w_k'], p['w_v'],
              p['gq'], p['gkn'], p['gkr'])
    return pl.pallas_call(
        _mla_proj_kernel,
        grid=(B, Tp // tm),
        in_specs=[pl.BlockSpec((1, tm, D_MODEL), row)]
        + [_const_spec(a.shape) for a in consts]
        + [pl.BlockSpec((tm, HEAD_PAD), tab)] * 3
        + [_const_spec(tables['one'].shape)],
        out_specs=[qk_spec, qk_spec, pl.BlockSpec((1, MLA_HEADS * V_HEAD_DIM, tm), lambda b, i: (b, 0, i))],
        out_shape=[qk_shape, qk_shape, jax.ShapeDtypeStruct((B, MLA_HEADS * V_HEAD_DIM, Tp), BF16)],
        compiler_params=_params(("parallel", "parallel")),
        name="mla_proj",
    )(h, *consts, tables['cos'], tables['sin'], _key_bias(tables, p['bound']), tables['one'])


def _flash_kernel(bound_ref, q_ref, k_ref, v_ref, o_ref, *, key_chunk):
    Tp = k_ref.shape[2]
    tq = q_ref.shape[2]
    kq = lambda k, q: lax.dot_general(k, q, (((1,), (1,)), ((), ())), preferred_element_type=F32)

    def attend(bounded):
        for j in range(HEAD_GROUP):
            q = q_ref[0, j]
            rows = slice(j * V_HEAD_DIM, (j + 1) * V_HEAD_DIM)
            if bounded:
                l = jnp.zeros((1, tq), F32)
                o = jnp.zeros((V_HEAD_DIM, tq), F32)
                for c0 in range(0, Tp, key_chunk):
                    p = jnp.exp2(kq(k_ref[0, j, c0:c0 + key_chunk, :], q))
                    l = l + jnp.sum(p, axis=0, keepdims=True)
                    o = o + jnp.dot(v_ref[0, rows, c0:c0 + key_chunk], p.astype(BF16),
                                    preferred_element_type=F32)
            else:
                s = kq(k_ref[0, j], q)
                p = jnp.exp2(s - jnp.max(s, axis=0, keepdims=True))
                l = jnp.sum(p, axis=0, keepdims=True)
                o = jnp.dot(v_ref[0, rows, :], p.astype(BF16), preferred_element_type=F32)
            o_ref[0, rows, :] = (o * (1.0 / l)).astype(BF16)

    bounded = bound_ref[0] <= SOFTMAX_BOUND_LIMIT

    @pl.when(bounded)
    def _():
        attend(True)

    @pl.when(jnp.logical_not(bounded))
    def _():
        attend(False)


def _flash(bound, q, k, v):
    B, _, Tp, _ = q.shape
    tq = _pick_tile(Tp, (384, 128))
    gw = HEAD_GROUP * V_HEAD_DIM
    return pl.pallas_call(
        functools.partial(_flash_kernel, key_chunk=_pick_tile(Tp, (1408, 384, 128))),
        grid=(B, MLA_HEADS // HEAD_GROUP, Tp // tq),
        in_specs=[pl.BlockSpec(memory_space=pltpu.SMEM),
                  pl.BlockSpec((1, HEAD_GROUP, tq, HEAD_PAD), lambda b, g, i: (b, g, i, 0)),
                  pl.BlockSpec((1, HEAD_GROUP, Tp, HEAD_PAD), lambda b, g, i: (b, g, 0, 0)),
                  pl.BlockSpec((1, gw, Tp), lambda b, g, i: (b, g, 0))],
        out_specs=pl.BlockSpec((1, gw, tq), lambda b, g, i: (b, g, i)),
        out_shape=jax.ShapeDtypeStruct((B, MLA_HEADS * V_HEAD_DIM, Tp), BF16),
        compiler_params=_params(("parallel", "parallel", "parallel")),
        name="flash",
    )(bound, q, k, v)


def _attn_out_kernel(ot_ref, w_ref, h_ref, o_ref):
    proj = lax.dot_general(ot_ref[0], w_ref[...], (((0,), (0,)), ((), ())), preferred_element_type=F32)
    o_ref[0] = h_ref[0] + proj


def _attn_out(o_t, w, h):
    B, Tp, _ = h.shape
    tm = _pick_tile(Tp, (384, 128))
    row = pl.BlockSpec((1, tm, D_MODEL), lambda b, i: (b, i, 0))
    return pl.pallas_call(
        _attn_out_kernel,
        grid=(B, Tp // tm),
        in_specs=[pl.BlockSpec((1, o_t.shape[1], tm), lambda b, i: (b, 0, i)), _const_spec(w.shape), row],
        out_specs=row,
        out_shape=jax.ShapeDtypeStruct((B, Tp, D_MODEL), F32),
        compiler_params=_params(("parallel", "parallel")),
        name="attn_out",
    )(o_t, w, h)


def _hidden_chunks():
    bounds, start = [], 0
    while start < FFN_HIDDEN:
        stop = min(start + 1536, FFN_HIDDEN)
        bounds.append((start, stop))
        start = stop
    return bounds


def _ffn_kernel(h_ref, nw_ref, wg_ref, wu_ref, wd_ref, fw_ref, o_ref, *, final):
    h = h_ref[...]
    u = _rms(h, nw_ref[...]).astype(BF16)
    acc = h
    for lo, hi in _hidden_chunks():
        g = jnp.dot(u, wg_ref[:, lo:hi], preferred_element_type=F32)
        up = jnp.dot(u, wu_ref[:, lo:hi], preferred_element_type=F32)
        a = (_silu(g) * up).astype(BF16)
        acc = acc + jnp.dot(a, wd_ref[lo:hi, :], preferred_element_type=F32)
    if final:
        acc = _rms(acc, fw_ref[...])
    o_ref[...] = acc


def _ffn(h, nw, wg, wu, wd, fw, final):
    R = h.shape[0]
    tm = _pick_tile(R, (512, 384, 256, 128))
    row = pl.BlockSpec((tm, D_MODEL), lambda i: (i, 0))
    return pl.pallas_call(
        functools.partial(_ffn_kernel, final=final),
        grid=(R // tm,),
        in_specs=[row] + [_const_spec(a.shape) for a in (nw, wg, wu, wd, fw)],
        out_specs=row,
        out_shape=jax.ShapeDtypeStruct((R, D_MODEL), F32),
        compiler_params=_params(("parallel",)),
        name="ffn",
    )(h, nw, wg, wu, wd, fw)


def _ssd_inproj_kernel(h_ref, nw_ref, w_ref, dtb_ref, z_ref, xbc_ref, dt_ref):
    tm = h_ref.shape[1]
    u = _rms(h_ref[0], nw_ref[...]).astype(BF16)
    y = jnp.dot(u, w_ref[...], preferred_element_type=F32)
    row = pl.program_id(1) * tm + lax.broadcasted_iota(jnp.int32, (tm, 1), 0)
    valid = row >= FRONT_PAD
    z_ref[0] = y[:, :D_INNER].astype(BF16)
    xbc_ref[0] = jnp.where(valid, y[:, D_INNER:D_INNER + CONV_DIM], 0.0).astype(BF16)
    dt_raw = y[:, D_INNER + CONV_DIM:] + dtb_ref[...]
    dt = jnp.maximum(dt_raw, 0.0) + jnp.log(1.0 + jnp.exp(-jnp.abs(dt_raw)))
    dt_ref[0] = jnp.where(valid, dt, 0.0)


def _ssd_inproj(h, nw, w, dtb):
    B, Tp, _ = h.shape
    tm = _pick_tile(Tp, (384, 128))
    row = lambda b, i: (b, i, 0)
    return pl.pallas_call(
        _ssd_inproj_kernel,
        grid=(B, Tp // tm),
        in_specs=[pl.BlockSpec((1, tm, D_MODEL), row)] + [_const_spec(a.shape) for a in (nw, w, dtb)],
        out_specs=[pl.BlockSpec((1, tm, D_INNER), row), pl.BlockSpec((1, tm, CONV_DIM), row),
                   pl.BlockSpec((1, tm, LANES), row)],
        out_shape=[jax.ShapeDtypeStruct((B, Tp, D_INNER), BF16),
                   jax.ShapeDtypeStruct((B, Tp, CONV_DIM), BF16),
                   jax.ShapeDtypeStruct((B, Tp, LANES), F32)],
        compiler_params=_params(("parallel", "parallel")),
        name="ssd_inproj",
    )(h, nw, w, dtb)


CONV_MARGIN = 16
CONV_ROWS = 128
CONV_SHIFTS = tuple(t - D_CONV // 2 for t in range(D_CONV) if t != D_CONV // 2)


def _conv_kernel(x_ref, w_ref, b_ref, o_ref, xs_ref):
    Tp, cw = x_ref.shape[1], x_ref.shape[2]
    win = CONV_ROWS + 2 * CONV_MARGIN
    zeros = jnp.zeros((CONV_MARGIN, cw), BF16)
    xs_ref[0:CONV_MARGIN, :] = zeros
    xs_ref[CONV_MARGIN + Tp:, :] = zeros
    xs_ref[CONV_MARGIN:CONV_MARGIN + Tp, :] = x_ref[0]
    r = lax.broadcasted_iota(jnp.int32, (len(CONV_SHIFTS) * CONV_ROWS, win), 0)
    j = lax.broadcasted_iota(jnp.int32, (len(CONV_SHIFTS) * CONV_ROWS, win), 1)
    select = jnp.zeros(r.shape, F32)
    for k, shift in enumerate(CONV_SHIFTS):
        hit = (r >= k * CONV_ROWS) & (r < (k + 1) * CONV_ROWS) & (j == r - k * CONV_ROWS + CONV_MARGIN + shift)
        select = jnp.where(hit, 1.0, select)
    select = select.astype(BF16)
    w = [jnp.broadcast_to(w_ref[t:t + 1, :], (CONV_ROWS, cw)) for t in range(D_CONV)]
    b = jnp.broadcast_to(b_ref[...], (CONV_ROWS, cw))
    centre = D_CONV // 2

    def body(i, carry):
        r0 = pl.multiple_of(i * CONV_ROWS, CONV_ROWS)
        shifted = jnp.dot(select, xs_ref[pl.ds(r0, win), :], preferred_element_type=F32)
        acc = b + w[centre] * xs_ref[pl.ds(r0 + CONV_MARGIN, CONV_ROWS), :].astype(F32)
        for k, shift in enumerate(CONV_SHIFTS):
            acc = acc + w[centre + shift] * shifted[k * CONV_ROWS:(k + 1) * CONV_ROWS, :]
        o_ref[0, pl.ds(r0, CONV_ROWS), :] = _silu(acc).astype(BF16)
        return carry

    n_blocks = Tp // CONV_ROWS
    lax.fori_loop(0, n_blocks, body, 0, unroll=3 if n_blocks % 3 == 0 else 1)


def _ssd_conv(xbc, w, b):
    B, Tp, C = xbc.shape
    cw = 512
    blk = pl.BlockSpec((1, Tp, cw), lambda bi, ci: (bi, 0, ci))
    return pl.pallas_call(
        _conv_kernel,
        grid=(B, C // cw),
        in_specs=[blk, pl.BlockSpec((D_CONV, cw), lambda bi, ci: (0, ci)),
                  pl.BlockSpec((1, cw), lambda bi, ci: (0, ci))],
        out_specs=blk,
        out_shape=jax.ShapeDtypeStruct((B, Tp, C), BF16),
        scratch_shapes=[pltpu.VMEM((Tp + 2 * CONV_MARGIN, cw), BF16)],
        compiler_params=_params(("parallel", "parallel")),
        name="ssd_conv",
    )(xbc, w, b)


def _ssd_decays(dt, neg_a, reverse):
    row = lax.broadcasted_iota(jnp.int32, (CHUNK, CHUNK), 0)
    col = lax.broadcasted_iota(jnp.int32, (CHUNK, CHUNK), 1)
    causal = (col >= row) if reverse else (col <= row)
    tri = causal.astype(BF16)
    rest = dt * neg_a
    cs = jnp.zeros((CHUNK, LANES), F32)
    for _ in range(3):
        part = rest.astype(BF16)
        cs = cs + jnp.dot(tri, part, preferred_element_type=F32)
        rest = rest - part.astype(F32)
    total = cs[0:1, :] if reverse else cs[CHUNK - 1:CHUNK, :]
    return dict(
        causal=causal, cs=cs,
        src_t=(cs - jnp.log2(dt)).T,
        out_w_t=(jnp.exp2(total - cs) * dt).T,
        e_tot=jnp.exp2(total),
        head0=SSD_HEADS if reverse else 0)


def _ssd_group(b_ref, c_ref, g):
    bg = b_ref[0, :, g * D_STATE:(g + 1) * D_STATE]
    cg = c_ref[0, :, g * D_STATE:(g + 1) * D_STATE]
    gram = lax.dot_general(cg, bg, (((1,), (1,)), ((), ())), preferred_element_type=F32)
    return gram, cg.astype(F32), bg.astype(F32).T


def _ssd_pair(dec, grp, x_ref, y_ref, st_ref, pair, low):
    gram, cg_f, bg_t = grp
    lanes = slice(pair * LANES, (pair + 1) * LANES)
    xp = x_ref[0, :, lanes]
    state = st_ref[pair]
    rhs = jnp.concatenate([xp, state.astype(BF16)], axis=0)
    hd0 = dec['head0'] + 2 * pair
    ys, ups = [], []
    for hd in (hd0, hd0 + 1):
        cs_l = jnp.broadcast_to(dec['cs'][:, hd:hd + 1], (CHUNK, CHUNK))
        seg = cs_l - dec['src_t'][hd:hd + 1, :]
        within = gram * jnp.exp2(jnp.where(dec['causal'], seg, -jnp.inf))
        from_state = cg_f * jnp.exp2(cs_l)
        lhs = jnp.concatenate([within, from_state], axis=1).astype(BF16)
        ys.append(jnp.dot(lhs, rhs, preferred_element_type=F32))
        ups.append(jnp.dot((bg_t * dec['out_w_t'][hd:hd + 1, :]).astype(BF16), xp, preferred_element_type=F32))
    y_ref[0, :, lanes] = jnp.where(low, ys[0], ys[1]).astype(y_ref.dtype)
    e_tot = dec['e_tot']
    keep = jnp.where(low[0:1, :], jnp.broadcast_to(e_tot[:, hd0:hd0 + 1], (1, LANES)),
                     jnp.broadcast_to(e_tot[:, hd0 + 1:hd0 + 2], (1, LANES)))
    st_ref[pair] = state * keep + jnp.where(low, ups[0], ups[1])


def _ssd_scan_kernel(xf_ref, bf_ref, cf_ref, dtf_ref, xb_ref, bb_ref, cb_ref, dtb_ref, alog_ref,
                     yf_ref, yb_ref, stf_ref, stb_ref):
    @pl.when(pl.program_id(1) == 0)
    def _():
        stf_ref[...] = jnp.zeros(stf_ref.shape, F32)
        stb_ref[...] = jnp.zeros(stb_ref.shape, F32)

    neg_a = -jnp.exp(alog_ref[...]) * LOG2E
    low = lax.broadcasted_iota(jnp.int32, (CHUNK, LANES), 1) < SSD_HEAD_DIM
    dec_f = _ssd_decays(dtf_ref[0], neg_a, reverse=False)
    dec_b = _ssd_decays(dtb_ref[0], neg_a, reverse=True)
    pairs_per_group = HEADS_PER_GROUP // 2
    for g in range(SSD_GROUPS):
        grp_f = _ssd_group(bf_ref, cf_ref, g)
        grp_b = _ssd_group(bb_ref, cb_ref, g)
        for pr in range(pairs_per_group):
            pair = g * pairs_per_group + pr
            _ssd_pair(dec_f, grp_f, xf_ref, yf_ref, stf_ref, pair, low)
            _ssd_pair(dec_b, grp_b, xb_ref, yb_ref, stb_ref, pair, low)


def _ssd_scan(xbc, dt, a_log):
    B, Tp, _ = xbc.shape
    nc = Tp // CHUNK
    bc_w = SSD_GROUPS * D_STATE
    b_blk = D_INNER // bc_w

    def specs(pos):
        return [pl.BlockSpec((1, CHUNK, D_INNER), lambda b, c: (b, pos(c), 0)),
                pl.BlockSpec((1, CHUNK, bc_w), lambda b, c: (b, pos(c), b_blk)),
                pl.BlockSpec((1, CHUNK, bc_w), lambda b, c: (b, pos(c), b_blk + 1)),
                pl.BlockSpec((1, CHUNK, LANES), lambda b, c: (b, pos(c), 0))]

    fwd = lambda c: c
    bwd = lambda c: nc - 1 - c
    y_shape = jax.ShapeDtypeStruct((B, Tp, D_INNER), BF16)
    state = pltpu.VMEM((SSD_HEADS // 2, D_STATE, LANES), F32)
    return pl.pallas_call(
        _ssd_scan_kernel,
        grid=(B, nc),
        in_specs=specs(fwd) + specs(bwd) + [pl.BlockSpec((1, LANES), lambda b, c: (0, 0))],
        out_specs=[pl.BlockSpec((1, CHUNK, D_INNER), lambda b, c: (b, fwd(c), 0)),
                   pl.BlockSpec((1, CHUNK, D_INNER), lambda b, c: (b, bwd(c), 0))],
        out_shape=[y_shape, y_shape],
        scratch_shapes=[state, state],
        compiler_params=_params(("parallel", "arbitrary")),
        name="ssd_scan",
    )(xbc, xbc, xbc, dt, xbc, xbc, xbc, dt, a_log)


def _ssd_out_kernel(yf_ref, yb_ref, x_ref, z_ref, d_ref, nw_ref, w_ref, h_ref, o_ref):
    y = yf_ref[...].astype(F32) + yb_ref[...].astype(F32) + d_ref[...] * x_ref[...].astype(F32)
    y = y * _silu(z_ref[...].astype(F32))
    yn = _rms(y, nw_ref[...]).astype(BF16)
    o_ref[...] = h_ref[...] + jnp.dot(yn, w_ref[...], preferred_element_type=F32)


def _ssd_out(yf, yb, xbc, z, d_e, nw, w, h):
    R = h.shape[0]
    tm = _pick_tile(R, (512, 384, 256, 128))
    wide = pl.BlockSpec((tm, D_INNER), lambda i: (i, 0))
    row = pl.BlockSpec((tm, D_MODEL), lambda i: (i, 0))
    return pl.pallas_call(
        _ssd_out_kernel,
        grid=(R // tm,),
        in_specs=[wide, wide, wide, wide] + [_const_spec(a.shape) for a in (d_e, nw, w)] + [row],
        out_specs=row,
        out_shape=jax.ShapeDtypeStruct((R, D_MODEL), F32),
        compiler_params=_params(("parallel",)),
        name="ssd_out",
    )(yf, yb, xbc, z, d_e, nw, w, h)


def _row(v, width=None):
    v = v.astype(F32).reshape(1, -1)
    if width is not None and v.shape[1] < width:
        v = jnp.pad(v, ((0, 0), (0, width - v.shape[1])))
    return v


def _prep_mla(w_in, q_norm, kv_norm, w_q_up, w_kv_up, q_head_norm, k_head_norm, w_out):
    lat = Q_LORA_RANK + KV_LORA_RANK
    rope_cols = jnp.pad(w_in[:, lat:], ((0, 0), (ROPE_LANE0, HEAD_PAD - QK_HEAD_DIM)))
    w_in_p = jnp.concatenate([w_in[:, :lat], rope_cols], axis=1).astype(BF16)
    w_q = w_q_up.reshape(Q_LORA_RANK, MLA_HEADS, QK_HEAD_DIM)
    w_q = jnp.pad(w_q, ((0, 0), (0, 0), (0, HEAD_PAD - QK_HEAD_DIM))).reshape(Q_LORA_RANK, -1).astype(BF16)
    w_kv = w_kv_up.reshape(KV_LORA_RANK, MLA_HEADS, QK_NOPE_DIM + V_HEAD_DIM)
    w_k = jnp.pad(w_kv[:, :, :QK_NOPE_DIM], ((0, 0), (0, 0), (0, HEAD_PAD - QK_NOPE_DIM)))
    w_k = w_k.reshape(KV_LORA_RANK, -1).astype(BF16)
    w_v = w_kv[:, :, QK_NOPE_DIM:].reshape(KV_LORA_RANK, -1).T.astype(BF16)
    scale = LOG2E / math.sqrt(QK_HEAD_DIM)
    gq = q_head_norm.astype(F32)
    gk = k_head_norm.astype(F32)
    bound = 1.02 * scale * QK_HEAD_DIM * jnp.max(jnp.abs(gq)) * jnp.max(jnp.abs(gk))
    return dict(
        w_in=w_in_p, q_norm=_row(q_norm), kv_norm=_row(kv_norm), w_q=w_q, w_k=w_k, w_v=w_v,
        bound=bound.reshape(1),
        gq=_row(gq * scale, HEAD_PAD),
        gkn=_row(gk[:QK_NOPE_DIM], HEAD_PAD),
        gkr=_row(jnp.pad(gk[QK_NOPE_DIM:], (ROPE_LANE0, 0)), HEAD_PAD),
        w_out=w_out.astype(BF16))


def _attention_tables(Tp):
    half = QK_ROPE_DIM // 2
    pos = jnp.arange(Tp, dtype=F32) - FRONT_PAD
    inv_freq = ROPE_BASE ** (-jnp.arange(half, dtype=F32) / half)
    ang = pos[:, None] * inv_freq[None, :]
    cos, sin = jnp.cos(ang), jnp.sin(ang)
    tail = HEAD_PAD - QK_HEAD_DIM
    cos_t = jnp.concatenate([jnp.ones((Tp, ROPE_LANE0), F32), cos, cos, jnp.ones((Tp, tail), F32)], axis=1)
    sin_t = jnp.concatenate([jnp.zeros((Tp, ROPE_LANE0), F32), -sin, sin, jnp.zeros((Tp, tail), F32)], axis=1)
    is_pad = (jnp.arange(Tp) < FRONT_PAD)[:, None]
    lane = jnp.arange(HEAD_PAD)[None, :]
    one = (lane == MASK_LANE).astype(F32)
    return dict(cos=cos_t, sin=sin_t, pad_mask=jnp.where(is_pad, MASK_VALUE, 0.0).astype(F32) * one, one=one)


def _key_bias(tables, bound):
    return jnp.where(tables['pad_mask'] < 0, tables['pad_mask'], -bound * tables['one'])


def _prep_ssd(w_in, conv_w, conv_b, dt_bias, a_log, d_skip, norm_w, w_out):
    w_in_p = jnp.pad(w_in, ((0, 0), (0, LANES - 2 * SSD_HEADS))).astype(BF16)
    return dict(
        w_in=w_in_p, conv_w=conv_w.astype(F32), conv_b=_row(conv_b),
        dt_bias=_row(dt_bias, LANES), a_log=_row(a_log, LANES),
        d_e=_row(jnp.repeat(d_skip.astype(F32), SSD_HEAD_DIM)),
        norm=_row(norm_w), w_out=w_out.astype(BF16))


def kernel(x_prompt, x_sample, meta_tokens, mix_norm, ffn_norm, mla_w_in, mla_q_norm, mla_kv_norm, mla_w_q_up, mla_w_kv_up, mla_q_head_norm, mla_k_head_norm, mla_w_out, ssd_w_in, ssd_conv_w, ssd_conv_b, ssd_dt_bias, ssd_a_log, ssd_d, ssd_norm, ssd_w_out, ffn_w_gate, ffn_w_up, ffn_w_down, final_norm):
    assert x_prompt.shape[1:] == x_sample.shape[1:] and x_prompt.shape[2] == D_MODEL
    assert x_prompt.shape[1] % CHUNK == 0
    depth = mix_norm.shape[0]
    n_prompt = x_prompt.shape[0]
    B = n_prompt + x_sample.shape[0]
    S = x_prompt.shape[1]
    Tp = FRONT_PAD + N_META + S
    head = jnp.concatenate([jnp.zeros((FRONT_PAD, D_MODEL), F32), meta_tokens.astype(F32)], axis=0)
    h = jnp.concatenate([jnp.broadcast_to(head[None], (B, CHUNK, D_MODEL)),
                         jnp.concatenate([x_prompt, x_sample], axis=0)], axis=1)
    tables = _attention_tables(Tp)
    attn_p = (mla_w_in, mla_q_norm, mla_kv_norm, mla_w_q_up, mla_w_kv_up,
              mla_q_head_norm, mla_k_head_norm, mla_w_out)
    ssd_p = (ssd_w_in, ssd_conv_w, ssd_conv_b, ssd_dt_bias, ssd_a_log, ssd_d, ssd_norm, ssd_w_out)
    flat = lambda t: t.reshape(B * Tp, t.shape[-1])
    ia = ib = 0
    for layer in range(depth):
        nw = _row(mix_norm[layer])
        if layer % 2 == 0:
            p = _prep_mla(*[t[ia] for t in attn_p])
            ia += 1
            q, k, v = _mla_proj(h, nw, p, tables)
            o = _flash(p['bound'], q, k, v)
            h2 = flat(_attn_out(o, p['w_out'], h))
        else:
            p = _prep_ssd(*[t[ib] for t in ssd_p])
            ib += 1
            z, xbc, dt = _ssd_inproj(h, nw, p['w_in'], p['dt_bias'])
            xbc = _ssd_conv(xbc, p['conv_w'], p['conv_b'])
            yf, yb = _ssd_scan(xbc, dt, p['a_log'])
            h2 = _ssd_out(flat(yf), flat(yb), flat(xbc), flat(z), p['d_e'], p['norm'], p['w_out'], flat(h))
        h2 = _ffn(h2, _row(ffn_norm[layer]), ffn_w_gate[layer].astype(BF16), ffn_w_up[layer].astype(BF16),
                  ffn_w_down[layer].astype(BF16), _row(final_norm), final=(layer == depth - 1))
        h = h2.reshape(B, Tp, D_MODEL)
    y = h[:, CHUNK:]
    return (y[:n_prompt], y[n_prompt:])
```

```python
import functools
import math

import jax
import jax.numpy as jnp
from jax import lax
from jax.experimental import pallas as pl
from jax.experimental.pallas import tpu as pltpu

F32 = jnp.float32
BF16 = jnp.bfloat16

D_MODEL = 1024
N_META = 16
EPS = 1e-6
MLA_HEADS = 16
QK_NOPE_DIM = 64
QK_ROPE_DIM = 32
QK_HEAD_DIM = QK_NOPE_DIM + QK_ROPE_DIM
V_HEAD_DIM = 64
Q_LORA_RANK = 384
KV_LORA_RANK = 256
ROPE_BASE = 10000.0
D_INNER = 2 * D_MODEL
SSD_HEAD_DIM = 64
SSD_HEADS = D_INNER // SSD_HEAD_DIM
SSD_GROUPS = 4
HEADS_PER_GROUP = SSD_HEADS // SSD_GROUPS
D_STATE = 128
D_CONV = 5
CONV_DIM = D_INNER + 2 * SSD_GROUPS * D_STATE
CHUNK = 128
FFN_HIDDEN = -(-8 * D_MODEL // (3 * 256)) * 256

LANES = 128
FRONT_PAD = CHUNK - N_META
HEAD_PAD = LANES
ROPE_LANE0 = QK_NOPE_DIM
MASK_LANE = QK_HEAD_DIM
MASK_VALUE = -1e30
LOG2E = 1.0 / math.log(2.0)
SOFTMAX_BOUND_LIMIT = 40.0
HEAD_GROUP = 4
GROUP_W = HEADS_PER_GROUP * SSD_HEAD_DIM
VMEM_LIMIT = 56 * 1024 * 1024


def _pick_tile(n, candidates):
    for c in candidates:
        if n % c == 0:
            return c
    return n


def _params(sem, vmem=VMEM_LIMIT):
    return pltpu.CompilerParams(dimension_semantics=sem, vmem_limit_bytes=vmem)


def _const_spec(shape):
    nd = len(shape)
    return pl.BlockSpec(shape, lambda *_: (0,) * nd, pipeline_mode=pl.Buffered(1))


def _rms(x, w):
    return x * lax.rsqrt(jnp.mean(x * x, axis=-1, keepdims=True) + EPS) * w


def _silu(x):
    return x * (1.0 / (1.0 + jnp.exp(-x)))


def _mla_proj_kernel(h_ref, nw_ref, win_ref, qn_ref, kvn_ref, wq_ref, wk_ref, wv_ref,
                     gq_ref, gkn_ref, gkr_ref, cos_ref, sin_ref, bias_ref, one_ref,
                     q_ref, k_ref, v_ref):
    u = _rms(h_ref[0], nw_ref[...]).astype(BF16)
    c = jnp.dot(u, win_ref[...], preferred_element_type=F32)
    cq = _rms(c[:, :Q_LORA_RANK], qn_ref[...]).astype(BF16)
    ckv = _rms(c[:, Q_LORA_RANK:Q_LORA_RANK + KV_LORA_RANK], kvn_ref[...]).astype(BF16)
    k_rope = c[:, Q_LORA_RANK + KV_LORA_RANK:]
    q = jnp.dot(cq, wq_ref[...], preferred_element_type=F32)
    k_nope = jnp.dot(ckv, wk_ref[...], preferred_element_type=F32)
    v_ref[0] = lax.dot_general(wv_ref[...], ckv, (((1,), (1,)), ((), ())),
                               preferred_element_type=F32).astype(BF16)

    cos = cos_ref[...]
    sin = sin_ref[...]
    lane = lax.broadcasted_iota(jnp.int32, cos.shape, 1)
    half = QK_ROPE_DIM // 2
    first_half = (lane >= ROPE_LANE0) & (lane < ROPE_LANE0 + half)

    def rope(x):
        swapped = jnp.where(first_half, pltpu.roll(x, HEAD_PAD - half, 1), pltpu.roll(x, half, 1))
        return x * cos + swapped * sin

    inv_d = 1.0 / QK_HEAD_DIM
    kr = rope(k_rope * gkr_ref[...])
    kr_ss = jnp.sum(k_rope * k_rope, axis=-1, keepdims=True)
    gq = gq_ref[...]
    gkn = gkn_ref[...]
    bias = bias_ref[...]
    one = one_ref[...]
    for hd in range(MLA_HEADS):
        sl = slice(hd * HEAD_PAD, (hd + 1) * HEAD_PAD)
        qh = q[:, sl]
        rq = lax.rsqrt(jnp.sum(qh * qh, axis=-1, keepdims=True) * inv_d + EPS)
        q_ref[0, hd] = (rope(qh * rq * gq) + one).astype(BF16)
        kh = k_nope[:, sl]
        rk = lax.rsqrt((jnp.sum(kh * kh, axis=-1, keepdims=True) + kr_ss) * inv_d + EPS)
        k_ref[0, hd] = ((kh * gkn + kr) * rk + bias).astype(BF16)


def _mla_proj(h, nw, p, tables):
    B, Tp, _ = h.shape
    tm = _pick_tile(Tp, (384, 128))
    row = lambda b, i: (b, i, 0)
    tab = lambda b, i: (i, 0)
    qk_shape = jax.ShapeDtypeStruct((B, MLA_HEADS, Tp, HEAD_PAD), BF16)
    qk_spec = pl.BlockSpec((1, MLA_HEADS, tm, HEAD_PAD), lambda b, i: (b, 0, i, 0))
    consts = (nw, p['w_in'], p['q_norm'], p['kv_norm'], p['w_q'], p['w_k'], p['w_v'],
              p['gq'], p['gkn'], p['gkr'])
    return pl.pallas_call(
        _mla_proj_kernel,
        grid=(B, Tp // tm),
        in_specs=[pl.BlockSpec((1, tm, D_MODEL), row)]
        + [_const_spec(a.shape) for a in consts]
        + [pl.BlockSpec((tm, HEAD_PAD), tab)] * 3
        + [_const_spec(tables['one'].shape)],
        out_specs=[qk_spec, qk_spec, pl.BlockSpec((1, MLA_HEADS * V_HEAD_DIM, tm), lambda b, i: (b, 0, i))],
        out_shape=[qk_shape, qk_shape, jax.ShapeDtypeStruct((B, MLA_HEADS * V_HEAD_DIM, Tp), BF16)],
        compiler_params=_params(("parallel", "parallel")),
        name="mla_proj",
    )(h, *consts, tables['cos'], tables['sin'], _key_bias(tables, p['bound']), tables['one'])


def _flash_kernel(bound_ref, q_ref, k_ref, v_ref, o_ref, *, key_chunk):
    Tp = k_ref.shape[2]
    tq = q_ref.shape[2]
    kq = lambda k, q: lax.dot_general(k, q, (((1,), (1,)), ((), ())), preferred_element_type=F32)

    def attend(bounded):
        for j in range(HEAD_GROUP):
            q = q_ref[0, j]
            rows = slice(j * V_HEAD_DIM, (j + 1) * V_HEAD_DIM)
            if bounded:
                l = jnp.zeros((1, tq), F32)
                o = jnp.zeros((V_HEAD_DIM, tq), F32)
                for c0 in range(0, Tp, key_chunk):
                    p = jnp.exp2(kq(k_ref[0, j, c0:c0 + key_chunk, :], q))
                    l = l + jnp.sum(p, axis=0, keepdims=True)
                    o = o + jnp.dot(v_ref[0, rows, c0:c0 + key_chunk], p.astype(BF16),
                                    preferred_element_type=F32)
            else:
                s = kq(k_ref[0, j], q)
                p = jnp.exp2(s - jnp.max(s, axis=0, keepdims=True))
                l = jnp.sum(p, axis=0, keepdims=True)
                o = jnp.dot(v_ref[0, rows, :], p.astype(BF16), preferred_element_type=F32)
            o_ref[0, rows, :] = (o * (1.0 / l)).astype(BF16)

    bounded = bound_ref[0] <= SOFTMAX_BOUND_LIMIT

    @pl.when(bounded)
    def _():
        attend(True)

    @pl.when(jnp.logical_not(bounded))
    def _():
        attend(False)


def _flash(bound, q, k, v):
    B, _, Tp, _ = q.shape
    tq = _pick_tile(Tp, (384, 128))
    gw = HEAD_GROUP * V_HEAD_DIM
    return pl.pallas_call(
        functools.partial(_flash_kernel, key_chunk=Tp),
        grid=(B, MLA_HEADS // HEAD_GROUP, Tp // tq),
        in_specs=[pl.BlockSpec(memory_space=pltpu.SMEM),
                  pl.BlockSpec((1, HEAD_GROUP, tq, HEAD_PAD), lambda b, g, i: (b, g, i, 0)),
                  pl.BlockSpec((1, HEAD_GROUP, Tp, HEAD_PAD), lambda b, g, i: (b, g, 0, 0)),
                  pl.BlockSpec((1, gw, Tp), lambda b, g, i: (b, g, 0))],
        out_specs=pl.BlockSpec((1, gw, tq), lambda b, g, i: (b, g, i)),
        out_shape=jax.ShapeDtypeStruct((B, MLA_HEADS * V_HEAD_DIM, Tp), BF16),
        compiler_params=_params(("parallel", "parallel", "parallel")),
        name="flash",
    )(bound, q, k, v)


def _attn_out_kernel(ot_ref, w_ref, h_ref, o_ref):
    proj = lax.dot_general(ot_ref[0], w_ref[...], (((0,), (0,)), ((), ())), preferred_element_type=F32)
    o_ref[0] = h_ref[0] + proj


def _attn_out(o_t, w, h):
    B, Tp, _ = h.shape
    tm = _pick_tile(Tp, (384, 128))
    row = pl.BlockSpec((1, tm, D_MODEL), lambda b, i: (b, i, 0))
    return pl.pallas_call(
        _attn_out_kernel,
        grid=(B, Tp // tm),
        in_specs=[pl.BlockSpec((1, o_t.shape[1], tm), lambda b, i: (b, 0, i)), _const_spec(w.shape), row],
        out_specs=row,
        out_shape=jax.ShapeDtypeStruct((B, Tp, D_MODEL), F32),
        compiler_params=_params(("parallel", "parallel")),
        name="attn_out",
    )(o_t, w, h)


def _hidden_chunks():
    bounds, start = [], 0
    while start < FFN_HIDDEN:
        stop = min(start + 1536, FFN_HIDDEN)
        bounds.append((start, stop))
        start = stop
    return bounds


def _ffn_kernel(h_ref, nw_ref, wg_ref, wu_ref, wd_ref, fw_ref, o_ref, *, final):
    h = h_ref[...]
    u = _rms(h, nw_ref[...]).astype(BF16)
    acc = h
    for lo, hi in _hidden_chunks():
        g = jnp.dot(u, wg_ref[:, lo:hi], preferred_element_type=F32)
        up = jnp.dot(u, wu_ref[:, lo:hi], preferred_element_type=F32)
        a = (_silu(g) * up).astype(BF16)
        acc = acc + jnp.dot(a, wd_ref[lo:hi, :], preferred_element_type=F32)
    if final:
        acc = _rms(acc, fw_ref[...])
    o_ref[...] = acc


def _ffn(h, nw, wg, wu, wd, fw, final):
    R = h.shape[0]
    tm = _pick_tile(R, (512, 384, 256, 128))
    row = pl.BlockSpec((tm, D_MODEL), lambda i: (i, 0))
    return pl.pallas_call(
        functools.partial(_ffn_kernel, final=final),
        grid=(R // tm,),
        in_specs=[row] + [_const_spec(a.shape) for a in (nw, wg, wu, wd, fw)],
        out_specs=row,
        out_shape=jax.ShapeDtypeStruct((R, D_MODEL), F32),
        compiler_params=_params(("parallel",)),
        name="ffn",
    )(h, nw, wg, wu, wd, fw)


SUBLANES = 8


def _ssd_inproj_kernel(h_ref, nw_ref, w_ref, dtb_ref, alog_ref, z_ref, xbc_ref, cs_ref, srct_ref, outwt_ref,
                       etot_ref):
    tm = h_ref.shape[1]
    u = _rms(h_ref[0], nw_ref[...]).astype(BF16)
    y = jnp.dot(u, w_ref[...], preferred_element_type=F32)
    row = pl.program_id(1) * tm + lax.broadcasted_iota(jnp.int32, (tm, 1), 0)
    valid = row >= FRONT_PAD
    z_ref[0] = y[:, :D_INNER].astype(BF16)
    xbc_ref[0] = jnp.where(valid, y[:, D_INNER:D_INNER + CONV_DIM], 0.0).astype(BF16)
    dt_raw = y[:, D_INNER + CONV_DIM:] + dtb_ref[...]
    dt_all = jnp.where(valid, jnp.maximum(dt_raw, 0.0) + jnp.log(1.0 + jnp.exp(-jnp.abs(dt_raw))), 0.0)

    neg_a = -jnp.exp(alog_ref[...]) * LOG2E
    forward = lax.broadcasted_iota(jnp.int32, (1, LANES), 1) < SSD_HEADS
    r = lax.broadcasted_iota(jnp.int32, (CHUNK, CHUNK), 0)
    c = lax.broadcasted_iota(jnp.int32, (CHUNK, CHUNK), 1)
    tri_f = (c <= r).astype(BF16)
    tri_b = (c >= r).astype(BF16)
    for ch in range(tm // CHUNK):
        rows = slice(ch * CHUNK, (ch + 1) * CHUNK)
        dt = dt_all[rows]
        rest = dt * neg_a
        cs_f = jnp.zeros((CHUNK, LANES), F32)
        cs_b = jnp.zeros((CHUNK, LANES), F32)
        for _ in range(3):
            part = rest.astype(BF16)
            cs_f = cs_f + jnp.dot(tri_f, part, preferred_element_type=F32)
            cs_b = cs_b + jnp.dot(tri_b, part, preferred_element_type=F32)
            rest = rest - part.astype(F32)
        cs = jnp.where(forward, cs_f, cs_b)
        total = jnp.where(forward, cs_f[CHUNK - 1:CHUNK, :], cs_b[0:1, :])
        cs_ref[0, rows, :] = cs
        srct_ref[0, :, rows] = (cs - jnp.log2(dt)).T
        outwt_ref[0, :, rows] = (jnp.exp2(total - cs) * dt).T
        etot_ref[0, ch * SUBLANES:(ch + 1) * SUBLANES, :] = jnp.broadcast_to(jnp.exp2(total), (SUBLANES, LANES))


def _ssd_inproj(h, nw, w, dtb, a_log):
    B, Tp, _ = h.shape
    tm = _pick_tile(Tp, (384, 128))
    row = lambda b, i: (b, i, 0)
    col = lambda b, i: (b, 0, i)
    per_chunk = SUBLANES * tm // CHUNK
    return pl.pallas_call(
        _ssd_inproj_kernel,
        grid=(B, Tp // tm),
        in_specs=[pl.BlockSpec((1, tm, D_MODEL), row)] + [_const_spec(a.shape) for a in (nw, w, dtb, a_log)],
        out_specs=[pl.BlockSpec((1, tm, D_INNER), row), pl.BlockSpec((1, tm, CONV_DIM), row),
                   pl.BlockSpec((1, tm, LANES), row), pl.BlockSpec((1, LANES, tm), col),
                   pl.BlockSpec((1, LANES, tm), col), pl.BlockSpec((1, per_chunk, LANES), row)],
        out_shape=[jax.ShapeDtypeStruct((B, Tp, D_INNER), BF16),
                   jax.ShapeDtypeStruct((B, Tp, CONV_DIM), BF16),
                   jax.ShapeDtypeStruct((B, Tp, LANES), F32),
                   jax.ShapeDtypeStruct((B, LANES, Tp), F32),
                   jax.ShapeDtypeStruct((B, LANES, Tp), F32),
                   jax.ShapeDtypeStruct((B, SUBLANES * Tp // CHUNK, LANES), F32)],
        compiler_params=_params(("parallel", "parallel")),
        name="ssd_inproj",
    )(h, nw, w, dtb, a_log)


CONV_MARGIN = 16
CONV_ROWS = 128
CONV_SHIFTS = tuple(t - D_CONV // 2 for t in range(D_CONV) if t != D_CONV // 2)


def _conv_kernel(x_ref, w_ref, b_ref, o_ref, xs_ref):
    Tp, cw = x_ref.shape[1], x_ref.shape[2]
    win = CONV_ROWS + 2 * CONV_MARGIN
    zeros = jnp.zeros((CONV_MARGIN, cw), BF16)
    xs_ref[0:CONV_MARGIN, :] = zeros
    xs_ref[CONV_MARGIN + Tp:, :] = zeros
    xs_ref[CONV_MARGIN:CONV_MARGIN + Tp, :] = x_ref[0]
    r = lax.broadcasted_iota(jnp.int32, (len(CONV_SHIFTS) * CONV_ROWS, win), 0)
    j = lax.broadcasted_iota(jnp.int32, (len(CONV_SHIFTS) * CONV_ROWS, win), 1)
    select = jnp.zeros(r.shape, F32)
    for k, shift in enumerate(CONV_SHIFTS):
        hit = (r >= k * CONV_ROWS) & (r < (k + 1) * CONV_ROWS) & (j == r - k * CONV_ROWS + CONV_MARGIN + shift)
        select = jnp.where(hit, 1.0, select)
    select = select.astype(BF16)
    w = [jnp.broadcast_to(w_ref[t:t + 1, :], (CONV_ROWS, cw)) for t in range(D_CONV)]
    b = jnp.broadcast_to(b_ref[...], (CONV_ROWS, cw))
    centre = D_CONV // 2

    def body(i, carry):
        r0 = pl.multiple_of(i * CONV_ROWS, CONV_ROWS)
        shifted = jnp.dot(select, xs_ref[pl.ds(r0, win), :], preferred_element_type=F32)
        acc = b + w[centre] * xs_ref[pl.ds(r0 + CONV_MARGIN, CONV_ROWS), :].astype(F32)
        for k, shift in enumerate(CONV_SHIFTS):
            acc = acc + w[centre + shift] * shifted[k * CONV_ROWS:(k + 1) * CONV_ROWS, :]
        o_ref[0, pl.ds(r0, CONV_ROWS), :] = _silu(acc).astype(BF16)
        return carry

    n_blocks = Tp // CONV_ROWS
    lax.fori_loop(0, n_blocks, body, 0, unroll=3 if n_blocks % 3 == 0 else 1)


def _ssd_conv(xbc, w, b):
    B, Tp, C = xbc.shape
    cw = 512
    blk = pl.BlockSpec((1, Tp, cw), lambda bi, ci: (bi, 0, ci))
    return pl.pallas_call(
        _conv_kernel,
        grid=(B, C // cw),
        in_specs=[blk, pl.BlockSpec((D_CONV, cw), lambda bi, ci: (0, ci)),
                  pl.BlockSpec((1, cw), lambda bi, ci: (0, ci))],
        out_specs=blk,
        out_shape=jax.ShapeDtypeStruct((B, Tp, C), BF16),
        scratch_shapes=[pltpu.VMEM((Tp + 2 * CONV_MARGIN, cw), BF16)],
        compiler_params=_params(("parallel", "parallel")),
        name="ssd_conv",
    )(xbc, w, b)


def _ssd_decays(cs_ref, srct_ref, outwt_ref, etot_ref, reverse):
    row = lax.broadcasted_iota(jnp.int32, (CHUNK, CHUNK), 0)
    col = lax.broadcasted_iota(jnp.int32, (CHUNK, CHUNK), 1)
    causal = (col >= row) if reverse else (col <= row)
    return dict(
        hide=jnp.where(causal, 0.0, -jnp.inf),
        cs=cs_ref[0], src_t=srct_ref[0], out_w_t=outwt_ref[0], e_tot=etot_ref[0, 0:1, :],
        head0=SSD_HEADS if reverse else 0)


def _ssd_group(b_ref, c_ref, g):
    bg = b_ref[0, :, g * D_STATE:(g + 1) * D_STATE]
    cg = c_ref[0, :, g * D_STATE:(g + 1) * D_STATE]
    gram = lax.dot_general(cg, bg, (((1,), (1,)), ((), ())), preferred_element_type=F32)
    return gram, cg.astype(F32), bg.astype(F32).T


def _ssd_pair(dec, grp, x_ref, y_ref, st_ref, pair, low):
    gram, cg_f, bg_t = grp
    lanes = slice(pair * LANES, (pair + 1) * LANES)
    xp = x_ref[0, :, lanes]
    state = st_ref[pair]
    rhs = jnp.concatenate([xp, state.astype(BF16)], axis=0)
    hd0 = dec['head0'] + 2 * pair
    ys, ups = [], []
    for hd in (hd0, hd0 + 1):
        cs_l = jnp.broadcast_to(dec['cs'][:, hd:hd + 1], (CHUNK, CHUNK))
        seg = cs_l - dec['src_t'][hd:hd + 1, :]
        within = gram * jnp.exp2(seg + dec['hide'])
        from_state = cg_f * jnp.exp2(cs_l)
        lhs = jnp.concatenate([within, from_state], axis=1).astype(BF16)
        ys.append(jnp.dot(lhs, rhs, preferred_element_type=F32))
        ups.append(jnp.dot((bg_t * dec['out_w_t'][hd:hd + 1, :]).astype(BF16), xp, preferred_element_type=F32))
    y_ref[0, :, lanes] = jnp.where(low, ys[0], ys[1]).astype(y_ref.dtype)
    e_tot = dec['e_tot']
    keep = jnp.where(low[0:1, :], jnp.broadcast_to(e_tot[:, hd0:hd0 + 1], (1, LANES)),
                     jnp.broadcast_to(e_tot[:, hd0 + 1:hd0 + 2], (1, LANES)))
    st_ref[pair] = state * keep + jnp.where(low, ups[0], ups[1])


def _ssd_scan_kernel(xf_ref, bf_ref, cf_ref, csf_ref, srcf_ref, outf_ref, etotf_ref,
                     xb_ref, bb_ref, cb_ref, csb_ref, srcb_ref, outb_ref, etotb_ref,
                     yf_ref, yb_ref, stf_ref, stb_ref):
    @pl.when(pl.program_id(1) == 0)
    def _():
        stf_ref[...] = jnp.zeros(stf_ref.shape, F32)
        stb_ref[...] = jnp.zeros(stb_ref.shape, F32)

    low = lax.broadcasted_iota(jnp.int32, (CHUNK, LANES), 1) < SSD_HEAD_DIM
    dec_f = _ssd_decays(csf_ref, srcf_ref, outf_ref, etotf_ref, reverse=False)
    dec_b = _ssd_decays(csb_ref, srcb_ref, outb_ref, etotb_ref, reverse=True)
    pairs_per_group = HEADS_PER_GROUP // 2
    for g in range(SSD_GROUPS):
        grp_f = _ssd_group(bf_ref, cf_ref, g)
        grp_b = _ssd_group(bb_ref, cb_ref, g)
        for pr in range(pairs_per_group):
            pair = g * pairs_per_group + pr
            _ssd_pair(dec_f, grp_f, xf_ref, yf_ref, stf_ref, pair, low)
            _ssd_pair(dec_b, grp_b, xb_ref, yb_ref, stb_ref, pair, low)


def _ssd_scan(xbc, cs, src_t, out_w_t, e_tot):
    B, Tp, _ = xbc.shape
    nc = Tp // CHUNK
    bc_w = SSD_GROUPS * D_STATE
    b_blk = D_INNER // bc_w

    def specs(pos):
        return [pl.BlockSpec((1, CHUNK, D_INNER), lambda b, c: (b, pos(c), 0)),
                pl.BlockSpec((1, CHUNK, bc_w), lambda b, c: (b, pos(c), b_blk)),
                pl.BlockSpec((1, CHUNK, bc_w), lambda b, c: (b, pos(c), b_blk + 1)),
                pl.BlockSpec((1, CHUNK, LANES), lambda b, c: (b, pos(c), 0)),
                pl.BlockSpec((1, LANES, CHUNK), lambda b, c: (b, 0, pos(c))),
                pl.BlockSpec((1, LANES, CHUNK), lambda b, c: (b, 0, pos(c))),
                pl.BlockSpec((1, SUBLANES, LANES), lambda b, c: (b, pos(c), 0))]

    fwd = lambda c: c
    bwd = lambda c: nc - 1 - c
    y_shape = jax.ShapeDtypeStruct((B, Tp, D_INNER), BF16)
    state = pltpu.VMEM((SSD_HEADS // 2, D_STATE, LANES), F32)
    return pl.pallas_call(
        _ssd_scan_kernel,
        grid=(B, nc),
        in_specs=specs(fwd) + specs(bwd),
        out_specs=[pl.BlockSpec((1, CHUNK, D_INNER), lambda b, c: (b, fwd(c), 0)),
                   pl.BlockSpec((1, CHUNK, D_INNER), lambda b, c: (b, bwd(c), 0))],
        out_shape=[y_shape, y_shape],
        scratch_shapes=[state, state],
        compiler_params=_params(("parallel", "arbitrary")),
        name="ssd_scan",
    )(*(2 * (xbc, xbc, xbc, cs, src_t, out_w_t, e_tot)))


def _ssd_out_kernel(yf_ref, yb_ref, x_ref, z_ref, d_ref, nw_ref, w_ref, h_ref, o_ref):
    y = yf_ref[...].astype(F32) + yb_ref[...].astype(F32) + d_ref[...] * x_ref[...].astype(F32)
    y = y * _silu(z_ref[...].astype(F32))
    r = lax.rsqrt(jnp.mean(y * y, axis=-1, keepdims=True) + EPS)
    proj = jnp.dot((y * nw_ref[...]).astype(BF16), w_ref[...], preferred_element_type=F32)
    o_ref[...] = h_ref[...] + r * proj


def _ssd_out(yf, yb, xbc, z, d_e, nw, w, h):
    R = h.shape[0]
    tm = _pick_tile(R, (512, 384, 256, 128))
    wide = pl.BlockSpec((tm, D_INNER), lambda i: (i, 0))
    row = pl.BlockSpec((tm, D_MODEL), lambda i: (i, 0))
    return pl.pallas_call(
        _ssd_out_kernel,
        grid=(R // tm,),
        in_specs=[wide, wide, wide, wide] + [_const_spec(a.shape) for a in (d_e, nw, w)] + [row],
        out_specs=row,
        out_shape=jax.ShapeDtypeStruct((R, D_MODEL), F32),
        compiler_params=_params(("parallel",)),
        name="ssd_out",
    )(yf, yb, xbc, z, d_e, nw, w, h)


def _row(v, width=None):
    v = v.astype(F32).reshape(1, -1)
    if width is not None and v.shape[1] < width:
        v = jnp.pad(v, ((0, 0), (0, width - v.shape[1])))
    return v


def _prep_mla(w_in, q_norm, kv_norm, w_q_up, w_kv_up, q_head_norm, k_head_norm, w_out):
    lat = Q_LORA_RANK + KV_LORA_RANK
    rope_cols = jnp.pad(w_in[:, lat:], ((0, 0), (ROPE_LANE0, HEAD_PAD - QK_HEAD_DIM)))
    w_in_p = jnp.concatenate([w_in[:, :lat], rope_cols], axis=1).astype(BF16)
    w_q = w_q_up.reshape(Q_LORA_RANK, MLA_HEADS, QK_HEAD_DIM)
    w_q = jnp.pad(w_q, ((0, 0), (0, 0), (0, HEAD_PAD - QK_HEAD_DIM))).reshape(Q_LORA_RANK, -1).astype(BF16)
    w_kv = w_kv_up.reshape(KV_LORA_RANK, MLA_HEADS, QK_NOPE_DIM + V_HEAD_DIM)
    w_k = jnp.pad(w_kv[:, :, :QK_NOPE_DIM], ((0, 0), (0, 0), (0, HEAD_PAD - QK_NOPE_DIM)))
    w_k = w_k.reshape(KV_LORA_RANK, -1).astype(BF16)
    w_v = w_kv[:, :, QK_NOPE_DIM:].reshape(KV_LORA_RANK, -1).T.astype(BF16)
    scale = LOG2E / math.sqrt(QK_HEAD_DIM)
    gq = q_head_norm.astype(F32)
    gk = k_head_norm.astype(F32)
    bound = 1.02 * scale * QK_HEAD_DIM * jnp.max(jnp.abs(gq)) * jnp.max(jnp.abs(gk))
    return dict(
        w_in=w_in_p, q_norm=_row(q_norm), kv_norm=_row(kv_norm), w_q=w_q, w_k=w_k, w_v=w_v,
        bound=bound.reshape(1),
        gq=_row(gq * scale, HEAD_PAD),
        gkn=_row(gk[:QK_NOPE_DIM], HEAD_PAD),
        gkr=_row(jnp.pad(gk[QK_NOPE_DIM:], (ROPE_LANE0, 0)), HEAD_PAD),
        w_out=w_out.astype(BF16))


def _attention_tables(Tp):
    half = QK_ROPE_DIM // 2
    pos = jnp.arange(Tp, dtype=F32) - FRONT_PAD
    inv_freq = ROPE_BASE ** (-jnp.arange(half, dtype=F32) / half)
    ang = pos[:, None] * inv_freq[None, :]
    cos, sin = jnp.cos(ang), jnp.sin(ang)
    tail = HEAD_PAD - QK_HEAD_DIM
    cos_t = jnp.concatenate([jnp.ones((Tp, ROPE_LANE0), F32), cos, cos, jnp.ones((Tp, tail), F32)], axis=1)
    sin_t = jnp.concatenate([jnp.zeros((Tp, ROPE_LANE0), F32), -sin, sin, jnp.zeros((Tp, tail), F32)], axis=1)
    is_pad = (jnp.arange(Tp) < FRONT_PAD)[:, None]
    lane = jnp.arange(HEAD_PAD)[None, :]
    one = (lane == MASK_LANE).astype(F32)
    return dict(cos=cos_t, sin=sin_t, pad_mask=jnp.where(is_pad, MASK_VALUE, 0.0).astype(F32) * one, one=one)


def _key_bias(tables, bound):
    return jnp.where(tables['pad_mask'] < 0, tables['pad_mask'], -bound * tables['one'])


def _prep_ssd(w_in, conv_w, conv_b, dt_bias, a_log, d_skip, norm_w, w_out):
    w_in_p = jnp.pad(w_in, ((0, 0), (0, LANES - 2 * SSD_HEADS))).astype(BF16)
    return dict(
        w_in=w_in_p, conv_w=conv_w.astype(F32), conv_b=_row(conv_b),
        dt_bias=_row(dt_bias, LANES), a_log=_row(a_log, LANES),
        d_e=_row(jnp.repeat(d_skip.astype(F32), SSD_HEAD_DIM)),
        norm=_row(norm_w), w_out=w_out.astype(BF16))


def kernel(x_prompt, x_sample, meta_tokens, mix_norm, ffn_norm, mla_w_in, mla_q_norm, mla_kv_norm, mla_w_q_up, mla_w_kv_up, mla_q_head_norm, mla_k_head_norm, mla_w_out, ssd_w_in, ssd_conv_w, ssd_conv_b, ssd_dt_bias, ssd_a_log, ssd_d, ssd_norm, ssd_w_out, ffn_w_gate, ffn_w_up, ffn_w_down, final_norm):
    assert x_prompt.shape[1:] == x_sample.shape[1:] and x_prompt.shape[2] == D_MODEL
    assert x_prompt.shape[1] % CHUNK == 0
    depth = mix_norm.shape[0]
    n_prompt = x_prompt.shape[0]
    B = n_prompt + x_sample.shape[0]
    S = x_prompt.shape[1]
    Tp = FRONT_PAD + N_META + S
    head = jnp.concatenate([jnp.zeros((FRONT_PAD, D_MODEL), F32), meta_tokens.astype(F32)], axis=0)
    h = jnp.concatenate([jnp.broadcast_to(head[None], (B, CHUNK, D_MODEL)),
                         jnp.concatenate([x_prompt, x_sample], axis=0)], axis=1)
    tables = _attention_tables(Tp)
    attn_p = (mla_w_in, mla_q_norm, mla_kv_norm, mla_w_q_up, mla_w_kv_up,
              mla_q_head_norm, mla_k_head_norm, mla_w_out)
    ssd_p = (ssd_w_in, ssd_conv_w, ssd_conv_b, ssd_dt_bias, ssd_a_log, ssd_d, ssd_norm, ssd_w_out)
    flat = lambda t: t.reshape(B * Tp, t.shape[-1])
    ia = ib = 0
    for layer in range(depth):
        nw = _row(mix_norm[layer])
        if layer % 2 == 0:
            p = _prep_mla(*[t[ia] for t in attn_p])
            ia += 1
            q, k, v = _mla_proj(h, nw, p, tables)
            o = _flash(p['bound'], q, k, v)
            h2 = flat(_attn_out(o, p['w_out'], h))
        else:
            p = _prep_ssd(*[t[ib] for t in ssd_p])
            ib += 1
            z, xbc, *decays = _ssd_inproj(h, nw, p['w_in'], p['dt_bias'], p['a_log'])
            xbc = _ssd_conv(xbc, p['conv_w'], p['conv_b'])
            yf, yb = _ssd_scan(xbc, *decays)
            h2 = _ssd_out(flat(yf), flat(yb), flat(xbc), flat(z), p['d_e'], p['norm'], p['w_out'], flat(h))
        h2 = _ffn(h2, _row(ffn_norm[layer]), ffn_w_gate[layer].astype(BF16), ffn_w_up[layer].astype(BF16),
                  ffn_w_down[layer].astype(BF16), _row(final_norm), final=(layer == depth - 1))
        h = h2.reshape(B, Tp, D_MODEL)
    y = h[:, CHUNK:]
    return (y[:n_prompt], y[n_prompt:])
```

```python
import functools
import math

import jax
import jax.numpy as jnp
from jax import lax
from jax.experimental import pallas as pl
from jax.experimental.pallas import tpu as pltpu

F32 = jnp.float32
BF16 = jnp.bfloat16

D_MODEL = 1024
N_META = 16
EPS = 1e-6
MLA_HEADS = 16
QK_NOPE_DIM = 64
QK_ROPE_DIM = 32
QK_HEAD_DIM = QK_NOPE_DIM + QK_ROPE_DIM
V_HEAD_DIM = 64
Q_LORA_RANK = 384
KV_LORA_RANK = 256
ROPE_BASE = 10000.0
D_INNER = 2 * D_MODEL
SSD_HEAD_DIM = 64
SSD_HEADS = D_INNER // SSD_HEAD_DIM
SSD_GROUPS = 4
HEADS_PER_GROUP = SSD_HEADS // SSD_GROUPS
D_STATE = 128
D_CONV = 5
CONV_DIM = D_INNER + 2 * SSD_GROUPS * D_STATE
CHUNK = 128
FFN_HIDDEN = -(-8 * D_MODEL // (3 * 256)) * 256

LANES = 128
FRONT_PAD = CHUNK - N_META
HEAD_PAD = LANES
ROPE_LANE0 = QK_NOPE_DIM
MASK_LANE = QK_HEAD_DIM
MASK_VALUE = -1e30
LOG2E = 1.0 / math.log(2.0)
SOFTMAX_BOUND_LIMIT = 40.0
HEAD_GROUP = 4
GROUP_W = HEADS_PER_GROUP * SSD_HEAD_DIM
VMEM_LIMIT = 56 * 1024 * 1024


def _pick_tile(n, candidates):
    for c in candidates:
        if n % c == 0:
            return c
    return n


def _params(sem, vmem=VMEM_LIMIT):
    return pltpu.CompilerParams(dimension_semantics=sem, vmem_limit_bytes=vmem)


def _const_spec(shape):
    nd = len(shape)
    return pl.BlockSpec(shape, lambda *_: (0,) * nd, pipeline_mode=pl.Buffered(1))


def _rms(x, w):
    return x * lax.rsqrt(jnp.mean(x * x, axis=-1, keepdims=True) + EPS) * w


def _silu(x):
    half = 0.5 * x
    return half + half * jnp.tanh(half)


def _mla_proj_kernel(h_ref, nw_ref, win_ref, qn_ref, kvn_ref, wq_ref, wk_ref, wv_ref,
                     gq_ref, gkn_ref, gkr_ref, cos_ref, sin_ref, bias_ref, one_ref,
                     q_ref, k_ref, v_ref):
    u = _rms(h_ref[0], nw_ref[...]).astype(BF16)
    c = jnp.dot(u, win_ref[...], preferred_element_type=F32)
    cq = _rms(c[:, :Q_LORA_RANK], qn_ref[...]).astype(BF16)
    ckv = _rms(c[:, Q_LORA_RANK:Q_LORA_RANK + KV_LORA_RANK], kvn_ref[...]).astype(BF16)
    k_rope = c[:, Q_LORA_RANK + KV_LORA_RANK:]
    q = jnp.dot(cq, wq_ref[...], preferred_element_type=F32)
    k_nope = jnp.dot(ckv, wk_ref[...], preferred_element_type=F32)
    v_ref[0] = lax.dot_general(wv_ref[...], ckv, (((1,), (1,)), ((), ())),
                               preferred_element_type=F32).astype(BF16)

    cos = cos_ref[...]
    sin = sin_ref[...]
    lane = lax.broadcasted_iota(jnp.int32, cos.shape, 1)
    half = QK_ROPE_DIM // 2
    first_half = (lane >= ROPE_LANE0) & (lane < ROPE_LANE0 + half)

    def rope(x):
        swapped = jnp.where(first_half, pltpu.roll(x, HEAD_PAD - half, 1), pltpu.roll(x, half, 1))
        return x * cos + swapped * sin

    inv_d = 1.0 / QK_HEAD_DIM
    kr = rope(k_rope * gkr_ref[...])
    kr_ss = jnp.sum(k_rope * k_rope, axis=-1, keepdims=True)
    gq = gq_ref[...]
    gkn = gkn_ref[...]
    bias = bias_ref[...]
    one = one_ref[...]
    for hd in range(MLA_HEADS):
        sl = slice(hd * HEAD_PAD, (hd + 1) * HEAD_PAD)
        qh = q[:, sl]
        rq = lax.rsqrt(jnp.sum(qh * qh, axis=-1, keepdims=True) * inv_d + EPS)
        q_ref[0, hd] = (rope(qh * rq * gq) + one).astype(BF16)
        kh = k_nope[:, sl]
        rk = lax.rsqrt((jnp.sum(kh * kh, axis=-1, keepdims=True) + kr_ss) * inv_d + EPS)
        k_ref[0, hd] = ((kh * gkn + kr) * rk + bias).astype(BF16)


def _mla_proj(h, nw, p, tables):
    B, Tp, _ = h.shape
    tm = _pick_tile(Tp, (384, 128))
    row = lambda b, i: (b, i, 0)
    tab = lambda b, i: (i, 0)
    qk_shape = jax.ShapeDtypeStruct((B, MLA_HEADS, Tp, HEAD_PAD), BF16)
    qk_spec = pl.BlockSpec((1, MLA_HEADS, tm, HEAD_PAD), lambda b, i: (b, 0, i, 0))
    consts = (nw, p['w_in'], p['q_norm'], p['kv_norm'], p['w_q'], p['w_k'], p['w_v'],
              p['gq'], p['gkn'], p['gkr'])
    return pl.pallas_call(
        _mla_proj_kernel,
        grid=(B, Tp // tm),
        in_specs=[pl.BlockSpec((1, tm, D_MODEL), row)]
        + [_const_spec(a.shape) for a in consts]
        + [pl.BlockSpec((tm, HEAD_PAD), tab)] * 3
        + [_const_spec(tables['one'].shape)],
        out_specs=[qk_spec, qk_spec, pl.BlockSpec((1, MLA_HEADS * V_HEAD_DIM, tm), lambda b, i: (b, 0, i))],
        out_shape=[qk_shape, qk_shape, jax.ShapeDtypeStruct((B, MLA_HEADS * V_HEAD_DIM, Tp), BF16)],
        compiler_params=_params(("parallel", "parallel")),
        name="mla_proj",
    )(h, *consts, tables['cos'], tables['sin'], _key_bias(tables, p['bound']), tables['one'])


def _flash_kernel(bound_ref, q_ref, k_ref, v_ref, o_ref, *, key_chunk):
    Tp = k_ref.shape[2]
    tq = q_ref.shape[2]
    kq = lambda k, q: lax.dot_general(k, q, (((1,), (1,)), ((), ())), preferred_element_type=F32)

    def attend(bounded):
        for j in range(HEAD_GROUP):
            q = q_ref[0, j]
            rows = slice(j * V_HEAD_DIM, (j + 1) * V_HEAD_DIM)
            if bounded:
                l = jnp.zeros((1, tq), F32)
                o = jnp.zeros((V_HEAD_DIM, tq), F32)
                for c0 in range(0, Tp, key_chunk):
                    p = jnp.exp2(kq(k_ref[0, j, c0:c0 + key_chunk, :], q))
                    l = l + jnp.sum(p, axis=0, keepdims=True)
                    o = o + jnp.dot(v_ref[0, rows, c0:c0 + key_chunk], p.astype(BF16),
                                    preferred_element_type=F32)
            else:
                s = kq(k_ref[0, j], q)
                p = jnp.exp2(s - jnp.max(s, axis=0, keepdims=True))
                l = jnp.sum(p, axis=0, keepdims=True)
                o = jnp.dot(v_ref[0, rows, :], p.astype(BF16), preferred_element_type=F32)
            o_ref[0, rows, :] = (o * (1.0 / l)).astype(BF16)

    bounded = bound_ref[0] <= SOFTMAX_BOUND_LIMIT

    @pl.when(bounded)
    def _():
        attend(True)

    @pl.when(jnp.logical_not(bounded))
    def _():
        attend(False)


def _flash(bound, q, k, v):
    B, _, Tp, _ = q.shape
    tq = _pick_tile(Tp, (384, 128))
    gw = HEAD_GROUP * V_HEAD_DIM
    return pl.pallas_call(
        functools.partial(_flash_kernel, key_chunk=Tp),
        grid=(B, MLA_HEADS // HEAD_GROUP, Tp // tq),
        in_specs=[pl.BlockSpec(memory_space=pltpu.SMEM),
                  pl.BlockSpec((1, HEAD_GROUP, tq, HEAD_PAD), lambda b, g, i: (b, g, i, 0)),
                  pl.BlockSpec((1, HEAD_GROUP, Tp, HEAD_PAD), lambda b, g, i: (b, g, 0, 0)),
                  pl.BlockSpec((1, gw, Tp), lambda b, g, i: (b, g, 0))],
        out_specs=pl.BlockSpec((1, gw, tq), lambda b, g, i: (b, g, i)),
        out_shape=jax.ShapeDtypeStruct((B, MLA_HEADS * V_HEAD_DIM, Tp), BF16),
        compiler_params=_params(("parallel", "parallel", "parallel")),
        name="flash",
    )(bound, q, k, v)


def _attn_out_kernel(ot_ref, w_ref, h_ref, o_ref):
    proj = lax.dot_general(ot_ref[0], w_ref[...], (((0,), (0,)), ((), ())), preferred_element_type=F32)
    o_ref[0] = h_ref[0] + proj


def _attn_out(o_t, w, h):
    B, Tp, _ = h.shape
    tm = _pick_tile(Tp, (1408, 384, 128))
    row = pl.BlockSpec((1, tm, D_MODEL), lambda b, i: (b, i, 0))
    return pl.pallas_call(
        _attn_out_kernel,
        grid=(B, Tp // tm),
        in_specs=[pl.BlockSpec((1, o_t.shape[1], tm), lambda b, i: (b, 0, i)), _const_spec(w.shape), row],
        out_specs=row,
        out_shape=jax.ShapeDtypeStruct((B, Tp, D_MODEL), F32),
        compiler_params=_params(("parallel", "parallel")),
        name="attn_out",
    )(o_t, w, h)


def _hidden_chunks():
    bounds, start = [], 0
    while start < FFN_HIDDEN:
        stop = min(start + 1536, FFN_HIDDEN)
        bounds.append((start, stop))
        start = stop
    return bounds


def _ffn_kernel(h_ref, nw_ref, wg_ref, wu_ref, wd_ref, fw_ref, o_ref, *, final):
    h = h_ref[...]
    u = _rms(h, nw_ref[...]).astype(BF16)
    acc = h
    for lo, hi in _hidden_chunks():
        g = jnp.dot(u, wg_ref[:, lo:hi], preferred_element_type=F32)
        up = jnp.dot(u, wu_ref[:, lo:hi], preferred_element_type=F32)
        a = (_silu(g) * up).astype(BF16)
        acc = acc + jnp.dot(a, wd_ref[lo:hi, :], preferred_element_type=F32)
    if final:
        acc = _rms(acc, fw_ref[...])
    o_ref[...] = acc


def _ffn(h, nw, wg, wu, wd, fw, final):
    R = h.shape[0]
    tm = _pick_tile(R, (512, 384, 256, 128))
    row = pl.BlockSpec((tm, D_MODEL), lambda i: (i, 0))
    return pl.pallas_call(
        functools.partial(_ffn_kernel, final=final),
        grid=(R // tm,),
        in_specs=[row] + [_const_spec(a.shape) for a in (nw, wg, wu, wd, fw)],
        out_specs=row,
        out_shape=jax.ShapeDtypeStruct((R, D_MODEL), F32),
        compiler_params=_params(("parallel",)),
        name="ffn",
    )(h, nw, wg, wu, wd, fw)


SUBLANES = 8


def _ssd_inproj_kernel(h_ref, nw_ref, w_ref, dtb_ref, alog_ref, z_ref, xbc_ref, cs_ref, srct_ref, outwt_ref,
                       etot_ref):
    tm = h_ref.shape[1]
    u = _rms(h_ref[0], nw_ref[...]).astype(BF16)
    y = jnp.dot(u, w_ref[...], preferred_element_type=F32)
    row = pl.program_id(1) * tm + lax.broadcasted_iota(jnp.int32, (tm, 1), 0)
    valid = row >= FRONT_PAD
    z_ref[0] = y[:, :D_INNER].astype(BF16)
    xbc_ref[0] = jnp.where(valid, y[:, D_INNER:D_INNER + CONV_DIM], 0.0).astype(BF16)
    dt_raw = y[:, D_INNER + CONV_DIM:] + dtb_ref[...]
    dt_all = jnp.where(valid, jnp.maximum(dt_raw, 0.0) + jnp.log(1.0 + jnp.exp(-jnp.abs(dt_raw))), 0.0)

    neg_a = -jnp.exp(alog_ref[...]) * LOG2E
    forward = lax.broadcasted_iota(jnp.int32, (1, LANES), 1) < SSD_HEADS
    r = lax.broadcasted_iota(jnp.int32, (CHUNK, CHUNK), 0)
    c = lax.broadcasted_iota(jnp.int32, (CHUNK, CHUNK), 1)
    tri_f = (c <= r).astype(BF16)
    tri_b = (c >= r).astype(BF16)
    for ch in range(tm // CHUNK):
        rows = slice(ch * CHUNK, (ch + 1) * CHUNK)
        dt = dt_all[rows]
        rest = dt * neg_a
        cs_f = jnp.zeros((CHUNK, LANES), F32)
        cs_b = jnp.zeros((CHUNK, LANES), F32)
        for _ in range(3):
            part = rest.astype(BF16)
            cs_f = cs_f + jnp.dot(tri_f, part, preferred_element_type=F32)
            cs_b = cs_b + jnp.dot(tri_b, part, preferred_element_type=F32)
            rest = rest - part.astype(F32)
        cs = jnp.where(forward, cs_f, cs_b)
        total = jnp.where(forward, cs_f[CHUNK - 1:CHUNK, :], cs_b[0:1, :])
        cs_ref[0, rows, :] = cs
        srct_ref[0, :, rows] = (cs - jnp.log2(dt)).T
        outwt_ref[0, :, rows] = (jnp.exp2(total - cs) * dt).T
        etot_ref[0, ch * SUBLANES:(ch + 1) * SUBLANES, :] = jnp.broadcast_to(jnp.exp2(total), (SUBLANES, LANES))


def _ssd_inproj(h, nw, w, dtb, a_log):
    B, Tp, _ = h.shape
    tm = _pick_tile(Tp, (384, 128))
    row = lambda b, i: (b, i, 0)
    col = lambda b, i: (b, 0, i)
    per_chunk = SUBLANES * tm // CHUNK
    return pl.pallas_call(
        _ssd_inproj_kernel,
        grid=(B, Tp // tm),
        in_specs=[pl.BlockSpec((1, tm, D_MODEL), row)] + [_const_spec(a.shape) for a in (nw, w, dtb, a_log)],
        out_specs=[pl.BlockSpec((1, tm, D_INNER), row), pl.BlockSpec((1, tm, CONV_DIM), row),
                   pl.BlockSpec((1, tm, LANES), row), pl.BlockSpec((1, LANES, tm), col),
                   pl.BlockSpec((1, LANES, tm), col), pl.BlockSpec((1, per_chunk, LANES), row)],
        out_shape=[jax.ShapeDtypeStruct((B, Tp, D_INNER), BF16),
                   jax.ShapeDtypeStruct((B, Tp, CONV_DIM), BF16),
                   jax.ShapeDtypeStruct((B, Tp, LANES), F32),
                   jax.ShapeDtypeStruct((B, LANES, Tp), F32),
                   jax.ShapeDtypeStruct((B, LANES, Tp), F32),
                   jax.ShapeDtypeStruct((B, SUBLANES * Tp // CHUNK, LANES), F32)],
        compiler_params=_params(("parallel", "parallel")),
        name="ssd_inproj",
    )(h, nw, w, dtb, a_log)


CONV_MARGIN = 16
CONV_ROWS = 128
CONV_SHIFTS = tuple(t - D_CONV // 2 for t in range(D_CONV) if t != D_CONV // 2)


def _conv_kernel(x_ref, w_ref, b_ref, o_ref, xs_ref):
    Tp, cw = x_ref.shape[1], x_ref.shape[2]
    win = CONV_ROWS + 2 * CONV_MARGIN
    zeros = jnp.zeros((CONV_MARGIN, cw), BF16)
    xs_ref[0:CONV_MARGIN, :] = zeros
    xs_ref[CONV_MARGIN + Tp:, :] = zeros
    xs_ref[CONV_MARGIN:CONV_MARGIN + Tp, :] = x_ref[0]
    r = lax.broadcasted_iota(jnp.int32, (len(CONV_SHIFTS) * CONV_ROWS, win), 0)
    j = lax.broadcasted_iota(jnp.int32, (len(CONV_SHIFTS) * CONV_ROWS, win), 1)
    select = jnp.zeros(r.shape, F32)
    for k, shift in enumerate(CONV_SHIFTS):
        hit = (r >= k * CONV_ROWS) & (r < (k + 1) * CONV_ROWS) & (j == r - k * CONV_ROWS + CONV_MARGIN + shift)
        select = jnp.where(hit, 1.0, select)
    select = select.astype(BF16)
    w = [jnp.broadcast_to(w_ref[t:t + 1, :], (CONV_ROWS, cw)) for t in range(D_CONV)]
    b = jnp.broadcast_to(b_ref[...], (CONV_ROWS, cw))
    centre = D_CONV // 2

    def body(i, carry):
        r0 = pl.multiple_of(i * CONV_ROWS, CONV_ROWS)
        shifted = jnp.dot(select, xs_ref[pl.ds(r0, win), :], preferred_element_type=F32)
        acc = b + w[centre] * xs_ref[pl.ds(r0 + CONV_MARGIN, CONV_ROWS), :].astype(F32)
        for k, shift in enumerate(CONV_SHIFTS):
            acc = acc + w[centre + shift] * shifted[k * CONV_ROWS:(k + 1) * CONV_ROWS, :]
        o_ref[0, pl.ds(r0, CONV_ROWS), :] = _silu(acc).astype(BF16)
        return carry

    n_blocks = Tp // CONV_ROWS
    lax.fori_loop(0, n_blocks, body, 0, unroll=3 if n_blocks % 3 == 0 else 1)


def _ssd_conv(xbc, w, b):
    B, Tp, C = xbc.shape
    cw = 512
    blk = pl.BlockSpec((1, Tp, cw), lambda bi, ci: (bi, 0, ci))
    return pl.pallas_call(
        _conv_kernel,
        grid=(B, C // cw),
        in_specs=[blk, pl.BlockSpec((D_CONV, cw), lambda bi, ci: (0, ci)),
                  pl.BlockSpec((1, cw), lambda bi, ci: (0, ci))],
        out_specs=blk,
        out_shape=jax.ShapeDtypeStruct((B, Tp, C), BF16),
        scratch_shapes=[pltpu.VMEM((Tp + 2 * CONV_MARGIN, cw), BF16)],
        compiler_params=_params(("parallel", "parallel")),
        name="ssd_conv",
    )(xbc, w, b)


def _ssd_decays(cs_ref, srct_ref, outwt_ref, etot_ref, reverse):
    row = lax.broadcasted_iota(jnp.int32, (CHUNK, CHUNK), 0)
    col = lax.broadcasted_iota(jnp.int32, (CHUNK, CHUNK), 1)
    causal = (col >= row) if reverse else (col <= row)
    return dict(
        hide=jnp.where(causal, 0.0, -jnp.inf),
        cs=cs_ref[0], src_t=srct_ref[0], out_w_t=outwt_ref[0], e_tot=etot_ref[0, 0:1, :],
        head0=SSD_HEADS if reverse else 0)


def _ssd_group(b_ref, c_ref, g):
    bg = b_ref[0, :, g * D_STATE:(g + 1) * D_STATE]
    cg = c_ref[0, :, g * D_STATE:(g + 1) * D_STATE]
    gram = lax.dot_general(cg, bg, (((1,), (1,)), ((), ())), preferred_element_type=F32)
    return gram, cg.astype(F32), bg.astype(F32).T


def _ssd_pair(dec, grp, x_ref, y_ref, st_ref, pair, low):
    gram, cg_f, bg_t = grp
    lanes = slice(pair * LANES, (pair + 1) * LANES)
    xp = x_ref[0, :, lanes]
    state = st_ref[pair]
    rhs = jnp.concatenate([xp, state.astype(BF16)], axis=0)
    hd0 = dec['head0'] + 2 * pair
    ys, ups = [], []
    for hd in (hd0, hd0 + 1):
        cs_l = jnp.broadcast_to(dec['cs'][:, hd:hd + 1], (CHUNK, CHUNK))
        seg = cs_l - dec['src_t'][hd:hd + 1, :]
        within = gram * jnp.exp2(seg + dec['hide'])
        from_state = cg_f * jnp.exp2(cs_l)
        lhs = jnp.concatenate([within, from_state], axis=1).astype(BF16)
        ys.append(jnp.dot(lhs, rhs, preferred_element_type=F32))
        ups.append(jnp.dot((bg_t * dec['out_w_t'][hd:hd + 1, :]).astype(BF16), xp, preferred_element_type=F32))
    y_ref[0, :, lanes] = jnp.where(low, ys[0], ys[1]).astype(y_ref.dtype)
    e_tot = dec['e_tot']
    keep = jnp.where(low[0:1, :], jnp.broadcast_to(e_tot[:, hd0:hd0 + 1], (1, LANES)),
                     jnp.broadcast_to(e_tot[:, hd0 + 1:hd0 + 2], (1, LANES)))
    st_ref[pair] = state * keep + jnp.where(low, ups[0], ups[1])


def _ssd_scan_kernel(xf_ref, bf_ref, cf_ref, csf_ref, srcf_ref, outf_ref, etotf_ref,
                     xb_ref, bb_ref, cb_ref, csb_ref, srcb_ref, outb_ref, etotb_ref,
                     yf_ref, yb_ref, stf_ref, stb_ref):
    @pl.when(pl.program_id(1) == 0)
    def _():
        stf_ref[...] = jnp.zeros(stf_ref.shape, F32)
        stb_ref[...] = jnp.zeros(stb_ref.shape, F32)

    low = lax.broadcasted_iota(jnp.int32, (CHUNK, LANES), 1) < SSD_HEAD_DIM
    dec_f = _ssd_decays(csf_ref, srcf_ref, outf_ref, etotf_ref, reverse=False)
    dec_b = _ssd_decays(csb_ref, srcb_ref, outb_ref, etotb_ref, reverse=True)
    pairs_per_group = HEADS_PER_GROUP // 2
    for g in range(SSD_GROUPS):
        grp_f = _ssd_group(bf_ref, cf_ref, g)
        grp_b = _ssd_group(bb_ref, cb_ref, g)
        for pr in range(pairs_per_group):
            pair = g * pairs_per_group + pr
            _ssd_pair(dec_f, grp_f, xf_ref, yf_ref, stf_ref, pair, low)
            _ssd_pair(dec_b, grp_b, xb_ref, yb_ref, stb_ref, pair, low)


def _ssd_scan(xbc, cs, src_t, out_w_t, e_tot):
    B, Tp, _ = xbc.shape
    nc = Tp // CHUNK
    bc_w = SSD_GROUPS * D_STATE
    b_blk = D_INNER // bc_w

    def specs(pos):
        return [pl.BlockSpec((1, CHUNK, D_INNER), lambda b, c: (b, pos(c), 0)),
                pl.BlockSpec((1, CHUNK, bc_w), lambda b, c: (b, pos(c), b_blk)),
                pl.BlockSpec((1, CHUNK, bc_w), lambda b, c: (b, pos(c), b_blk + 1)),
                pl.BlockSpec((1, CHUNK, LANES), lambda b, c: (b, pos(c), 0)),
                pl.BlockSpec((1, LANES, CHUNK), lambda b, c: (b, 0, pos(c))),
                pl.BlockSpec((1, LANES, CHUNK), lambda b, c: (b, 0, pos(c))),
                pl.BlockSpec((1, SUBLANES, LANES), lambda b, c: (b, pos(c), 0))]

    fwd = lambda c: c
    bwd = lambda c: nc - 1 - c
    y_shape = jax.ShapeDtypeStruct((B, Tp, D_INNER), BF16)
    state = pltpu.VMEM((SSD_HEADS // 2, D_STATE, LANES), F32)
    return pl.pallas_call(
        _ssd_scan_kernel,
        grid=(B, nc),
        in_specs=specs(fwd) + specs(bwd),
        out_specs=[pl.BlockSpec((1, CHUNK, D_INNER), lambda b, c: (b, fwd(c), 0)),
                   pl.BlockSpec((1, CHUNK, D_INNER), lambda b, c: (b, bwd(c), 0))],
        out_shape=[y_shape, y_shape],
        scratch_shapes=[state, state],
        compiler_params=_params(("parallel", "arbitrary")),
        name="ssd_scan",
    )(*(2 * (xbc, xbc, xbc, cs, src_t, out_w_t, e_tot)))


def _ssd_out_kernel(yf_ref, yb_ref, x_ref, z_ref, d_ref, nw_ref, w_ref, h_ref, o_ref):
    y = yf_ref[...].astype(F32) + yb_ref[...].astype(F32) + d_ref[...] * x_ref[...].astype(F32)
    y = y * _silu(z_ref[...].astype(F32))
    r = lax.rsqrt(jnp.mean(y * y, axis=-1, keepdims=True) + EPS)
    proj = jnp.dot((y * nw_ref[...]).astype(BF16), w_ref[...], preferred_element_type=F32)
    o_ref[...] = h_ref[...] + r * proj


def _ssd_out(yf, yb, xbc, z, d_e, nw, w, h):
    R = h.shape[0]
    tm = _pick_tile(R, (512, 384, 256, 128))
    wide = pl.BlockSpec((tm, D_INNER), lambda i: (i, 0))
    row = pl.BlockSpec((tm, D_MODEL), lambda i: (i, 0))
    return pl.pallas_call(
        _ssd_out_kernel,
        grid=(R // tm,),
        in_specs=[wide, wide, wide, wide] + [_const_spec(a.shape) for a in (d_e, nw, w)] + [row],
        out_specs=row,
        out_shape=jax.ShapeDtypeStruct((R, D_MODEL), F32),
        compiler_params=_params(("parallel",)),
        name="ssd_out",
    )(yf, yb, xbc, z, d_e, nw, w, h)


def _row(v, width=None):
    v = v.astype(F32).reshape(1, -1)
    if width is not None and v.shape[1] < width:
        v = jnp.pad(v, ((0, 0), (0, width - v.shape[1])))
    return v


def _prep_mla(w_in, q_norm, kv_norm, w_q_up, w_kv_up, q_head_norm, k_head_norm, w_out):
    lat = Q_LORA_RANK + KV_LORA_RANK
    rope_cols = jnp.pad(w_in[:, lat:], ((0, 0), (ROPE_LANE0, HEAD_PAD - QK_HEAD_DIM)))
    w_in_p = jnp.concatenate([w_in[:, :lat], rope_cols], axis=1).astype(BF16)
    w_q = w_q_up.reshape(Q_LORA_RANK, MLA_HEADS, QK_HEAD_DIM)
    w_q = jnp.pad(w_q, ((0, 0), (0, 0), (0, HEAD_PAD - QK_HEAD_DIM))).reshape(Q_LORA_RANK, -1).astype(BF16)
    w_kv = w_kv_up.reshape(KV_LORA_RANK, MLA_HEADS, QK_NOPE_DIM + V_HEAD_DIM)
    w_k = jnp.pad(w_kv[:, :, :QK_NOPE_DIM], ((0, 0), (0, 0), (0, HEAD_PAD - QK_NOPE_DIM)))
    w_k = w_k.reshape(KV_LORA_RANK, -1).astype(BF16)
    w_v = w_kv[:, :, QK_NOPE_DIM:].reshape(KV_LORA_RANK, -1).T.astype(BF16)
    scale = LOG2E / math.sqrt(QK_HEAD_DIM)
    gq = q_head_norm.astype(F32)
    gk = k_head_norm.astype(F32)
    bound = 1.02 * scale * QK_HEAD_DIM * jnp.max(jnp.abs(gq)) * jnp.max(jnp.abs(gk))
    return dict(
        w_in=w_in_p, q_norm=_row(q_norm), kv_norm=_row(kv_norm), w_q=w_q, w_k=w_k, w_v=w_v,
        bound=bound.reshape(1),
        gq=_row(gq * scale, HEAD_PAD),
        gkn=_row(gk[:QK_NOPE_DIM], HEAD_PAD),
        gkr=_row(jnp.pad(gk[QK_NOPE_DIM:], (ROPE_LANE0, 0)), HEAD_PAD),
        w_out=w_out.astype(BF16))


def _attention_tables(Tp):
    half = QK_ROPE_DIM // 2
    pos = jnp.arange(Tp, dtype=F32) - FRONT_PAD
    inv_freq = ROPE_BASE ** (-jnp.arange(half, dtype=F32) / half)
    ang = pos[:, None] * inv_freq[None, :]
    cos, sin = jnp.cos(ang), jnp.sin(ang)
    tail = HEAD_PAD - QK_HEAD_DIM
    cos_t = jnp.concatenate([jnp.ones((Tp, ROPE_LANE0), F32), cos, cos, jnp.ones((Tp, tail), F32)], axis=1)
    sin_t = jnp.concatenate([jnp.zeros((Tp, ROPE_LANE0), F32), -sin, sin, jnp.zeros((Tp, tail), F32)], axis=1)
    is_pad = (jnp.arange(Tp) < FRONT_PAD)[:, None]
    lane = jnp.arange(HEAD_PAD)[None, :]
    one = (lane == MASK_LANE).astype(F32)
    return dict(cos=cos_t, sin=sin_t, pad_mask=jnp.where(is_pad, MASK_VALUE, 0.0).astype(F32) * one, one=one)


def _key_bias(tables, bound):
    return jnp.where(tables['pad_mask'] < 0, tables['pad_mask'], -bound * tables['one'])


def _prep_ssd(w_in, conv_w, conv_b, dt_bias, a_log, d_skip, norm_w, w_out):
    w_in_p = jnp.pad(w_in, ((0, 0), (0, LANES - 2 * SSD_HEADS))).astype(BF16)
    return dict(
        w_in=w_in_p, conv_w=conv_w.astype(F32), conv_b=_row(conv_b),
        dt_bias=_row(dt_bias, LANES), a_log=_row(a_log, LANES),
        d_e=_row(jnp.repeat(d_skip.astype(F32), SSD_HEAD_DIM)),
        norm=_row(norm_w), w_out=w_out.astype(BF16))


def kernel(x_prompt, x_sample, meta_tokens, mix_norm, ffn_norm, mla_w_in, mla_q_norm, mla_kv_norm, mla_w_q_up, mla_w_kv_up, mla_q_head_norm, mla_k_head_norm, mla_w_out, ssd_w_in, ssd_conv_w, ssd_conv_b, ssd_dt_bias, ssd_a_log, ssd_d, ssd_norm, ssd_w_out, ffn_w_gate, ffn_w_up, ffn_w_down, final_norm):
    assert x_prompt.shape[1:] == x_sample.shape[1:] and x_prompt.shape[2] == D_MODEL
    assert x_prompt.shape[1] % CHUNK == 0
    depth = mix_norm.shape[0]
    n_prompt = x_prompt.shape[0]
    B = n_prompt + x_sample.shape[0]
    S = x_prompt.shape[1]
    Tp = FRONT_PAD + N_META + S
    head = jnp.concatenate([jnp.zeros((FRONT_PAD, D_MODEL), F32), meta_tokens.astype(F32)], axis=0)
    h = jnp.concatenate([jnp.broadcast_to(head[None], (B, CHUNK, D_MODEL)),
                         jnp.concatenate([x_prompt, x_sample], axis=0)], axis=1)
    tables = _attention_tables(Tp)
    attn_p = (mla_w_in, mla_q_norm, mla_kv_norm, mla_w_q_up, mla_w_kv_up,
              mla_q_head_norm, mla_k_head_norm, mla_w_out)
    ssd_p = (ssd_w_in, ssd_conv_w, ssd_conv_b, ssd_dt_bias, ssd_a_log, ssd_d, ssd_norm, ssd_w_out)
    flat = lambda t: t.reshape(B * Tp, t.shape[-1])
    ia = ib = 0
    for layer in range(depth):
        nw = _row(mix_norm[layer])
        if layer % 2 == 0:
            p = _prep_mla(*[t[ia] for t in attn_p])
            ia += 1
            q, k, v = _mla_proj(h, nw, p, tables)
            o = _flash(p['bound'], q, k, v)
            h2 = flat(_attn_out(o, p['w_out'], h))
        else:
            p = _prep_ssd(*[t[ib] for t in ssd_p])
            ib += 1
            z, xbc, *decays = _ssd_inproj(h, nw, p['w_in'], p['dt_bias'], p['a_log'])
            xbc = _ssd_conv(xbc, p['conv_w'], p['conv_b'])
            yf, yb = _ssd_scan(xbc, *decays)
            h2 = _ssd_out(flat(yf), flat(yb), flat(xbc), flat(z), p['d_e'], p['norm'], p['w_out'], flat(h))
        h2 = _ffn(h2, _row(ffn_norm[layer]), ffn_w_gate[layer].astype(BF16), ffn_w_up[layer].astype(BF16),
                  ffn_w_down[layer].astype(BF16), _row(final_norm), final=(layer == depth - 1))
        h = h2.reshape(B, Tp, D_MODEL)
    y = h[:, CHUNK:]
    return (y[:n_prompt], y[n_prompt:])
```

```python
import functools
import math

import jax
import jax.numpy as jnp
from jax import lax
from jax.experimental import pallas as pl
from jax.experimental.pallas import tpu as pltpu

F32 = jnp.float32
BF16 = jnp.bfloat16

D_MODEL = 1024
N_META = 16
EPS = 1e-6
MLA_HEADS = 16
QK_NOPE_DIM = 64
QK_ROPE_DIM = 32
QK_HEAD_DIM = QK_NOPE_DIM + QK_ROPE_DIM
V_HEAD_DIM = 64
Q_LORA_RANK = 384
KV_LORA_RANK = 256
ROPE_BASE = 10000.0
D_INNER = 2 * D_MODEL
SSD_HEAD_DIM = 64
SSD_HEADS = D_INNER // SSD_HEAD_DIM
SSD_GROUPS = 4
HEADS_PER_GROUP = SSD_HEADS // SSD_GROUPS
D_STATE = 128
D_CONV = 5
CONV_DIM = D_INNER + 2 * SSD_GROUPS * D_STATE
CHUNK = 128
FFN_HIDDEN = -(-8 * D_MODEL // (3 * 256)) * 256

LANES = 128
FRONT_PAD = CHUNK - N_META
HEAD_PAD = LANES
ROPE_LANE0 = QK_NOPE_DIM
MASK_LANE = QK_HEAD_DIM
MASK_VALUE = -1e30
LOG2E = 1.0 / math.log(2.0)
SOFTMAX_BOUND_LIMIT = 40.0
HEAD_GROUP = 4
GROUP_W = HEADS_PER_GROUP * SSD_HEAD_DIM
VMEM_LIMIT = 56 * 1024 * 1024


def _pick_tile(n, candidates):
    for c in candidates:
        if n % c == 0:
            return c
    return n


def _params(sem, vmem=VMEM_LIMIT):
    return pltpu.CompilerParams(dimension_semantics=sem, vmem_limit_bytes=vmem)


def _const_spec(shape):
    nd = len(shape)
    return pl.BlockSpec(shape, lambda *_: (0,) * nd, pipeline_mode=pl.Buffered(1))


def _rms(x, w):
    return x * lax.rsqrt(jnp.mean(x * x, axis=-1, keepdims=True) + EPS) * w


def _silu(x):
    half = 0.5 * x
    return half + half * jnp.tanh(half)


def _mla_proj_kernel(h_ref, nw_ref, win_ref, qn_ref, kvn_ref, wq_ref, wk_ref, wv_ref,
                     gq_ref, gkn_ref, gkr_ref, cos_ref, sin_ref, bias_ref, one_ref,
                     q_ref, k_ref, v_ref):
    u = _rms(h_ref[0], nw_ref[...]).astype(BF16)
    c = jnp.dot(u, win_ref[...], preferred_element_type=F32)
    cq = _rms(c[:, :Q_LORA_RANK], qn_ref[...]).astype(BF16)
    ckv = _rms(c[:, Q_LORA_RANK:Q_LORA_RANK + KV_LORA_RANK], kvn_ref[...]).astype(BF16)
    k_rope = c[:, Q_LORA_RANK + KV_LORA_RANK:]
    q = jnp.dot(cq, wq_ref[...], preferred_element_type=F32)
    k_nope = jnp.dot(ckv, wk_ref[...], preferred_element_type=F32)
    v_ref[0] = lax.dot_general(wv_ref[...], ckv, (((1,), (1,)), ((), ())),
                               preferred_element_type=F32).astype(BF16)

    cos = cos_ref[...]
    sin = sin_ref[...]
    lane = lax.broadcasted_iota(jnp.int32, cos.shape, 1)
    half = QK_ROPE_DIM // 2
    first_half = (lane >= ROPE_LANE0) & (lane < ROPE_LANE0 + half)

    def rope(x):
        swapped = jnp.where(first_half, pltpu.roll(x, HEAD_PAD - half, 1), pltpu.roll(x, half, 1))
        return x * cos + swapped * sin

    inv_d = 1.0 / QK_HEAD_DIM
    kr = rope(k_rope * gkr_ref[...])
    kr_ss = jnp.sum(k_rope * k_rope, axis=-1, keepdims=True)
    gq = gq_ref[...]
    gkn = gkn_ref[...]
    bias = bias_ref[...]
    one = one_ref[...]
    for hd in range(MLA_HEADS):
        sl = slice(hd * HEAD_PAD, (hd + 1) * HEAD_PAD)
        qh = q[:, sl]
        rq = lax.rsqrt(jnp.sum(qh * qh, axis=-1, keepdims=True) * inv_d + EPS)
        q_ref[0, hd] = (rope(qh * rq * gq) + one).astype(BF16)
        kh = k_nope[:, sl]
        rk = lax.rsqrt((jnp.sum(kh * kh, axis=-1, keepdims=True) + kr_ss) * inv_d + EPS)
        k_ref[0, hd] = ((kh * gkn + kr) * rk + bias).astype(BF16)


def _mla_proj(h, nw, p, tables):
    B, Tp, _ = h.shape
    tm = _pick_tile(Tp, (384, 128))
    row = lambda b, i: (b, i, 0)
    tab = lambda b, i: (i, 0)
    qk_shape = jax.ShapeDtypeStruct((B, MLA_HEADS, Tp, HEAD_PAD), BF16)
    qk_spec = pl.BlockSpec((1, MLA_HEADS, tm, HEAD_PAD), lambda b, i: (b, 0, i, 0))
    consts = (nw, p['w_in'], p['q_norm'], p['kv_norm'], p['w_q'], p['w_k'], p['w_v'],
              p['gq'], p['gkn'], p['gkr'])
    return pl.pallas_call(
        _mla_proj_kernel,
        grid=(B, Tp // tm),
        in_specs=[pl.BlockSpec((1, tm, D_MODEL), row)]
        + [_const_spec(a.shape) for a in consts]
        + [pl.BlockSpec((tm, HEAD_PAD), tab)] * 3
        + [_const_spec(tables['one'].shape)],
        out_specs=[qk_spec, qk_spec, pl.BlockSpec((1, MLA_HEADS * V_HEAD_DIM, tm), lambda b, i: (b, 0, i))],
        out_shape=[qk_shape, qk_shape, jax.ShapeDtypeStruct((B, MLA_HEADS * V_HEAD_DIM, Tp), BF16)],
        compiler_params=_params(("parallel", "parallel")),
        name="mla_proj",
    )(h, *consts, tables['cos'], tables['sin'], _key_bias(tables, p['bound']), tables['one'])


def _flash_kernel(bound_ref, q_ref, k_ref, v_ref, o_ref, *, key_chunk):
    Tp = k_ref.shape[2]
    tq = q_ref.shape[2]
    kq = lambda k, q: lax.dot_general(k, q, (((1,), (1,)), ((), ())), preferred_element_type=F32)

    chunks = range(0, Tp, key_chunk)

    def head(j, bounded):
        q = q_ref[0, j]
        r0 = j * V_HEAD_DIM
        rows = pl.ds(r0 if isinstance(j, int) else pl.multiple_of(r0, V_HEAD_DIM), V_HEAD_DIM)
        scores = lambda c0: kq(k_ref[0, j, c0:c0 + key_chunk, :], q)
        if bounded:
            shift = None
        else:
            shift = jnp.full((1, tq), -jnp.inf, F32)
            for c0 in chunks:
                shift = jnp.maximum(shift, jnp.max(scores(c0), axis=0, keepdims=True))
        l = jnp.zeros((1, tq), F32)
        o = jnp.zeros((V_HEAD_DIM, tq), F32)
        for c0 in chunks:
            s = scores(c0)
            p = jnp.exp2(s if bounded else s - shift)
            l = l + jnp.sum(p, axis=0, keepdims=True)
            o = o + jnp.dot(v_ref[0, rows, c0:c0 + key_chunk], p.astype(BF16), preferred_element_type=F32)
        o_ref[0, rows, :] = (o * (1.0 / l)).astype(BF16)

    bounded = bound_ref[0] <= SOFTMAX_BOUND_LIMIT

    @pl.when(bounded)
    def _():
        for j in range(HEAD_GROUP):
            head(j, True)

    @pl.when(jnp.logical_not(bounded))
    def _():
        def body(j, carry):
            head(j, False)
            return carry
        lax.fori_loop(0, HEAD_GROUP, body, 0)


def _flash(bound, q, k, v):
    B, _, Tp, _ = q.shape
    tq = _pick_tile(Tp, (1408, 384, 128))
    gw = HEAD_GROUP * V_HEAD_DIM
    return pl.pallas_call(
        functools.partial(_flash_kernel, key_chunk=_pick_tile(Tp, (1408, 384, 128))),
        grid=(B, MLA_HEADS // HEAD_GROUP, Tp // tq),
        in_specs=[pl.BlockSpec(memory_space=pltpu.SMEM),
                  pl.BlockSpec((1, HEAD_GROUP, tq, HEAD_PAD), lambda b, g, i: (b, g, i, 0)),
                  pl.BlockSpec((1, HEAD_GROUP, Tp, HEAD_PAD), lambda b, g, i: (b, g, 0, 0)),
                  pl.BlockSpec((1, gw, Tp), lambda b, g, i: (b, g, 0))],
        out_specs=pl.BlockSpec((1, gw, tq), lambda b, g, i: (b, g, i)),
        out_shape=jax.ShapeDtypeStruct((B, MLA_HEADS * V_HEAD_DIM, Tp), BF16),
        compiler_params=_params(("parallel", "parallel", "parallel")),
        name="flash",
    )(bound, q, k, v)


def _attn_out_kernel(ot_ref, w_ref, h_ref, o_ref):
    proj = lax.dot_general(ot_ref[0], w_ref[...], (((0,), (0,)), ((), ())), preferred_element_type=F32)
    o_ref[0] = h_ref[0] + proj


def _attn_out(o_t, w, h):
    B, Tp, _ = h.shape
    tm = _pick_tile(Tp, (1408, 384, 128))
    row = pl.BlockSpec((1, tm, D_MODEL), lambda b, i: (b, i, 0))
    return pl.pallas_call(
        _attn_out_kernel,
        grid=(B, Tp // tm),
        in_specs=[pl.BlockSpec((1, o_t.shape[1], tm), lambda b, i: (b, 0, i)), _const_spec(w.shape), row],
        out_specs=row,
        out_shape=jax.ShapeDtypeStruct((B, Tp, D_MODEL), F32),
        compiler_params=_params(("parallel", "parallel")),
        name="attn_out",
    )(o_t, w, h)


def _hidden_chunks():
    bounds, start = [], 0
    while start < FFN_HIDDEN:
        stop = min(start + 1536, FFN_HIDDEN)
        bounds.append((start, stop))
        start = stop
    return bounds


def _ffn_kernel(h_ref, nw_ref, wg_ref, wu_ref, wd_ref, fw_ref, o_ref, *, final):
    h = h_ref[...]
    u = _rms(h, nw_ref[...]).astype(BF16)
    acc = h
    for lo, hi in _hidden_chunks():
        g = jnp.dot(u, wg_ref[:, lo:hi], preferred_element_type=F32)
        up = jnp.dot(u, wu_ref[:, lo:hi], preferred_element_type=F32)
        a = (_silu(g) * up).astype(BF16)
        acc = acc + jnp.dot(a, wd_ref[lo:hi, :], preferred_element_type=F32)
    if final:
        acc = _rms(acc, fw_ref[...])
    o_ref[...] = acc


def _ffn(h, nw, wg, wu, wd, fw, final):
    R = h.shape[0]
    tm = _pick_tile(R, (512, 384, 256, 128))
    row = pl.BlockSpec((tm, D_MODEL), lambda i: (i, 0))
    return pl.pallas_call(
        functools.partial(_ffn_kernel, final=final),
        grid=(R // tm,),
        in_specs=[row] + [_const_spec(a.shape) for a in (nw, wg, wu, wd, fw)],
        out_specs=row,
        out_shape=jax.ShapeDtypeStruct((R, D_MODEL), F32),
        compiler_params=_params(("parallel",)),
        name="ffn",
    )(h, nw, wg, wu, wd, fw)


SUBLANES = 8


def _ssd_inproj_kernel(h_ref, nw_ref, w_ref, dtb_ref, alog_ref, z_ref, xbc_ref, cs_ref, srct_ref, outwt_ref,
                       etot_ref):
    tm = h_ref.shape[1]
    u = _rms(h_ref[0], nw_ref[...]).astype(BF16)
    y = jnp.dot(u, w_ref[...], preferred_element_type=F32)
    row = pl.program_id(1) * tm + lax.broadcasted_iota(jnp.int32, (tm, 1), 0)
    valid = row >= FRONT_PAD
    z_ref[0] = y[:, :D_INNER].astype(BF16)
    xbc_ref[0] = jnp.where(valid, y[:, D_INNER:D_INNER + CONV_DIM], 0.0).astype(BF16)
    dt_raw = y[:, D_INNER + CONV_DIM:] + dtb_ref[...]
    dt_all = jnp.where(valid, jnp.maximum(dt_raw, 0.0) + jnp.log(1.0 + jnp.exp(-jnp.abs(dt_raw))), 0.0)

    neg_a = -jnp.exp(alog_ref[...]) * LOG2E
    forward = lax.broadcasted_iota(jnp.int32, (1, LANES), 1) < SSD_HEADS
    r = lax.broadcasted_iota(jnp.int32, (CHUNK, CHUNK), 0)
    c = lax.broadcasted_iota(jnp.int32, (CHUNK, CHUNK), 1)
    tri_f = (c <= r).astype(BF16)
    tri_b = (c >= r).astype(BF16)
    for ch in range(tm // CHUNK):
        rows = slice(ch * CHUNK, (ch + 1) * CHUNK)
        dt = dt_all[rows]
        rest = dt * neg_a
        cs_f = jnp.zeros((CHUNK, LANES), F32)
        cs_b = jnp.zeros((CHUNK, LANES), F32)
        for _ in range(3):
            part = rest.astype(BF16)
            cs_f = cs_f + jnp.dot(tri_f, part, preferred_element_type=F32)
            cs_b = cs_b + jnp.dot(tri_b, part, preferred_element_type=F32)
            rest = rest - part.astype(F32)
        cs = jnp.where(forward, cs_f, cs_b)
        total = jnp.where(forward, cs_f[CHUNK - 1:CHUNK, :], cs_b[0:1, :])
        cs_ref[0, rows, :] = cs
        srct_ref[0, :, rows] = (cs - jnp.log2(dt)).T
        outwt_ref[0, :, rows] = (jnp.exp2(total - cs) * dt).T
        etot_ref[0, ch * SUBLANES:(ch + 1) * SUBLANES, :] = jnp.broadcast_to(jnp.exp2(total), (SUBLANES, LANES))


def _ssd_inproj(h, nw, w, dtb, a_log):
    B, Tp, _ = h.shape
    tm = _pick_tile(Tp, (384, 128))
    row = lambda b, i: (b, i, 0)
    col = lambda b, i: (b, 0, i)
    per_chunk = SUBLANES * tm // CHUNK
    return pl.pallas_call(
        _ssd_inproj_kernel,
        grid=(B, Tp // tm),
        in_specs=[pl.BlockSpec((1, tm, D_MODEL), row)] + [_const_spec(a.shape) for a in (nw, w, dtb, a_log)],
        out_specs=[pl.BlockSpec((1, tm, D_INNER), row), pl.BlockSpec((1, tm, CONV_DIM), row),
                   pl.BlockSpec((1, tm, LANES), row), pl.BlockSpec((1, LANES, tm), col),
                   pl.BlockSpec((1, LANES, tm), col), pl.BlockSpec((1, per_chunk, LANES), row)],
        out_shape=[jax.ShapeDtypeStruct((B, Tp, D_INNER), BF16),
                   jax.ShapeDtypeStruct((B, Tp, CONV_DIM), BF16),
                   jax.ShapeDtypeStruct((B, Tp, LANES), F32),
                   jax.ShapeDtypeStruct((B, LANES, Tp), F32),
                   jax.ShapeDtypeStruct((B, LANES, Tp), F32),
                   jax.ShapeDtypeStruct((B, SUBLANES * Tp // CHUNK, LANES), F32)],
        compiler_params=_params(("parallel", "parallel")),
        name="ssd_inproj",
    )(h, nw, w, dtb, a_log)


CONV_MARGIN = 16
CONV_ROWS = 128
CONV_SHIFTS = tuple(t - D_CONV // 2 for t in range(D_CONV) if t != D_CONV // 2)


def _conv_kernel(x_ref, w_ref, b_ref, o_ref, xs_ref):
    Tp, cw = x_ref.shape[1], x_ref.shape[2]
    win = CONV_ROWS + 2 * CONV_MARGIN
    zeros = jnp.zeros((CONV_MARGIN, cw), BF16)
    xs_ref[0:CONV_MARGIN, :] = zeros
    xs_ref[CONV_MARGIN + Tp:, :] = zeros
    xs_ref[CONV_MARGIN:CONV_MARGIN + Tp, :] = x_ref[0]
    r = lax.broadcasted_iota(jnp.int32, (len(CONV_SHIFTS) * CONV_ROWS, win), 0)
    j = lax.broadcasted_iota(jnp.int32, (len(CONV_SHIFTS) * CONV_ROWS, win), 1)
    select = jnp.zeros(r.shape, F32)
    for k, shift in enumerate(CONV_SHIFTS):
        hit = (r >= k * CONV_ROWS) & (r < (k + 1) * CONV_ROWS) & (j == r - k * CONV_ROWS + CONV_MARGIN + shift)
        select = jnp.where(hit, 1.0, select)
    select = select.astype(BF16)
    w = [jnp.broadcast_to(w_ref[t:t + 1, :], (CONV_ROWS, cw)) for t in range(D_CONV)]
    b = jnp.broadcast_to(b_ref[...], (CONV_ROWS, cw))
    centre = D_CONV // 2

    def body(i, carry):
        r0 = pl.multiple_of(i * CONV_ROWS, CONV_ROWS)
        shifted = jnp.dot(select, xs_ref[pl.ds(r0, win), :], preferred_element_type=F32)
        acc = b + w[centre] * xs_ref[pl.ds(r0 + CONV_MARGIN, CONV_ROWS), :].astype(F32)
        for k, shift in enumerate(CONV_SHIFTS):
            acc = acc + w[centre + shift] * shifted[k * CONV_ROWS:(k + 1) * CONV_ROWS, :]
        o_ref[0, pl.ds(r0, CONV_ROWS), :] = _silu(acc).astype(BF16)
        return carry

    n_blocks = Tp // CONV_ROWS
    lax.fori_loop(0, n_blocks, body, 0, unroll=3 if n_blocks % 3 == 0 else 1)


def _ssd_conv(xbc, w, b):
    B, Tp, C = xbc.shape
    cw = 512
    blk = pl.BlockSpec((1, Tp, cw), lambda bi, ci: (bi, 0, ci))
    return pl.pallas_call(
        _conv_kernel,
        grid=(B, C // cw),
        in_specs=[blk, pl.BlockSpec((D_CONV, cw), lambda bi, ci: (0, ci)),
                  pl.BlockSpec((1, cw), lambda bi, ci: (0, ci))],
        out_specs=blk,
        out_shape=jax.ShapeDtypeStruct((B, Tp, C), BF16),
        scratch_shapes=[pltpu.VMEM((Tp + 2 * CONV_MARGIN, cw), BF16)],
        compiler_params=_params(("parallel", "parallel")),
        name="ssd_conv",
    )(xbc, w, b)


def _ssd_decays(cs_ref, srct_ref, outwt_ref, etot_ref, reverse):
    row = lax.broadcasted_iota(jnp.int32, (CHUNK, CHUNK), 0)
    col = lax.broadcasted_iota(jnp.int32, (CHUNK, CHUNK), 1)
    causal = (col >= row) if reverse else (col <= row)
    return dict(
        hide=jnp.where(causal, 0.0, -jnp.inf),
        cs=cs_ref[0], src_t=srct_ref[0], out_w_t=outwt_ref[0], e_tot=etot_ref[0, 0:1, :],
        head0=SSD_HEADS if reverse else 0)


def _ssd_group(b_ref, c_ref, g):
    bg = b_ref[0, :, g * D_STATE:(g + 1) * D_STATE]
    cg = c_ref[0, :, g * D_STATE:(g + 1) * D_STATE]
    gram = lax.dot_general(cg, bg, (((1,), (1,)), ((), ())), preferred_element_type=F32)
    return gram, cg.astype(F32), bg.astype(F32).T


def _ssd_pair(dec, grp, x_ref, y_ref, st_ref, pair, low):
    gram, cg_f, bg_t = grp
    lanes = slice(pair * LANES, (pair + 1) * LANES)
    xp = x_ref[0, :, lanes]
    state = st_ref[pair]
    rhs = jnp.concatenate([xp, state.astype(BF16)], axis=0)
    hd0 = dec['head0'] + 2 * pair
    ys, ups = [], []
    for hd in (hd0, hd0 + 1):
        cs_l = jnp.broadcast_to(dec['cs'][:, hd:hd + 1], (CHUNK, CHUNK))
        seg = cs_l - dec['src_t'][hd:hd + 1, :]
        within = gram * jnp.exp2(seg + dec['hide'])
        from_state = cg_f * jnp.exp2(cs_l)
        lhs = jnp.concatenate([within, from_state], axis=1).astype(BF16)
        ys.append(jnp.dot(lhs, rhs, preferred_element_type=F32))
        ups.append(jnp.dot((bg_t * dec['out_w_t'][hd:hd + 1, :]).astype(BF16), xp, preferred_element_type=F32))
    y_ref[0, :, lanes] = jnp.where(low, ys[0], ys[1]).astype(y_ref.dtype)
    e_tot = dec['e_tot']
    keep = jnp.where(low[0:1, :], jnp.broadcast_to(e_tot[:, hd0:hd0 + 1], (1, LANES)),
                     jnp.broadcast_to(e_tot[:, hd0 + 1:hd0 + 2], (1, LANES)))
    st_ref[pair] = state * keep + jnp.where(low, ups[0], ups[1])


def _ssd_scan_kernel(xf_ref, bf_ref, cf_ref, csf_ref, srcf_ref, outf_ref, etotf_ref,
                     xb_ref, bb_ref, cb_ref, csb_ref, srcb_ref, outb_ref, etotb_ref,
                     yf_ref, yb_ref, stf_ref, stb_ref):
    @pl.when(pl.program_id(1) == 0)
    def _():
        stf_ref[...] = jnp.zeros(stf_ref.shape, F32)
        stb_ref[...] = jnp.zeros(stb_ref.shape, F32)

    low = lax.broadcasted_iota(jnp.int32, (CHUNK, LANES), 1) < SSD_HEAD_DIM
    dec_f = _ssd_decays(csf_ref, srcf_ref, outf_ref, etotf_ref, reverse=False)
    dec_b = _ssd_decays(csb_ref, srcb_ref, outb_ref, etotb_ref, reverse=True)
    pairs_per_group = HEADS_PER_GROUP // 2
    for g in range(SSD_GROUPS):
        grp_f = _ssd_group(bf_ref, cf_ref, g)
        grp_b = _ssd_group(bb_ref, cb_ref, g)
        for pr in range(pairs_per_group):
            pair = g * pairs_per_group + pr
            _ssd_pair(dec_f, grp_f, xf_ref, yf_ref, stf_ref, pair, low)
            _ssd_pair(dec_b, grp_b, xb_ref, yb_ref, stb_ref, pair, low)


def _ssd_scan(xbc, cs, src_t, out_w_t, e_tot):
    B, Tp, _ = xbc.shape
    nc = Tp // CHUNK
    bc_w = SSD_GROUPS * D_STATE
    b_blk = D_INNER // bc_w

    def specs(pos):
        return [pl.BlockSpec((1, CHUNK, D_INNER), lambda b, c: (b, pos(c), 0)),
                pl.BlockSpec((1, CHUNK, bc_w), lambda b, c: (b, pos(c), b_blk)),
                pl.BlockSpec((1, CHUNK, bc_w), lambda b, c: (b, pos(c), b_blk + 1)),
                pl.BlockSpec((1, CHUNK, LANES), lambda b, c: (b, pos(c), 0)),
                pl.BlockSpec((1, LANES, CHUNK), lambda b, c: (b, 0, pos(c))),
                pl.BlockSpec((1, LANES, CHUNK), lambda b, c: (b, 0, pos(c))),
                pl.BlockSpec((1, SUBLANES, LANES), lambda b, c: (b, pos(c), 0))]

    fwd = lambda c: c
    bwd = lambda c: nc - 1 - c
    y_shape = jax.ShapeDtypeStruct((B, Tp, D_INNER), BF16)
    state = pltpu.VMEM((SSD_HEADS // 2, D_STATE, LANES), F32)
    return pl.pallas_call(
        _ssd_scan_kernel,
        grid=(B, nc),
        in_specs=specs(fwd) + specs(bwd),
        out_specs=[pl.BlockSpec((1, CHUNK, D_INNER), lambda b, c: (b, fwd(c), 0)),
                   pl.BlockSpec((1, CHUNK, D_INNER), lambda b, c: (b, bwd(c), 0))],
        out_shape=[y_shape, y_shape],
        scratch_shapes=[state, state],
        compiler_params=_params(("parallel", "arbitrary")),
        name="ssd_scan",
    )(*(2 * (xbc, xbc, xbc, cs, src_t, out_w_t, e_tot)))


def _ssd_out_kernel(yf_ref, yb_ref, x_ref, z_ref, d_ref, nw_ref, w_ref, h_ref, o_ref):
    y = yf_ref[...].astype(F32) + yb_ref[...].astype(F32) + d_ref[...] * x_ref[...].astype(F32)
    y = y * _silu(z_ref[...].astype(F32))
    r = lax.rsqrt(jnp.mean(y * y, axis=-1, keepdims=True) + EPS)
    proj = jnp.dot((y * nw_ref[...]).astype(BF16), w_ref[...], preferred_element_type=F32)
    o_ref[...] = h_ref[...] + r * proj


def _ssd_out(yf, yb, xbc, z, d_e, nw, w, h):
    R = h.shape[0]
    tm = _pick_tile(R, (512, 384, 256, 128))
    wide = pl.BlockSpec((tm, D_INNER), lambda i: (i, 0))
    row = pl.BlockSpec((tm, D_MODEL), lambda i: (i, 0))
    return pl.pallas_call(
        _ssd_out_kernel,
        grid=(R // tm,),
        in_specs=[wide, wide, wide, wide] + [_const_spec(a.shape) for a in (d_e, nw, w)] + [row],
        out_specs=row,
        out_shape=jax.ShapeDtypeStruct((R, D_MODEL), F32),
        compiler_params=_params(("parallel",)),
        name="ssd_out",
    )(yf, yb, xbc, z, d_e, nw, w, h)


def _row(v, width=None):
    v = v.astype(F32).reshape(1, -1)
    if width is not None and v.shape[1] < width:
        v = jnp.pad(v, ((0, 0), (0, width - v.shape[1])))
    return v


def _prep_mla(w_in, q_norm, kv_norm, w_q_up, w_kv_up, q_head_norm, k_head_norm, w_out):
    lat = Q_LORA_RANK + KV_LORA_RANK
    rope_cols = jnp.pad(w_in[:, lat:], ((0, 0), (ROPE_LANE0, HEAD_PAD - QK_HEAD_DIM)))
    w_in_p = jnp.concatenate([w_in[:, :lat], rope_cols], axis=1).astype(BF16)
    w_q = w_q_up.reshape(Q_LORA_RANK, MLA_HEADS, QK_HEAD_DIM)
    w_q = jnp.pad(w_q, ((0, 0), (0, 0), (0, HEAD_PAD - QK_HEAD_DIM))).reshape(Q_LORA_RANK, -1).astype(BF16)
    w_kv = w_kv_up.reshape(KV_LORA_RANK, MLA_HEADS, QK_NOPE_DIM + V_HEAD_DIM)
    w_k = jnp.pad(w_kv[:, :, :QK_NOPE_DIM], ((0, 0), (0, 0), (0, HEAD_PAD - QK_NOPE_DIM)))
    w_k = w_k.reshape(KV_LORA_RANK, -1).astype(BF16)
    w_v = w_kv[:, :, QK_NOPE_DIM:].reshape(KV_LORA_RANK, -1).T.astype(BF16)
    scale = LOG2E / math.sqrt(QK_HEAD_DIM)
    gq = q_head_norm.astype(F32)
    gk = k_head_norm.astype(F32)
    bound = 1.02 * scale * QK_HEAD_DIM * jnp.max(jnp.abs(gq)) * jnp.max(jnp.abs(gk))
    return dict(
        w_in=w_in_p, q_norm=_row(q_norm), kv_norm=_row(kv_norm), w_q=w_q, w_k=w_k, w_v=w_v,
        bound=bound.reshape(1),
        gq=_row(gq * scale, HEAD_PAD),
        gkn=_row(gk[:QK_NOPE_DIM], HEAD_PAD),
        gkr=_row(jnp.pad(gk[QK_NOPE_DIM:], (ROPE_LANE0, 0)), HEAD_PAD),
        w_out=w_out.astype(BF16))


def _attention_tables(Tp):
    half = QK_ROPE_DIM // 2
    pos = jnp.arange(Tp, dtype=F32) - FRONT_PAD
    inv_freq = ROPE_BASE ** (-jnp.arange(half, dtype=F32) / half)
    ang = pos[:, None] * inv_freq[None, :]
    cos, sin = jnp.cos(ang), jnp.sin(ang)
    tail = HEAD_PAD - QK_HEAD_DIM
    cos_t = jnp.concatenate([jnp.ones((Tp, ROPE_LANE0), F32), cos, cos, jnp.ones((Tp, tail), F32)], axis=1)
    sin_t = jnp.concatenate([jnp.zeros((Tp, ROPE_LANE0), F32), -sin, sin, jnp.zeros((Tp, tail), F32)], axis=1)
    is_pad = (jnp.arange(Tp) < FRONT_PAD)[:, None]
    lane = jnp.arange(HEAD_PAD)[None, :]
    one = (lane == MASK_LANE).astype(F32)
    return dict(cos=cos_t, sin=sin_t, pad_mask=jnp.where(is_pad, MASK_VALUE, 0.0).astype(F32) * one, one=one)


def _key_bias(tables, bound):
    return jnp.where(tables['pad_mask'] < 0, tables['pad_mask'], -bound * tables['one'])


def _prep_ssd(w_in, conv_w, conv_b, dt_bias, a_log, d_skip, norm_w, w_out):
    w_in_p = jnp.pad(w_in, ((0, 0), (0, LANES - 2 * SSD_HEADS))).astype(BF16)
    return dict(
        w_in=w_in_p, conv_w=conv_w.astype(F32), conv_b=_row(conv_b),
        dt_bias=_row(dt_bias, LANES), a_log=_row(a_log, LANES),
        d_e=_row(jnp.repeat(d_skip.astype(F32), SSD_HEAD_DIM)),
        norm=_row(norm_w), w_out=w_out.astype(BF16))


def kernel(x_prompt, x_sample, meta_tokens, mix_norm, ffn_norm, mla_w_in, mla_q_norm, mla_kv_norm, mla_w_q_up, mla_w_kv_up, mla_q_head_norm, mla_k_head_norm, mla_w_out, ssd_w_in, ssd_conv_w, ssd_conv_b, ssd_dt_bias, ssd_a_log, ssd_d, ssd_norm, ssd_w_out, ffn_w_gate, ffn_w_up, ffn_w_down, final_norm):
    assert x_prompt.shape[1:] == x_sample.shape[1:] and x_prompt.shape[2] == D_MODEL
    assert x_prompt.shape[1] % CHUNK == 0
    depth = mix_norm.shape[0]
    n_prompt = x_prompt.shape[0]
    B = n_prompt + x_sample.shape[0]
    S = x_prompt.shape[1]
    Tp = FRONT_PAD + N_META + S
    head = jnp.concatenate([jnp.zeros((FRONT_PAD, D_MODEL), F32), meta_tokens.astype(F32)], axis=0)
    h = jnp.concatenate([jnp.broadcast_to(head[None], (B, CHUNK, D_MODEL)),
                         jnp.concatenate([x_prompt, x_sample], axis=0)], axis=1)
    tables = _attention_tables(Tp)
    attn_p = (mla_w_in, mla_q_norm, mla_kv_norm, mla_w_q_up, mla_w_kv_up,
              mla_q_head_norm, mla_k_head_norm, mla_w_out)
    ssd_p = (ssd_w_in, ssd_conv_w, ssd_conv_b, ssd_dt_bias, ssd_a_log, ssd_d, ssd_norm, ssd_w_out)
    flat = lambda t: t.reshape(B * Tp, t.shape[-1])
    ia = ib = 0
    for layer in range(depth):
        nw = _row(mix_norm[layer])
        if layer % 2 == 0:
            p = _prep_mla(*[t[ia] for t in attn_p])
            ia += 1
            q, k, v = _mla_proj(h, nw, p, tables)
            o = _flash(p['bound'], q, k, v)
            h2 = flat(_attn_out(o, p['w_out'], h))
        else:
            p = _prep_ssd(*[t[ib] for t in ssd_p])
            ib += 1
            z, xbc, *decays = _ssd_inproj(h, nw, p['w_in'], p['dt_bias'], p['a_log'])
            xbc = _ssd_conv(xbc, p['conv_w'], p['conv_b'])
            yf, yb = _ssd_scan(xbc, *decays)
            h2 = _ssd_out(flat(yf), flat(yb), flat(xbc), flat(z), p['d_e'], p['norm'], p['w_out'], flat(h))
        h2 = _ffn(h2, _row(ffn_norm[layer]), ffn_w_gate[layer].astype(BF16), ffn_w_up[layer].astype(BF16),
                  ffn_w_down[layer].astype(BF16), _row(final_norm), final=(layer == depth - 1))
        h = h2.reshape(B, Tp, D_MODEL)
    y = h[:, CHUNK:]
    return (y[:n_prompt], y[n_prompt:])
```

```python
import functools
import math

import jax
import jax.numpy as jnp
from jax import lax
from jax.experimental import pallas as pl
from jax.experimental.pallas import tpu as pltpu

F32 = jnp.float32
BF16 = jnp.bfloat16

D_MODEL = 1024
N_META = 16
EPS = 1e-6
MLA_HEADS = 16
QK_NOPE_DIM = 64
QK_ROPE_DIM = 32
QK_HEAD_DIM = QK_NOPE_DIM + QK_ROPE_DIM
V_HEAD_DIM = 64
Q_LORA_RANK = 384
KV_LORA_RANK = 256
ROPE_BASE = 10000.0
D_INNER = 2 * D_MODEL
SSD_HEAD_DIM = 64
SSD_HEADS = D_INNER // SSD_HEAD_DIM
SSD_GROUPS = 4
HEADS_PER_GROUP = SSD_HEADS // SSD_GROUPS
D_STATE = 128
D_CONV = 5
CONV_DIM = D_INNER + 2 * SSD_GROUPS * D_STATE
CHUNK = 128
FFN_HIDDEN = -(-8 * D_MODEL // (3 * 256)) * 256

LANES = 128
FRONT_PAD = CHUNK - N_META
HEAD_PAD = LANES
ROPE_LANE0 = QK_NOPE_DIM
MASK_LANE = QK_HEAD_DIM
MASK_VALUE = -1e30
LOG2E = 1.0 / math.log(2.0)
SOFTMAX_BOUND_LIMIT = 40.0
HEAD_GROUP = 4
GROUP_W = HEADS_PER_GROUP * SSD_HEAD_DIM
VMEM_LIMIT = 56 * 1024 * 1024


def _pick_tile(n, candidates):
    for c in candidates:
        if n % c == 0:
            return c
    return n


def _params(sem, vmem=VMEM_LIMIT):
    return pltpu.CompilerParams(dimension_semantics=sem, vmem_limit_bytes=vmem)


def _const_spec(shape):
    nd = len(shape)
    return pl.BlockSpec(shape, lambda *_: (0,) * nd, pipeline_mode=pl.Buffered(1))


def _rms(x, w):
    return x * lax.rsqrt(jnp.mean(x * x, axis=-1, keepdims=True) + EPS) * w


def _silu(x):
    half = 0.5 * x
    return half + half * jnp.tanh(half)


def _rope(x, cos, sin):
    lane = lax.broadcasted_iota(jnp.int32, x.shape, 1)
    half = QK_ROPE_DIM // 2
    first_half = (lane >= ROPE_LANE0) & (lane < ROPE_LANE0 + half)
    swapped = jnp.where(first_half, pltpu.roll(x, HEAD_PAD - half, 1), pltpu.roll(x, half, 1))
    return x * cos + swapped * sin


def _mla_proj_kernel(h_ref, nw_ref, win_ref, qn_ref, kvn_ref, wq_ref, wk_ref, wv_ref,
                     gkn_ref, gkr_ref, cos_ref, sin_ref, bias_ref, q_ref, k_ref, v_ref):
    u = _rms(h_ref[0], nw_ref[...]).astype(BF16)
    c = jnp.dot(u, win_ref[...], preferred_element_type=F32)
    cq = _rms(c[:, :Q_LORA_RANK], qn_ref[...]).astype(BF16)
    ckv = _rms(c[:, Q_LORA_RANK:Q_LORA_RANK + KV_LORA_RANK], kvn_ref[...]).astype(BF16)
    k_rope = c[:, Q_LORA_RANK + KV_LORA_RANK:]
    q_ref[0] = jnp.dot(cq, wq_ref[...], preferred_element_type=F32)
    k_nope = jnp.dot(ckv, wk_ref[...], preferred_element_type=F32)
    v_ref[0] = lax.dot_general(wv_ref[...], ckv, (((1,), (1,)), ((), ())),
                               preferred_element_type=F32).astype(BF16)

    cos = cos_ref[...]
    sin = sin_ref[...]
    inv_d = 1.0 / QK_HEAD_DIM
    kr = _rope(k_rope * gkr_ref[...], cos, sin)
    kr_ss = jnp.sum(k_rope * k_rope, axis=-1, keepdims=True)
    gkn = gkn_ref[...]
    bias = bias_ref[...]
    for hd in range(MLA_HEADS):
        kh = k_nope[:, hd * HEAD_PAD:(hd + 1) * HEAD_PAD]
        rk = lax.rsqrt((jnp.sum(kh * kh, axis=-1, keepdims=True) + kr_ss) * inv_d + EPS)
        k_ref[0, hd] = ((kh * gkn + kr) * rk + bias).astype(BF16)


def _mla_proj(h, nw, p, tables):
    B, Tp, _ = h.shape
    tm = _pick_tile(Tp, (384, 128))
    row = lambda b, i: (b, i, 0)
    tab = lambda b, i: (i, 0)
    wide = MLA_HEADS * HEAD_PAD
    consts = (nw, p['w_in'], p['q_norm'], p['kv_norm'], p['w_q'], p['w_k'], p['w_v'], p['gkn'], p['gkr'])
    return pl.pallas_call(
        _mla_proj_kernel,
        grid=(B, Tp // tm),
        in_specs=[pl.BlockSpec((1, tm, D_MODEL), row)]
        + [_const_spec(a.shape) for a in consts]
        + [pl.BlockSpec((tm, HEAD_PAD), tab)] * 3,
        out_specs=[pl.BlockSpec((1, tm, wide), row),
                   pl.BlockSpec((1, MLA_HEADS, tm, HEAD_PAD), lambda b, i: (b, 0, i, 0)),
                   pl.BlockSpec((1, MLA_HEADS * V_HEAD_DIM, tm), lambda b, i: (b, 0, i))],
        out_shape=[jax.ShapeDtypeStruct((B, Tp, wide), F32),
                   jax.ShapeDtypeStruct((B, MLA_HEADS, Tp, HEAD_PAD), BF16),
                   jax.ShapeDtypeStruct((B, MLA_HEADS * V_HEAD_DIM, Tp), BF16)],
        compiler_params=_params(("parallel", "parallel")),
        name="mla_proj",
    )(h, *consts, tables['cos'], tables['sin'], _key_bias(tables, p['bound']))


def _flash_kernel(bound_ref, q_ref, k_ref, v_ref, cos_ref, sin_ref, gq_ref, one_ref, o_ref, *, key_chunk):
    Tp = k_ref.shape[2]
    tq = q_ref.shape[1]
    kq = lambda k, q: lax.dot_general(k, q, (((1,), (1,)), ((), ())), preferred_element_type=F32)

    chunks = range(0, Tp, key_chunk)

    def queries(j):
        lane0 = j * HEAD_PAD
        qh = q_ref[0, :, pl.ds(lane0 if isinstance(j, int) else pl.multiple_of(lane0, HEAD_PAD), HEAD_PAD)]
        rq = lax.rsqrt(jnp.sum(qh * qh, axis=-1, keepdims=True) * (1.0 / QK_HEAD_DIM) + EPS)
        return (_rope(qh * rq * gq_ref[...], cos_ref[...], sin_ref[...]) + one_ref[...]).astype(BF16)

    def head(j, q, bounded):
        r0 = j * V_HEAD_DIM
        rows = pl.ds(r0 if isinstance(j, int) else pl.multiple_of(r0, V_HEAD_DIM), V_HEAD_DIM)
        scores = lambda c0: kq(k_ref[0, j, c0:c0 + key_chunk, :], q)
        if bounded:
            shift = None
        else:
            shift = jnp.full((1, tq), -jnp.inf, F32)
            for c0 in chunks:
                shift = jnp.maximum(shift, jnp.max(scores(c0), axis=0, keepdims=True))
        l = jnp.zeros((1, tq), F32)
        o = jnp.zeros((V_HEAD_DIM, tq), F32)
        for c0 in chunks:
            s = scores(c0)
            p = jnp.exp2(s if bounded else s - shift)
            l = l + jnp.sum(p, axis=0, keepdims=True)
            o = o + jnp.dot(v_ref[0, rows, c0:c0 + key_chunk], p.astype(BF16), preferred_element_type=F32)
        o_ref[0, rows, :] = (o * (1.0 / l)).astype(BF16)

    bounded = bound_ref[0] <= SOFTMAX_BOUND_LIMIT

    @pl.when(bounded)
    def _():
        qs = [queries(j) for j in range(HEAD_GROUP)]
        for j in range(HEAD_GROUP):
            head(j, qs[j], True)

    @pl.when(jnp.logical_not(bounded))
    def _():
        def body(j, carry):
            head(j, queries(j), False)
            return carry
        lax.fori_loop(0, HEAD_GROUP, body, 0)


def _flash(bound, q_raw, k, v, p, tables):
    B, Tp, _ = q_raw.shape
    tq = _pick_tile(Tp, (1408, 384, 128))
    gw = HEAD_GROUP * V_HEAD_DIM
    tab = pl.BlockSpec((tq, HEAD_PAD), lambda b, g, i: (i, 0))
    return pl.pallas_call(
        functools.partial(_flash_kernel, key_chunk=_pick_tile(Tp, (1408, 384, 128))),
        grid=(B, MLA_HEADS // HEAD_GROUP, Tp // tq),
        in_specs=[pl.BlockSpec(memory_space=pltpu.SMEM),
                  pl.BlockSpec((1, tq, HEAD_GROUP * HEAD_PAD), lambda b, g, i: (b, i, g)),
                  pl.BlockSpec((1, HEAD_GROUP, Tp, HEAD_PAD), lambda b, g, i: (b, g, 0, 0)),
                  pl.BlockSpec((1, gw, Tp), lambda b, g, i: (b, g, 0)),
                  tab, tab, _const_spec(p['gq'].shape), _const_spec(tables['one'].shape)],
        out_specs=pl.BlockSpec((1, gw, tq), lambda b, g, i: (b, g, i)),
        out_shape=jax.ShapeDtypeStruct((B, MLA_HEADS * V_HEAD_DIM, Tp), BF16),
        compiler_params=_params(("parallel", "parallel", "parallel")),
        name="flash",
    )(bound, q_raw, k, v, tables['cos'], tables['sin'], p['gq'], tables['one'])


def _attn_out_kernel(ot_ref, w_ref, h_ref, o_ref):
    proj = lax.dot_general(ot_ref[0], w_ref[...], (((0,), (0,)), ((), ())), preferred_element_type=F32)
    o_ref[0] = h_ref[0] + proj


def _attn_out(o_t, w, h):
    B, Tp, _ = h.shape
    tm = _pick_tile(Tp, (1408, 384, 128))
    row = pl.BlockSpec((1, tm, D_MODEL), lambda b, i: (b, i, 0))
    return pl.pallas_call(
        _attn_out_kernel,
        grid=(B, Tp // tm),
        in_specs=[pl.BlockSpec((1, o_t.shape[1], tm), lambda b, i: (b, 0, i)), _const_spec(w.shape), row],
        out_specs=row,
        out_shape=jax.ShapeDtypeStruct((B, Tp, D_MODEL), F32),
        compiler_params=_params(("parallel", "parallel")),
        name="attn_out",
    )(o_t, w, h)


def _hidden_chunks():
    bounds, start = [], 0
    while start < FFN_HIDDEN:
        stop = min(start + 1536, FFN_HIDDEN)
        bounds.append((start, stop))
        start = stop
    return bounds


def _ffn_kernel(h_ref, nw_ref, wg_ref, wu_ref, wd_ref, fw_ref, o_ref, *, final):
    h = h_ref[...]
    u = _rms(h, nw_ref[...]).astype(BF16)
    acc = h
    for lo, hi in _hidden_chunks():
        g = jnp.dot(u, wg_ref[:, lo:hi], preferred_element_type=F32)
        up = jnp.dot(u, wu_ref[:, lo:hi], preferred_element_type=F32)
        a = (_silu(g) * up).astype(BF16)
        acc = acc + jnp.dot(a, wd_ref[lo:hi, :], preferred_element_type=F32)
    if final:
        acc = _rms(acc, fw_ref[...])
    o_ref[...] = acc


def _ffn(h, nw, wg, wu, wd, fw, final):
    R = h.shape[0]
    tm = _pick_tile(R, (512, 384, 256, 128))
    row = pl.BlockSpec((tm, D_MODEL), lambda i: (i, 0))
    return pl.pallas_call(
        functools.partial(_ffn_kernel, final=final),
        grid=(R // tm,),
        in_specs=[row] + [_const_spec(a.shape) for a in (nw, wg, wu, wd, fw)],
        out_specs=row,
        out_shape=jax.ShapeDtypeStruct((R, D_MODEL), F32),
        compiler_params=_params(("parallel",)),
        name="ffn",
    )(h, nw, wg, wu, wd, fw)


SUBLANES = 8


def _ssd_inproj_kernel(h_ref, nw_ref, w_ref, dtb_ref, alog_ref, z_ref, xbc_ref, cs_ref, srct_ref, outwt_ref,
                       etot_ref):
    tm = h_ref.shape[1]
    u = _rms(h_ref[0], nw_ref[...]).astype(BF16)
    y = jnp.dot(u, w_ref[...], preferred_element_type=F32)
    row = pl.program_id(1) * tm + lax.broadcasted_iota(jnp.int32, (tm, 1), 0)
    valid = row >= FRONT_PAD
    z_ref[0] = y[:, :D_INNER].astype(BF16)
    xbc_ref[0] = jnp.where(valid, y[:, D_INNER:D_INNER + CONV_DIM], 0.0).astype(BF16)
    dt_raw = y[:, D_INNER + CONV_DIM:] + dtb_ref[...]
    dt_all = jnp.where(valid, jnp.maximum(dt_raw, 0.0) + jnp.log(1.0 + jnp.exp(-jnp.abs(dt_raw))), 0.0)

    neg_a = -jnp.exp(alog_ref[...]) * LOG2E
    forward = lax.broadcasted_iota(jnp.int32, (1, LANES), 1) < SSD_HEADS
    r = lax.broadcasted_iota(jnp.int32, (CHUNK, CHUNK), 0)
    c = lax.broadcasted_iota(jnp.int32, (CHUNK, CHUNK), 1)
    tri_f = (c <= r).astype(BF16)
    tri_b = (c >= r).astype(BF16)
    for ch in range(tm // CHUNK):
        rows = slice(ch * CHUNK, (ch + 1) * CHUNK)
        dt = dt_all[rows]
        rest = dt * neg_a
        cs_f = jnp.zeros((CHUNK, LANES), F32)
        cs_b = jnp.zeros((CHUNK, LANES), F32)
        for _ in range(3):
            part = rest.astype(BF16)
            cs_f = cs_f + jnp.dot(tri_f, part, preferred_element_type=F32)
            cs_b = cs_b + jnp.dot(tri_b, part, preferred_element_type=F32)
            rest = rest - part.astype(F32)
        cs = jnp.where(forward, cs_f, cs_b)
        total = jnp.where(forward, cs_f[CHUNK - 1:CHUNK, :], cs_b[0:1, :])
        cs_ref[0, rows, :] = cs
        srct_ref[0, :, rows] = (cs - jnp.log2(dt)).T
        outwt_ref[0, :, rows] = (jnp.exp2(total - cs) * dt).T
        etot_ref[0, ch * SUBLANES:(ch + 1) * SUBLANES, :] = jnp.broadcast_to(jnp.exp2(total), (SUBLANES, LANES))


def _ssd_inproj(h, nw, w, dtb, a_log):
    B, Tp, _ = h.shape
    tm = _pick_tile(Tp, (384, 128))
    row = lambda b, i: (b, i, 0)
    col = lambda b, i: (b, 0, i)
    per_chunk = SUBLANES * tm // CHUNK
    return pl.pallas_call(
        _ssd_inproj_kernel,
        grid=(B, Tp // tm),
        in_specs=[pl.BlockSpec((1, tm, D_MODEL), row)] + [_const_spec(a.shape) for a in (nw, w, dtb, a_log)],
        out_specs=[pl.BlockSpec((1, tm, D_INNER), row), pl.BlockSpec((1, tm, CONV_DIM), row),
                   pl.BlockSpec((1, tm, LANES), row), pl.BlockSpec((1, LANES, tm), col),
                   pl.BlockSpec((1, LANES, tm), col), pl.BlockSpec((1, per_chunk, LANES), row)],
        out_shape=[jax.ShapeDtypeStruct((B, Tp, D_INNER), BF16),
                   jax.ShapeDtypeStruct((B, Tp, CONV_DIM), BF16),
                   jax.ShapeDtypeStruct((B, Tp, LANES), F32),
                   jax.ShapeDtypeStruct((B, LANES, Tp), F32),
                   jax.ShapeDtypeStruct((B, LANES, Tp), F32),
                   jax.ShapeDtypeStruct((B, SUBLANES * Tp // CHUNK, LANES), F32)],
        compiler_params=_params(("parallel", "parallel")),
        name="ssd_inproj",
    )(h, nw, w, dtb, a_log)


CONV_MARGIN = 16
CONV_ROWS = 128
CONV_SHIFTS = tuple(t - D_CONV // 2 for t in range(D_CONV) if t != D_CONV // 2)


def _conv_kernel(x_ref, w_ref, b_ref, o_ref, xs_ref):
    Tp, cw = x_ref.shape[1], x_ref.shape[2]
    win = CONV_ROWS + 2 * CONV_MARGIN
    zeros = jnp.zeros((CONV_MARGIN, cw), BF16)
    xs_ref[0:CONV_MARGIN, :] = zeros
    xs_ref[CONV_MARGIN + Tp:, :] = zeros
    xs_ref[CONV_MARGIN:CONV_MARGIN + Tp, :] = x_ref[0]
    r = lax.broadcasted_iota(jnp.int32, (len(CONV_SHIFTS) * CONV_ROWS, win), 0)
    j = lax.broadcasted_iota(jnp.int32, (len(CONV_SHIFTS) * CONV_ROWS, win), 1)
    select = jnp.zeros(r.shape, F32)
    for k, shift in enumerate(CONV_SHIFTS):
        hit = (r >= k * CONV_ROWS) & (r < (k + 1) * CONV_ROWS) & (j == r - k * CONV_ROWS + CONV_MARGIN + shift)
        select = jnp.where(hit, 1.0, select)
    select = select.astype(BF16)
    w = [jnp.broadcast_to(w_ref[t:t + 1, :], (CONV_ROWS, cw)) for t in range(D_CONV)]
    b = jnp.broadcast_to(b_ref[...], (CONV_ROWS, cw))
    centre = D_CONV // 2

    def body(i, carry):
        r0 = pl.multiple_of(i * CONV_ROWS, CONV_ROWS)
        shifted = jnp.dot(select, xs_ref[pl.ds(r0, win), :], preferred_element_type=F32)
        acc = b + w[centre] * xs_ref[pl.ds(r0 + CONV_MARGIN, CONV_ROWS), :].astype(F32)
        for k, shift in enumerate(CONV_SHIFTS):
            acc = acc + w[centre + shift] * shifted[k * CONV_ROWS:(k + 1) * CONV_ROWS, :]
        o_ref[0, pl.ds(r0, CONV_ROWS), :] = _silu(acc).astype(BF16)
        return carry

    n_blocks = Tp // CONV_ROWS
    lax.fori_loop(0, n_blocks, body, 0, unroll=3 if n_blocks % 3 == 0 else 1)


def _ssd_conv(xbc, w, b):
    B, Tp, C = xbc.shape
    cw = 512
    blk = pl.BlockSpec((1, Tp, cw), lambda bi, ci: (bi, 0, ci))
    return pl.pallas_call(
        _conv_kernel,
        grid=(B, C // cw),
        in_specs=[blk, pl.BlockSpec((D_CONV, cw), lambda bi, ci: (0, ci)),
                  pl.BlockSpec((1, cw), lambda bi, ci: (0, ci))],
        out_specs=blk,
        out_shape=jax.ShapeDtypeStruct((B, Tp, C), BF16),
        scratch_shapes=[pltpu.VMEM((Tp + 2 * CONV_MARGIN, cw), BF16)],
        compiler_params=_params(("parallel", "parallel")),
        name="ssd_conv",
    )(xbc, w, b)


def _ssd_decays(cs_ref, srct_ref, outwt_ref, etot_ref, reverse):
    row = lax.broadcasted_iota(jnp.int32, (CHUNK, CHUNK), 0)
    col = lax.broadcasted_iota(jnp.int32, (CHUNK, CHUNK), 1)
    causal = (col >= row) if reverse else (col <= row)
    return dict(
        hide=jnp.where(causal, 0.0, -jnp.inf),
        cs=cs_ref[0], src_t=srct_ref[0], out_w_t=outwt_ref[0], e_tot=etot_ref[0, 0:1, :],
        head0=SSD_HEADS if reverse else 0)


def _ssd_group(b_ref, c_ref, g):
    bg = b_ref[0, :, g * D_STATE:(g + 1) * D_STATE]
    cg = c_ref[0, :, g * D_STATE:(g + 1) * D_STATE]
    gram = lax.dot_general(cg, bg, (((1,), (1,)), ((), ())), preferred_element_type=F32)
    return gram, cg.astype(F32), bg.astype(F32).T


def _ssd_pair(dec, grp, x_ref, y_ref, st_ref, pair, low):
    gram, cg_f, bg_t = grp
    lanes = slice(pair * LANES, (pair + 1) * LANES)
    xp = x_ref[0, :, lanes]
    state = st_ref[pair]
    rhs = jnp.concatenate([xp, state.astype(BF16)], axis=0)
    hd0 = dec['head0'] + 2 * pair
    ys, ups = [], []
    for hd in (hd0, hd0 + 1):
        cs_l = jnp.broadcast_to(dec['cs'][:, hd:hd + 1], (CHUNK, CHUNK))
        seg = cs_l - dec['src_t'][hd:hd + 1, :]
        within = gram * jnp.exp2(seg + dec['hide'])
        from_state = cg_f * jnp.exp2(cs_l)
        lhs = jnp.concatenate([within, from_state], axis=1).astype(BF16)
        ys.append(jnp.dot(lhs, rhs, preferred_element_type=F32))
        ups.append(jnp.dot((bg_t * dec['out_w_t'][hd:hd + 1, :]).astype(BF16), xp, preferred_element_type=F32))
    y_ref[0, :, lanes] = jnp.where(low, ys[0], ys[1]).astype(y_ref.dtype)
    e_tot = dec['e_tot']
    keep = jnp.where(low[0:1, :], jnp.broadcast_to(e_tot[:, hd0:hd0 + 1], (1, LANES)),
                     jnp.broadcast_to(e_tot[:, hd0 + 1:hd0 + 2], (1, LANES)))
    st_ref[pair] = state * keep + jnp.where(low, ups[0], ups[1])


def _ssd_scan_kernel(xf_ref, bf_ref, cf_ref, csf_ref, srcf_ref, outf_ref, etotf_ref,
                     xb_ref, bb_ref, cb_ref, csb_ref, srcb_ref, outb_ref, etotb_ref,
                     yf_ref, yb_ref, stf_ref, stb_ref):
    @pl.when(pl.program_id(1) == 0)
    def _():
        stf_ref[...] = jnp.zeros(stf_ref.shape, F32)
        stb_ref[...] = jnp.zeros(stb_ref.shape, F32)

    low = lax.broadcasted_iota(jnp.int32, (CHUNK, LANES), 1) < SSD_HEAD_DIM
    dec_f = _ssd_decays(csf_ref, srcf_ref, outf_ref, etotf_ref, reverse=False)
    dec_b = _ssd_decays(csb_ref, srcb_ref, outb_ref, etotb_ref, reverse=True)
    pairs_per_group = HEADS_PER_GROUP // 2
    for g in range(SSD_GROUPS):
        grp_f = _ssd_group(bf_ref, cf_ref, g)
        grp_b = _ssd_group(bb_ref, cb_ref, g)
        for pr in range(pairs_per_group):
            pair = g * pairs_per_group + pr
            _ssd_pair(dec_f, grp_f, xf_ref, yf_ref, stf_ref, pair, low)
            _ssd_pair(dec_b, grp_b, xb_ref, yb_ref, stb_ref, pair, low)


def _ssd_scan(xbc, cs, src_t, out_w_t, e_tot):
    B, Tp, _ = xbc.shape
    nc = Tp // CHUNK
    bc_w = SSD_GROUPS * D_STATE
    b_blk = D_INNER // bc_w

    def specs(pos):
        return [pl.BlockSpec((1, CHUNK, D_INNER), lambda b, c: (b, pos(c), 0)),
                pl.BlockSpec((1, CHUNK, bc_w), lambda b, c: (b, pos(c), b_blk)),
                pl.BlockSpec((1, CHUNK, bc_w), lambda b, c: (b, pos(c), b_blk + 1)),
                pl.BlockSpec((1, CHUNK, LANES), lambda b, c: (b, pos(c), 0)),
                pl.BlockSpec((1, LANES, CHUNK), lambda b, c: (b, 0, pos(c))),
                pl.BlockSpec((1, LANES, CHUNK), lambda b, c: (b, 0, pos(c))),
                pl.BlockSpec((1, SUBLANES, LANES), lambda b, c: (b, pos(c), 0))]

    fwd = lambda c: c
    bwd = lambda c: nc - 1 - c
    y_shape = jax.ShapeDtypeStruct((B, Tp, D_INNER), BF16)
    state = pltpu.VMEM((SSD_HEADS // 2, D_STATE, LANES), F32)
    return pl.pallas_call(
        _ssd_scan_kernel,
        grid=(B, nc),
        in_specs=specs(fwd) + specs(bwd),
        out_specs=[pl.BlockSpec((1, CHUNK, D_INNER), lambda b, c: (b, fwd(c), 0)),
                   pl.BlockSpec((1, CHUNK, D_INNER), lambda b, c: (b, bwd(c), 0))],
        out_shape=[y_shape, y_shape],
        scratch_shapes=[state, state],
        compiler_params=_params(("parallel", "arbitrary")),
        name="ssd_scan",
    )(*(2 * (xbc, xbc, xbc, cs, src_t, out_w_t, e_tot)))


def _ssd_out_kernel(yf_ref, yb_ref, x_ref, z_ref, d_ref, nw_ref, w_ref, h_ref, o_ref):
    y = yf_ref[...].astype(F32) + yb_ref[...].astype(F32) + d_ref[...] * x_ref[...].astype(F32)
    y = y * _silu(z_ref[...].astype(F32))
    r = lax.rsqrt(jnp.mean(y * y, axis=-1, keepdims=True) + EPS)
    proj = jnp.dot((y * nw_ref[...]).astype(BF16), w_ref[...], preferred_element_type=F32)
    o_ref[...] = h_ref[...] + r * proj


def _ssd_out(yf, yb, xbc, z, d_e, nw, w, h):
    R = h.shape[0]
    tm = _pick_tile(R, (512, 384, 256, 128))
    wide = pl.BlockSpec((tm, D_INNER), lambda i: (i, 0))
    row = pl.BlockSpec((tm, D_MODEL), lambda i: (i, 0))
    return pl.pallas_call(
        _ssd_out_kernel,
        grid=(R // tm,),
        in_specs=[wide, wide, wide, wide] + [_const_spec(a.shape) for a in (d_e, nw, w)] + [row],
        out_specs=row,
        out_shape=jax.ShapeDtypeStruct((R, D_MODEL), F32),
        compiler_params=_params(("parallel",)),
        name="ssd_out",
    )(yf, yb, xbc, z, d_e, nw, w, h)


def _row(v, width=None):
    v = v.astype(F32).reshape(1, -1)
    if width is not None and v.shape[1] < width:
        v = jnp.pad(v, ((0, 0), (0, width - v.shape[1])))
    return v


def _prep_mla(w_in, q_norm, kv_norm, w_q_up, w_kv_up, q_head_norm, k_head_norm, w_out):
    lat = Q_LORA_RANK + KV_LORA_RANK
    rope_cols = jnp.pad(w_in[:, lat:], ((0, 0), (ROPE_LANE0, HEAD_PAD - QK_HEAD_DIM)))
    w_in_p = jnp.concatenate([w_in[:, :lat], rope_cols], axis=1).astype(BF16)
    w_q = w_q_up.reshape(Q_LORA_RANK, MLA_HEADS, QK_HEAD_DIM)
    w_q = jnp.pad(w_q, ((0, 0), (0, 0), (0, HEAD_PAD - QK_HEAD_DIM))).reshape(Q_LORA_RANK, -1).astype(BF16)
    w_kv = w_kv_up.reshape(KV_LORA_RANK, MLA_HEADS, QK_NOPE_DIM + V_HEAD_DIM)
    w_k = jnp.pad(w_kv[:, :, :QK_NOPE_DIM], ((0, 0), (0, 0), (0, HEAD_PAD - QK_NOPE_DIM)))
    w_k = w_k.reshape(KV_LORA_RANK, -1).astype(BF16)
    w_v = w_kv[:, :, QK_NOPE_DIM:].reshape(KV_LORA_RANK, -1).T.astype(BF16)
    scale = LOG2E / math.sqrt(QK_HEAD_DIM)
    gq = q_head_norm.astype(F32)
    gk = k_head_norm.astype(F32)
    bound = 1.02 * scale * QK_HEAD_DIM * jnp.max(jnp.abs(gq)) * jnp.max(jnp.abs(gk))
    return dict(
        w_in=w_in_p, q_norm=_row(q_norm), kv_norm=_row(kv_norm), w_q=w_q, w_k=w_k, w_v=w_v,
        bound=bound.reshape(1),
        gq=_row(gq * scale, HEAD_PAD),
        gkn=_row(gk[:QK_NOPE_DIM], HEAD_PAD),
        gkr=_row(jnp.pad(gk[QK_NOPE_DIM:], (ROPE_LANE0, 0)), HEAD_PAD),
        w_out=w_out.astype(BF16))


def _attention_tables(Tp):
    half = QK_ROPE_DIM // 2
    pos = jnp.arange(Tp, dtype=F32) - FRONT_PAD
    inv_freq = ROPE_BASE ** (-jnp.arange(half, dtype=F32) / half)
    ang = pos[:, None] * inv_freq[None, :]
    cos, sin = jnp.cos(ang), jnp.sin(ang)
    tail = HEAD_PAD - QK_HEAD_DIM
    cos_t = jnp.concatenate([jnp.ones((Tp, ROPE_LANE0), F32), cos, cos, jnp.ones((Tp, tail), F32)], axis=1)
    sin_t = jnp.concatenate([jnp.zeros((Tp, ROPE_LANE0), F32), -sin, sin, jnp.zeros((Tp, tail), F32)], axis=1)
    is_pad = (jnp.arange(Tp) < FRONT_PAD)[:, None]
    lane = jnp.arange(HEAD_PAD)[None, :]
    one = (lane == MASK_LANE).astype(F32)
    return dict(cos=cos_t, sin=sin_t, pad_mask=jnp.where(is_pad, MASK_VALUE, 0.0).astype(F32) * one, one=one)


def _key_bias(tables, bound):
    return jnp.where(tables['pad_mask'] < 0, tables['pad_mask'], -bound * tables['one'])


def _prep_ssd(w_in, conv_w, conv_b, dt_bias, a_log, d_skip, norm_w, w_out):
    w_in_p = jnp.pad(w_in, ((0, 0), (0, LANES - 2 * SSD_HEADS))).astype(BF16)
    return dict(
        w_in=w_in_p, conv_w=conv_w.astype(F32), conv_b=_row(conv_b),
        dt_bias=_row(dt_bias, LANES), a_log=_row(a_log, LANES),
        d_e=_row(jnp.repeat(d_skip.astype(F32), SSD_HEAD_DIM)),
        norm=_row(norm_w), w_out=w_out.astype(BF16))


def kernel(x_prompt, x_sample, meta_tokens, mix_norm, ffn_norm, mla_w_in, mla_q_norm, mla_kv_norm, mla_w_q_up, mla_w_kv_up, mla_q_head_norm, mla_k_head_norm, mla_w_out, ssd_w_in, ssd_conv_w, ssd_conv_b, ssd_dt_bias, ssd_a_log, ssd_d, ssd_norm, ssd_w_out, ffn_w_gate, ffn_w_up, ffn_w_down, final_norm):
    assert x_prompt.shape[1:] == x_sample.shape[1:] and x_prompt.shape[2] == D_MODEL
    assert x_prompt.shape[1] % CHUNK == 0
    depth = mix_norm.shape[0]
    n_prompt = x_prompt.shape[0]
    B = n_prompt + x_sample.shape[0]
    S = x_prompt.shape[1]
    Tp = FRONT_PAD + N_META + S
    head = jnp.concatenate([jnp.zeros((FRONT_PAD, D_MODEL), F32), meta_tokens.astype(F32)], axis=0)
    h = jnp.concatenate([jnp.broadcast_to(head[None], (B, CHUNK, D_MODEL)),
                         jnp.concatenate([x_prompt, x_sample], axis=0)], axis=1)
    tables = _attention_tables(Tp)
    attn_p = (mla_w_in, mla_q_norm, mla_kv_norm, mla_w_q_up, mla_w_kv_up,
              mla_q_head_norm, mla_k_head_norm, mla_w_out)
    ssd_p = (ssd_w_in, ssd_conv_w, ssd_conv_b, ssd_dt_bias, ssd_a_log, ssd_d, ssd_norm, ssd_w_out)
    flat = lambda t: t.reshape(B * Tp, t.shape[-1])
    ia = ib = 0
    for layer in range(depth):
        nw = _row(mix_norm[layer])
        if layer % 2 == 0:
            p = _prep_mla(*[t[ia] for t in attn_p])
            ia += 1
            q, k, v = _mla_proj(h, nw, p, tables)
            o = _flash(p['bound'], q, k, v, p, tables)
            h2 = flat(_attn_out(o, p['w_out'], h))
        else:
            p = _prep_ssd(*[t[ib] for t in ssd_p])
            ib += 1
            z, xbc, *decays = _ssd_inproj(h, nw, p['w_in'], p['dt_bias'], p['a_log'])
            xbc = _ssd_conv(xbc, p['conv_w'], p['conv_b'])
            yf, yb = _ssd_scan(xbc, *decays)
            h2 = _ssd_out(flat(yf), flat(yb), flat(xbc), flat(z), p['d_e'], p['norm'], p['w_out'], flat(h))
        h2 = _ffn(h2, _row(ffn_norm[layer]), ffn_w_gate[layer].astype(BF16), ffn_w_up[layer].astype(BF16),
                  ffn_w_down[layer].astype(BF16), _row(final_norm), final=(layer == depth - 1))
        h = h2.reshape(B, Tp, D_MODEL)
    y = h[:, CHUNK:]
    return (y[:n_prompt], y[n_prompt:])
```

```python
import functools
import math

import jax
import jax.numpy as jnp
from jax import lax
from jax.experimental import pallas as pl
from jax.experimental.pallas import tpu as pltpu

F32 = jnp.float32
BF16 = jnp.bfloat16

D_MODEL = 1024
N_META = 16
EPS = 1e-6
MLA_HEADS = 16
QK_NOPE_DIM = 64
QK_ROPE_DIM = 32
QK_HEAD_DIM = QK_NOPE_DIM + QK_ROPE_DIM
V_HEAD_DIM = 64
Q_LORA_RANK = 384
KV_LORA_RANK = 256
ROPE_BASE = 10000.0
D_INNER = 2 * D_MODEL
SSD_HEAD_DIM = 64
SSD_HEADS = D_INNER // SSD_HEAD_DIM
SSD_GROUPS = 4
HEADS_PER_GROUP = SSD_HEADS // SSD_GROUPS
D_STATE = 128
D_CONV = 5
CONV_DIM = D_INNER + 2 * SSD_GROUPS * D_STATE
CHUNK = 128
FFN_HIDDEN = -(-8 * D_MODEL // (3 * 256)) * 256

LANES = 128
FRONT_PAD = CHUNK - N_META
HEAD_PAD = LANES
ROPE_LANE0 = QK_NOPE_DIM
MASK_LANE = QK_HEAD_DIM
MASK_VALUE = -1e30
LOG2E = 1.0 / math.log(2.0)
SOFTMAX_BOUND_LIMIT = 40.0
HEAD_GROUP = 4
GROUP_W = HEADS_PER_GROUP * SSD_HEAD_DIM
VMEM_LIMIT = 56 * 1024 * 1024


def _pick_tile(n, candidates):
    for c in candidates:
        if n % c == 0:
            return c
    return n


def _params(sem, vmem=VMEM_LIMIT):
    return pltpu.CompilerParams(dimension_semantics=sem, vmem_limit_bytes=vmem)


def _const_spec(shape):
    nd = len(shape)
    return pl.BlockSpec(shape, lambda *_: (0,) * nd, pipeline_mode=pl.Buffered(1))


def _rms(x, w):
    return x * lax.rsqrt(jnp.mean(x * x, axis=-1, keepdims=True) + EPS) * w


def _silu(x):
    half = 0.5 * x
    return half + half * jnp.tanh(half)


def _rope(x, cos, sin):
    lane = lax.broadcasted_iota(jnp.int32, x.shape, 1)
    half = QK_ROPE_DIM // 2
    first_half = (lane >= ROPE_LANE0) & (lane < ROPE_LANE0 + half)
    swapped = jnp.where(first_half, pltpu.roll(x, HEAD_PAD - half, 1), pltpu.roll(x, half, 1))
    return x * cos + swapped * sin


def _mla_proj_kernel(h_ref, nw_ref, win_ref, qn_ref, kvn_ref, wq_ref, wk_ref, wv_ref,
                     gkn_ref, gkr_ref, cos_ref, sin_ref, bias_ref, q_ref, k_ref, v_ref):
    u = _rms(h_ref[0], nw_ref[...]).astype(BF16)
    c = jnp.dot(u, win_ref[...], preferred_element_type=F32)
    cq = _rms(c[:, :Q_LORA_RANK], qn_ref[...]).astype(BF16)
    ckv = _rms(c[:, Q_LORA_RANK:Q_LORA_RANK + KV_LORA_RANK], kvn_ref[...]).astype(BF16)
    k_rope = c[:, Q_LORA_RANK + KV_LORA_RANK:]
    q_ref[0] = jnp.dot(cq, wq_ref[...], preferred_element_type=F32)
    k_nope = jnp.dot(ckv, wk_ref[...], preferred_element_type=F32)
    v_ref[0] = lax.dot_general(wv_ref[...], ckv, (((1,), (1,)), ((), ())),
                               preferred_element_type=F32).astype(BF16)

    cos = cos_ref[...]
    sin = sin_ref[...]
    inv_d = 1.0 / QK_HEAD_DIM
    kr = _rope(k_rope * gkr_ref[...], cos, sin)
    kr_ss = jnp.sum(k_rope * k_rope, axis=-1, keepdims=True)
    gkn = gkn_ref[...]
    bias = bias_ref[...]
    for hd in range(MLA_HEADS):
        kh = k_nope[:, hd * HEAD_PAD:(hd + 1) * HEAD_PAD]
        rk = lax.rsqrt((jnp.sum(kh * kh, axis=-1, keepdims=True) + kr_ss) * inv_d + EPS)
        k_ref[0, hd] = ((kh * gkn + kr) * rk + bias).astype(BF16)


def _mla_proj(h, nw, p, tables):
    B, Tp, _ = h.shape
    tm = _pick_tile(Tp, (384, 128))
    row = lambda b, i: (b, i, 0)
    tab = lambda b, i: (i, 0)
    wide = MLA_HEADS * HEAD_PAD
    consts = (nw, p['w_in'], p['q_norm'], p['kv_norm'], p['w_q'], p['w_k'], p['w_v'], p['gkn'], p['gkr'])
    return pl.pallas_call(
        _mla_proj_kernel,
        grid=(B, Tp // tm),
        in_specs=[pl.BlockSpec((1, tm, D_MODEL), row)]
        + [_const_spec(a.shape) for a in consts]
        + [pl.BlockSpec((tm, HEAD_PAD), tab)] * 3,
        out_specs=[pl.BlockSpec((1, tm, wide), row),
                   pl.BlockSpec((1, MLA_HEADS, tm, HEAD_PAD), lambda b, i: (b, 0, i, 0)),
                   pl.BlockSpec((1, MLA_HEADS * V_HEAD_DIM, tm), lambda b, i: (b, 0, i))],
        out_shape=[jax.ShapeDtypeStruct((B, Tp, wide), F32),
                   jax.ShapeDtypeStruct((B, MLA_HEADS, Tp, HEAD_PAD), BF16),
                   jax.ShapeDtypeStruct((B, MLA_HEADS * V_HEAD_DIM, Tp), BF16)],
        compiler_params=_params(("parallel", "parallel")),
        name="mla_proj",
    )(h, *consts, tables['cos'], tables['sin'], _key_bias(tables, p['bound']))


def _flash_kernel(bound_ref, q_ref, k_ref, v_ref, cos_ref, sin_ref, gq_ref, one_ref, o_ref, *, key_chunk):
    Tp = k_ref.shape[2]
    tq = q_ref.shape[1]
    kq = lambda k, q: lax.dot_general(k, q, (((1,), (1,)), ((), ())), preferred_element_type=F32)

    chunks = range(0, Tp, key_chunk)

    def queries(j):
        lane0 = j * HEAD_PAD
        qh = q_ref[0, :, pl.ds(lane0 if isinstance(j, int) else pl.multiple_of(lane0, HEAD_PAD), HEAD_PAD)]
        rq = lax.rsqrt(jnp.sum(qh * qh, axis=-1, keepdims=True) * (1.0 / QK_HEAD_DIM) + EPS)
        return (_rope(qh * rq * gq_ref[...], cos_ref[...], sin_ref[...]) + one_ref[...]).astype(BF16)

    def head(j, q, bounded):
        r0 = j * V_HEAD_DIM
        rows = pl.ds(r0 if isinstance(j, int) else pl.multiple_of(r0, V_HEAD_DIM), V_HEAD_DIM)
        scores = lambda c0: kq(k_ref[0, j, c0:c0 + key_chunk, :], q)
        if bounded:
            shift = None
        else:
            shift = jnp.full((1, tq), -jnp.inf, F32)
            for c0 in chunks:
                shift = jnp.maximum(shift, jnp.max(scores(c0), axis=0, keepdims=True))
        l = jnp.zeros((1, tq), F32)
        o = jnp.zeros((V_HEAD_DIM, tq), F32)
        for c0 in chunks:
            s = scores(c0)
            p = jnp.exp2(s if bounded else s - shift)
            l = l + jnp.sum(p, axis=0, keepdims=True)
            o = o + jnp.dot(v_ref[0, rows, c0:c0 + key_chunk], p.astype(BF16), preferred_element_type=F32)
        o_ref[0, rows, :] = (o * (1.0 / l)).astype(BF16)

    bounded = bound_ref[0] <= SOFTMAX_BOUND_LIMIT

    @pl.when(bounded)
    def _():
        qs = [queries(j) for j in range(HEAD_GROUP)]
        for j in range(HEAD_GROUP):
            head(j, qs[j], True)

    @pl.when(jnp.logical_not(bounded))
    def _():
        def body(j, carry):
            head(j, queries(j), False)
            return carry
        lax.fori_loop(0, HEAD_GROUP, body, 0)


def _flash(bound, q_raw, k, v, p, tables):
    B, Tp, _ = q_raw.shape
    tq = _pick_tile(Tp, (1408, 384, 128))
    gw = HEAD_GROUP * V_HEAD_DIM
    tab = pl.BlockSpec((tq, HEAD_PAD), lambda b, g, i: (i, 0))
    return pl.pallas_call(
        functools.partial(_flash_kernel, key_chunk=_pick_tile(Tp, (1408, 384, 128))),
        grid=(B, MLA_HEADS // HEAD_GROUP, Tp // tq),
        in_specs=[pl.BlockSpec(memory_space=pltpu.SMEM),
                  pl.BlockSpec((1, tq, HEAD_GROUP * HEAD_PAD), lambda b, g, i: (b, i, g)),
                  pl.BlockSpec((1, HEAD_GROUP, Tp, HEAD_PAD), lambda b, g, i: (b, g, 0, 0)),
                  pl.BlockSpec((1, gw, Tp), lambda b, g, i: (b, g, 0)),
                  tab, tab, _const_spec(p['gq'].shape), _const_spec(tables['one'].shape)],
        out_specs=pl.BlockSpec((1, gw, tq), lambda b, g, i: (b, g, i)),
        out_shape=jax.ShapeDtypeStruct((B, MLA_HEADS * V_HEAD_DIM, Tp), BF16),
        compiler_params=_params(("parallel", "parallel", "parallel")),
        name="flash",
    )(bound, q_raw, k, v, tables['cos'], tables['sin'], p['gq'], tables['one'])


def _attn_out_kernel(ot_ref, w_ref, h_ref, o_ref):
    proj = lax.dot_general(ot_ref[0], w_ref[...], (((0,), (0,)), ((), ())), preferred_element_type=F32)
    o_ref[0] = h_ref[0] + proj


def _attn_out(o_t, w, h):
    B, Tp, _ = h.shape
    tm = _pick_tile(Tp, (1408, 384, 128))
    row = pl.BlockSpec((1, tm, D_MODEL), lambda b, i: (b, i, 0))
    return pl.pallas_call(
        _attn_out_kernel,
        grid=(B, Tp // tm),
        in_specs=[pl.BlockSpec((1, o_t.shape[1], tm), lambda b, i: (b, 0, i)), _const_spec(w.shape), row],
        out_specs=row,
        out_shape=jax.ShapeDtypeStruct((B, Tp, D_MODEL), F32),
        compiler_params=_params(("parallel", "parallel")),
        name="attn_out",
    )(o_t, w, h)


def _hidden_chunks():
    bounds, start = [], 0
    while start < FFN_HIDDEN:
        stop = min(start + 1536, FFN_HIDDEN)
        bounds.append((start, stop))
        start = stop
    return bounds


def _ffn_kernel(h_ref, nw_ref, wg_ref, wu_ref, wd_ref, fw_ref, o_ref, *, final):
    h = h_ref[...]
    u = _rms(h, nw_ref[...]).astype(BF16)
    acc = h
    for lo, hi in _hidden_chunks():
        g = jnp.dot(u, wg_ref[:, lo:hi], preferred_element_type=F32)
        up = jnp.dot(u, wu_ref[:, lo:hi], preferred_element_type=F32)
        a = (_silu(g) * up).astype(BF16)
        acc = acc + jnp.dot(a, wd_ref[lo:hi, :], preferred_element_type=F32)
    if final:
        acc = _rms(acc, fw_ref[...])
    o_ref[...] = acc


def _ffn(h, nw, wg, wu, wd, fw, final):
    R = h.shape[0]
    tm = _pick_tile(R, (512, 384, 256, 128))
    row = pl.BlockSpec((tm, D_MODEL), lambda i: (i, 0))
    return pl.pallas_call(
        functools.partial(_ffn_kernel, final=final),
        grid=(R // tm,),
        in_specs=[row] + [_const_spec(a.shape) for a in (nw, wg, wu, wd, fw)],
        out_specs=row,
        out_shape=jax.ShapeDtypeStruct((R, D_MODEL), F32),
        compiler_params=_params(("parallel",)),
        name="ffn",
    )(h, nw, wg, wu, wd, fw)


SUBLANES = 8


def _ssd_inproj_kernel(h_ref, nw_ref, w_ref, dtb_ref, alog_ref, z_ref, xbc_ref, cs_ref, srct_ref, outwt_ref,
                       etot_ref):
    tm = h_ref.shape[1]
    u = _rms(h_ref[0], nw_ref[...]).astype(BF16)
    y = jnp.dot(u, w_ref[...], preferred_element_type=F32)
    row = pl.program_id(1) * tm + lax.broadcasted_iota(jnp.int32, (tm, 1), 0)
    valid = row >= FRONT_PAD
    z_ref[0] = y[:, :D_INNER].astype(BF16)
    xbc_ref[0] = jnp.where(valid, y[:, D_INNER:D_INNER + CONV_DIM], 0.0).astype(BF16)
    dt_raw = y[:, D_INNER + CONV_DIM:] + dtb_ref[...]
    dt_all = jnp.where(valid, jnp.maximum(dt_raw, 0.0) + jnp.log(1.0 + jnp.exp(-jnp.abs(dt_raw))), 0.0)

    neg_a = -jnp.exp(alog_ref[...]) * LOG2E
    forward = lax.broadcasted_iota(jnp.int32, (1, LANES), 1) < SSD_HEADS
    r = lax.broadcasted_iota(jnp.int32, (CHUNK, CHUNK), 0)
    c = lax.broadcasted_iota(jnp.int32, (CHUNK, CHUNK), 1)
    tri_f = (c <= r).astype(BF16)
    tri_b = (c >= r).astype(BF16)
    for ch in range(tm // CHUNK):
        rows = slice(ch * CHUNK, (ch + 1) * CHUNK)
        dt = dt_all[rows]
        rest = dt * neg_a
        cs_f = jnp.zeros((CHUNK, LANES), F32)
        cs_b = jnp.zeros((CHUNK, LANES), F32)
        for _ in range(3):
            part = rest.astype(BF16)
            cs_f = cs_f + jnp.dot(tri_f, part, preferred_element_type=F32)
            cs_b = cs_b + jnp.dot(tri_b, part, preferred_element_type=F32)
            rest = rest - part.astype(F32)
        cs = jnp.where(forward, cs_f, cs_b)
        total = jnp.where(forward, cs_f[CHUNK - 1:CHUNK, :], cs_b[0:1, :])
        cs_ref[0, rows, :] = cs
        srct_ref[0, :, rows] = (cs - jnp.log2(dt)).T
        outwt_ref[0, :, rows] = (jnp.exp2(total - cs) * dt).T
        etot_ref[0, ch * SUBLANES:(ch + 1) * SUBLANES, :] = jnp.broadcast_to(jnp.exp2(total), (SUBLANES, LANES))


def _ssd_inproj(h, nw, w, dtb, a_log):
    B, Tp, _ = h.shape
    tm = _pick_tile(Tp, (384, 128))
    row = lambda b, i: (b, i, 0)
    col = lambda b, i: (b, 0, i)
    per_chunk = SUBLANES * tm // CHUNK
    return pl.pallas_call(
        _ssd_inproj_kernel,
        grid=(B, Tp // tm),
        in_specs=[pl.BlockSpec((1, tm, D_MODEL), row)] + [_const_spec(a.shape) for a in (nw, w, dtb, a_log)],
        out_specs=[pl.BlockSpec((1, tm, D_INNER), row), pl.BlockSpec((1, tm, CONV_DIM), row),
                   pl.BlockSpec((1, tm, LANES), row), pl.BlockSpec((1, LANES, tm), col),
                   pl.BlockSpec((1, LANES, tm), col), pl.BlockSpec((1, per_chunk, LANES), row)],
        out_shape=[jax.ShapeDtypeStruct((B, Tp, D_INNER), BF16),
                   jax.ShapeDtypeStruct((B, Tp, CONV_DIM), BF16),
                   jax.ShapeDtypeStruct((B, Tp, LANES), F32),
                   jax.ShapeDtypeStruct((B, LANES, Tp), F32),
                   jax.ShapeDtypeStruct((B, LANES, Tp), F32),
                   jax.ShapeDtypeStruct((B, SUBLANES * Tp // CHUNK, LANES), F32)],
        compiler_params=_params(("parallel", "parallel")),
        name="ssd_inproj",
    )(h, nw, w, dtb, a_log)


CONV_MARGIN = 16
CONV_ROWS = 128
CONV_SHIFTS = tuple(t - D_CONV // 2 for t in range(D_CONV) if t != D_CONV // 2)


def _conv_kernel(x_ref, w_ref, b_ref, o_ref, xs_ref):
    Tp, cw = x_ref.shape[1], x_ref.shape[2]
    win = CONV_ROWS + 2 * CONV_MARGIN
    zeros = jnp.zeros((CONV_MARGIN, cw), BF16)
    xs_ref[0:CONV_MARGIN, :] = zeros
    xs_ref[CONV_MARGIN + Tp:, :] = zeros
    xs_ref[CONV_MARGIN:CONV_MARGIN + Tp, :] = x_ref[0]
    r = lax.broadcasted_iota(jnp.int32, (len(CONV_SHIFTS) * CONV_ROWS, win), 0)
    j = lax.broadcasted_iota(jnp.int32, (len(CONV_SHIFTS) * CONV_ROWS, win), 1)
    select = jnp.zeros(r.shape, F32)
    for k, shift in enumerate(CONV_SHIFTS):
        hit = (r >= k * CONV_ROWS) & (r < (k + 1) * CONV_ROWS) & (j == r - k * CONV_ROWS + CONV_MARGIN + shift)
        select = jnp.where(hit, 1.0, select)
    select = select.astype(BF16)
    w = [jnp.broadcast_to(w_ref[t:t + 1, :], (CONV_ROWS, cw)) for t in range(D_CONV)]
    b = jnp.broadcast_to(b_ref[...], (CONV_ROWS, cw))
    centre = D_CONV // 2

    def body(i, carry):
        r0 = pl.multiple_of(i * CONV_ROWS, CONV_ROWS)
        shifted = jnp.dot(select, xs_ref[pl.ds(r0, win), :], preferred_element_type=F32)
        acc = b + w[centre] * xs_ref[pl.ds(r0 + CONV_MARGIN, CONV_ROWS), :].astype(F32)
        for k, shift in enumerate(CONV_SHIFTS):
            acc = acc + w[centre + shift] * shifted[k * CONV_ROWS:(k + 1) * CONV_ROWS, :]
        o_ref[0, pl.ds(r0, CONV_ROWS), :] = _silu(acc).astype(BF16)
        return carry

    n_blocks = Tp // CONV_ROWS
    lax.fori_loop(0, n_blocks, body, 0, unroll=next(u for u in (11, 3, 1) if n_blocks % u == 0))


def _ssd_conv(xbc, w, b):
    B, Tp, C = xbc.shape
    cw = 512
    blk = pl.BlockSpec((1, Tp, cw), lambda bi, ci: (bi, 0, ci))
    return pl.pallas_call(
        _conv_kernel,
        grid=(B, C // cw),
        in_specs=[blk, pl.BlockSpec((D_CONV, cw), lambda bi, ci: (0, ci)),
                  pl.BlockSpec((1, cw), lambda bi, ci: (0, ci))],
        out_specs=blk,
        out_shape=jax.ShapeDtypeStruct((B, Tp, C), BF16),
        scratch_shapes=[pltpu.VMEM((Tp + 2 * CONV_MARGIN, cw), BF16)],
        compiler_params=_params(("parallel", "parallel")),
        name="ssd_conv",
    )(xbc, w, b)


def _ssd_decays(cs_ref, srct_ref, outwt_ref, etot_ref, k, reverse):
    row = lax.broadcasted_iota(jnp.int32, (CHUNK, CHUNK), 0)
    col = lax.broadcasted_iota(jnp.int32, (CHUNK, CHUNK), 1)
    causal = (col >= row) if reverse else (col <= row)
    span = slice(k * CHUNK, (k + 1) * CHUNK)
    return dict(
        hide=jnp.where(causal, 0.0, -jnp.inf),
        cs=cs_ref[0, span, :], src_t=srct_ref[0, :, span], out_w_t=outwt_ref[0, :, span],
        e_tot=etot_ref[0, k * SUBLANES:k * SUBLANES + 1, :],
        rows=span, head0=SSD_HEADS if reverse else 0)


def _ssd_group(b_ref, c_ref, g, rows):
    bg = b_ref[0, rows, g * D_STATE:(g + 1) * D_STATE]
    cg = c_ref[0, rows, g * D_STATE:(g + 1) * D_STATE]
    gram = lax.dot_general(cg, bg, (((1,), (1,)), ((), ())), preferred_element_type=F32)
    return gram, cg.astype(F32), bg.astype(F32).T


def _ssd_pair(dec, grp, x_ref, y_ref, st_ref, pair, low):
    gram, cg_f, bg_t = grp
    rows = dec['rows']
    lanes = slice(pair * LANES, (pair + 1) * LANES)
    xp = x_ref[0, rows, lanes]
    state = st_ref[pair]
    rhs = jnp.concatenate([xp, state.astype(BF16)], axis=0)
    hd0 = dec['head0'] + 2 * pair
    ys, ups = [], []
    for hd in (hd0, hd0 + 1):
        cs_l = jnp.broadcast_to(dec['cs'][:, hd:hd + 1], (CHUNK, CHUNK))
        seg = cs_l - dec['src_t'][hd:hd + 1, :]
        within = gram * jnp.exp2(seg + dec['hide'])
        from_state = cg_f * jnp.exp2(cs_l)
        lhs = jnp.concatenate([within, from_state], axis=1).astype(BF16)
        ys.append(jnp.dot(lhs, rhs, preferred_element_type=F32))
        ups.append(jnp.dot((bg_t * dec['out_w_t'][hd:hd + 1, :]).astype(BF16), xp, preferred_element_type=F32))
    y_ref[0, rows, lanes] = jnp.where(low, ys[0], ys[1]).astype(y_ref.dtype)
    e_tot = dec['e_tot']
    keep = jnp.where(low[0:1, :], jnp.broadcast_to(e_tot[:, hd0:hd0 + 1], (1, LANES)),
                     jnp.broadcast_to(e_tot[:, hd0 + 1:hd0 + 2], (1, LANES)))
    st_ref[pair] = state * keep + jnp.where(low, ups[0], ups[1])


def _ssd_scan_kernel(xf_ref, bf_ref, cf_ref, csf_ref, srcf_ref, outf_ref, etotf_ref,
                     xb_ref, bb_ref, cb_ref, csb_ref, srcb_ref, outb_ref, etotb_ref,
                     yf_ref, yb_ref, stf_ref, stb_ref):
    @pl.when(pl.program_id(1) == 0)
    def _():
        stf_ref[...] = jnp.zeros(stf_ref.shape, F32)
        stb_ref[...] = jnp.zeros(stb_ref.shape, F32)

    low = lax.broadcasted_iota(jnp.int32, (CHUNK, LANES), 1) < SSD_HEAD_DIM
    per_step = xf_ref.shape[1] // CHUNK
    pairs_per_group = HEADS_PER_GROUP // 2
    for k in range(per_step):
        dec_f = _ssd_decays(csf_ref, srcf_ref, outf_ref, etotf_ref, k, reverse=False)
        dec_b = _ssd_decays(csb_ref, srcb_ref, outb_ref, etotb_ref, per_step - 1 - k, reverse=True)
        for g in range(SSD_GROUPS):
            grp_f = _ssd_group(bf_ref, cf_ref, g, dec_f['rows'])
            grp_b = _ssd_group(bb_ref, cb_ref, g, dec_b['rows'])
            for pr in range(pairs_per_group):
                pair = g * pairs_per_group + pr
                _ssd_pair(dec_f, grp_f, xf_ref, yf_ref, stf_ref, pair, low)
                _ssd_pair(dec_b, grp_b, xb_ref, yb_ref, stb_ref, pair, low)


def _ssd_scan(xbc, cs, src_t, out_w_t, e_tot):
    B, Tp, _ = xbc.shape
    nc = Tp // CHUNK
    per_step = 3 if nc % 3 == 0 else 1
    nb = nc // per_step
    rows = per_step * CHUNK
    bc_w = SSD_GROUPS * D_STATE
    b_blk = D_INNER // bc_w

    def specs(pos):
        return [pl.BlockSpec((1, rows, D_INNER), lambda b, c: (b, pos(c), 0)),
                pl.BlockSpec((1, rows, bc_w), lambda b, c: (b, pos(c), b_blk)),
                pl.BlockSpec((1, rows, bc_w), lambda b, c: (b, pos(c), b_blk + 1)),
                pl.BlockSpec((1, rows, LANES), lambda b, c: (b, pos(c), 0)),
                pl.BlockSpec((1, LANES, rows), lambda b, c: (b, 0, pos(c))),
                pl.BlockSpec((1, LANES, rows), lambda b, c: (b, 0, pos(c))),
                pl.BlockSpec((1, per_step * SUBLANES, LANES), lambda b, c: (b, pos(c), 0))]

    fwd = lambda c: c
    bwd = lambda c: nb - 1 - c
    y_shape = jax.ShapeDtypeStruct((B, Tp, D_INNER), BF16)
    state = pltpu.VMEM((SSD_HEADS // 2, D_STATE, LANES), F32)
    return pl.pallas_call(
        _ssd_scan_kernel,
        grid=(B, nb),
        in_specs=specs(fwd) + specs(bwd),
        out_specs=[pl.BlockSpec((1, rows, D_INNER), lambda b, c: (b, fwd(c), 0)),
                   pl.BlockSpec((1, rows, D_INNER), lambda b, c: (b, bwd(c), 0))],
        out_shape=[y_shape, y_shape],
        scratch_shapes=[state, state],
        compiler_params=_params(("parallel", "arbitrary")),
        name="ssd_scan",
    )(*(2 * (xbc, xbc, xbc, cs, src_t, out_w_t, e_tot)))


def _ssd_out_kernel(yf_ref, yb_ref, x_ref, z_ref, d_ref, nw_ref, w_ref, h_ref, o_ref):
    y = yf_ref[...].astype(F32) + yb_ref[...].astype(F32) + d_ref[...] * x_ref[...].astype(F32)
    y = y * _silu(z_ref[...].astype(F32))
    r = lax.rsqrt(jnp.mean(y * y, axis=-1, keepdims=True) + EPS)
    proj = jnp.dot((y * nw_ref[...]).astype(BF16), w_ref[...], preferred_element_type=F32)
    o_ref[...] = h_ref[...] + r * proj


def _ssd_out(yf, yb, xbc, z, d_e, nw, w, h):
    R = h.shape[0]
    tm = _pick_tile(R, (512, 384, 256, 128))
    wide = pl.BlockSpec((tm, D_INNER), lambda i: (i, 0))
    row = pl.BlockSpec((tm, D_MODEL), lambda i: (i, 0))
    return pl.pallas_call(
        _ssd_out_kernel,
        grid=(R // tm,),
        in_specs=[wide, wide, wide, wide] + [_const_spec(a.shape) for a in (d_e, nw, w)] + [row],
        out_specs=row,
        out_shape=jax.ShapeDtypeStruct((R, D_MODEL), F32),
        compiler_params=_params(("parallel",)),
        name="ssd_out",
    )(yf, yb, xbc, z, d_e, nw, w, h)


def _row(v, width=None):
    v = v.astype(F32).reshape(1, -1)
    if width is not None and v.shape[1] < width:
        v = jnp.pad(v, ((0, 0), (0, width - v.shape[1])))
    return v


def _prep_mla(w_in, q_norm, kv_norm, w_q_up, w_kv_up, q_head_norm, k_head_norm, w_out):
    lat = Q_LORA_RANK + KV_LORA_RANK
    rope_cols = jnp.pad(w_in[:, lat:], ((0, 0), (ROPE_LANE0, HEAD_PAD - QK_HEAD_DIM)))
    w_in_p = jnp.concatenate([w_in[:, :lat], rope_cols], axis=1).astype(BF16)
    w_q = w_q_up.reshape(Q_LORA_RANK, MLA_HEADS, QK_HEAD_DIM)
    w_q = jnp.pad(w_q, ((0, 0), (0, 0), (0, HEAD_PAD - QK_HEAD_DIM))).reshape(Q_LORA_RANK, -1).astype(BF16)
    w_kv = w_kv_up.reshape(KV_LORA_RANK, MLA_HEADS, QK_NOPE_DIM + V_HEAD_DIM)
    w_k = jnp.pad(w_kv[:, :, :QK_NOPE_DIM], ((0, 0), (0, 0), (0, HEAD_PAD - QK_NOPE_DIM)))
    w_k = w_k.reshape(KV_LORA_RANK, -1).astype(BF16)
    w_v = w_kv[:, :, QK_NOPE_DIM:].reshape(KV_LORA_RANK, -1).T.astype(BF16)
    scale = LOG2E / math.sqrt(QK_HEAD_DIM)
    gq = q_head_norm.astype(F32)
    gk = k_head_norm.astype(F32)
    bound = 1.02 * scale * QK_HEAD_DIM * jnp.max(jnp.abs(gq)) * jnp.max(jnp.abs(gk))
    return dict(
        w_in=w_in_p, q_norm=_row(q_norm), kv_norm=_row(kv_norm), w_q=w_q, w_k=w_k, w_v=w_v,
        bound=bound.reshape(1),
        gq=_row(gq * scale, HEAD_PAD),
        gkn=_row(gk[:QK_NOPE_DIM], HEAD_PAD),
        gkr=_row(jnp.pad(gk[QK_NOPE_DIM:], (ROPE_LANE0, 0)), HEAD_PAD),
        w_out=w_out.astype(BF16))


def _attention_tables(Tp):
    half = QK_ROPE_DIM // 2
    pos = jnp.arange(Tp, dtype=F32) - FRONT_PAD
    inv_freq = ROPE_BASE ** (-jnp.arange(half, dtype=F32) / half)
    ang = pos[:, None] * inv_freq[None, :]
    cos, sin = jnp.cos(ang), jnp.sin(ang)
    tail = HEAD_PAD - QK_HEAD_DIM
    cos_t = jnp.concatenate([jnp.ones((Tp, ROPE_LANE0), F32), cos, cos, jnp.ones((Tp, tail), F32)], axis=1)
    sin_t = jnp.concatenate([jnp.zeros((Tp, ROPE_LANE0), F32), -sin, sin, jnp.zeros((Tp, tail), F32)], axis=1)
    is_pad = (jnp.arange(Tp) < FRONT_PAD)[:, None]
    lane = jnp.arange(HEAD_PAD)[None, :]
    one = (lane == MASK_LANE).astype(F32)
    return dict(cos=cos_t, sin=sin_t, pad_mask=jnp.where(is_pad, MASK_VALUE, 0.0).astype(F32) * one, one=one)


def _key_bias(tables, bound):
    return jnp.where(tables['pad_mask'] < 0, tables['pad_mask'], -bound * tables['one'])


def _prep_ssd(w_in, conv_w, conv_b, dt_bias, a_log, d_skip, norm_w, w_out):
    w_in_p = jnp.pad(w_in, ((0, 0), (0, LANES - 2 * SSD_HEADS))).astype(BF16)
    return dict(
        w_in=w_in_p, conv_w=conv_w.astype(F32), conv_b=_row(conv_b),
        dt_bias=_row(dt_bias, LANES), a_log=_row(a_log, LANES),
        d_e=_row(jnp.repeat(d_skip.astype(F32), SSD_HEAD_DIM)),
        norm=_row(norm_w), w_out=w_out.astype(BF16))


def kernel(x_prompt, x_sample, meta_tokens, mix_norm, ffn_norm, mla_w_in, mla_q_norm, mla_kv_norm, mla_w_q_up, mla_w_kv_up, mla_q_head_norm, mla_k_head_norm, mla_w_out, ssd_w_in, ssd_conv_w, ssd_conv_b, ssd_dt_bias, ssd_a_log, ssd_d, ssd_norm, ssd_w_out, ffn_w_gate, ffn_w_up, ffn_w_down, final_norm):
    assert x_prompt.shape[1:] == x_sample.shape[1:] and x_prompt.shape[2] == D_MODEL
    assert x_prompt.shape[1] % CHUNK == 0
    depth = mix_norm.shape[0]
    n_prompt = x_prompt.shape[0]
    B = n_prompt + x_sample.shape[0]
    S = x_prompt.shape[1]
    Tp = FRONT_PAD + N_META + S
    head = jnp.concatenate([jnp.zeros((FRONT_PAD, D_MODEL), F32), meta_tokens.astype(F32)], axis=0)
    h = jnp.concatenate([jnp.broadcast_to(head[None], (B, CHUNK, D_MODEL)),
                         jnp.concatenate([x_prompt, x_sample], axis=0)], axis=1)
    tables = _attention_tables(Tp)
    attn_p = (mla_w_in, mla_q_norm, mla_kv_norm, mla_w_q_up, mla_w_kv_up,
              mla_q_head_norm, mla_k_head_norm, mla_w_out)
    ssd_p = (ssd_w_in, ssd_conv_w, ssd_conv_b, ssd_dt_bias, ssd_a_log, ssd_d, ssd_norm, ssd_w_out)
    flat = lambda t: t.reshape(B * Tp, t.shape[-1])
    ia = ib = 0
    for layer in range(depth):
        nw = _row(mix_norm[layer])
        if layer % 2 == 0:
            p = _prep_mla(*[t[ia] for t in attn_p])
            ia += 1
            q, k, v = _mla_proj(h, nw, p, tables)
            o = _flash(p['bound'], q, k, v, p, tables)
            h2 = flat(_attn_out(o, p['w_out'], h))
        else:
            p = _prep_ssd(*[t[ib] for t in ssd_p])
            ib += 1
            z, xbc, *decays = _ssd_inproj(h, nw, p['w_in'], p['dt_bias'], p['a_log'])
            xbc = _ssd_conv(xbc, p['conv_w'], p['conv_b'])
            yf, yb = _ssd_scan(xbc, *decays)
            h2 = _ssd_out(flat(yf), flat(yb), flat(xbc), flat(z), p['d_e'], p['norm'], p['w_out'], flat(h))
        h2 = _ffn(h2, _row(ffn_norm[layer]), ffn_w_gate[layer].astype(BF16), ffn_w_up[layer].astype(BF16),
                  ffn_w_down[layer].astype(BF16), _row(final_norm), final=(layer == depth - 1))
        h = h2.reshape(B, Tp, D_MODEL)
    y = h[:, CHUNK:]
    return (y[:n_prompt], y[n_prompt:])
```

```python
import functools
import math

import jax
import jax.numpy as jnp
from jax import lax
from jax.experimental import pallas as pl
from jax.experimental.pallas import tpu as pltpu

F32 = jnp.float32
BF16 = jnp.bfloat16

D_MODEL = 1024
N_META = 16
EPS = 1e-6
MLA_HEADS = 16
QK_NOPE_DIM = 64
QK_ROPE_DIM = 32
QK_HEAD_DIM = QK_NOPE_DIM + QK_ROPE_DIM
V_HEAD_DIM = 64
Q_LORA_RANK = 384
KV_LORA_RANK = 256
ROPE_BASE = 10000.0
D_INNER = 2 * D_MODEL
SSD_HEAD_DIM = 64
SSD_HEADS = D_INNER // SSD_HEAD_DIM
SSD_GROUPS = 4
HEADS_PER_GROUP = SSD_HEADS // SSD_GROUPS
D_STATE = 128
D_CONV = 5
CONV_DIM = D_INNER + 2 * SSD_GROUPS * D_STATE
CHUNK = 128
FFN_HIDDEN = -(-8 * D_MODEL // (3 * 256)) * 256

LANES = 128
FRONT_PAD = CHUNK - N_META
HEAD_PAD = LANES
ROPE_LANE0 = QK_NOPE_DIM
MASK_LANE = QK_HEAD_DIM
MASK_VALUE = -1e30
LOG2E = 1.0 / math.log(2.0)
SOFTMAX_BOUND_LIMIT = 40.0
HEAD_GROUP = 4
GROUP_W = HEADS_PER_GROUP * SSD_HEAD_DIM
VMEM_LIMIT = 56 * 1024 * 1024


def _pick_tile(n, candidates):
    for c in candidates:
        if n % c == 0:
            return c
    return n


def _params(sem, vmem=VMEM_LIMIT):
    return pltpu.CompilerParams(dimension_semantics=sem, vmem_limit_bytes=vmem)


def _const_spec(shape):
    nd = len(shape)
    return pl.BlockSpec(shape, lambda *_: (0,) * nd, pipeline_mode=pl.Buffered(1))


def _rms(x, w):
    return x * lax.rsqrt(jnp.mean(x * x, axis=-1, keepdims=True) + EPS) * w


def _silu(x):
    half = 0.5 * x
    return half + half * jnp.tanh(half)


def _rope(x, cos, sin):
    lane = lax.broadcasted_iota(jnp.int32, x.shape, 1)
    half = QK_ROPE_DIM // 2
    first_half = (lane >= ROPE_LANE0) & (lane < ROPE_LANE0 + half)
    swapped = jnp.where(first_half, pltpu.roll(x, HEAD_PAD - half, 1), pltpu.roll(x, half, 1))
    return x * cos + swapped * sin


def _mla_proj_kernel(h_ref, nw_ref, win_ref, qn_ref, kvn_ref, wq_ref, wk_ref, wv_ref,
                     gkn_ref, gkr_ref, cos_ref, sin_ref, bias_ref, q_ref, k_ref, v_ref):
    u = _rms(h_ref[0], nw_ref[...]).astype(BF16)
    c = jnp.dot(u, win_ref[...], preferred_element_type=F32)
    cq = _rms(c[:, :Q_LORA_RANK], qn_ref[...]).astype(BF16)
    ckv = _rms(c[:, Q_LORA_RANK:Q_LORA_RANK + KV_LORA_RANK], kvn_ref[...]).astype(BF16)
    k_rope = c[:, Q_LORA_RANK + KV_LORA_RANK:]
    q_ref[0] = jnp.dot(cq, wq_ref[...], preferred_element_type=F32)
    k_nope = jnp.dot(ckv, wk_ref[...], preferred_element_type=F32)
    v_ref[0] = lax.dot_general(wv_ref[...], ckv, (((1,), (1,)), ((), ())),
                               preferred_element_type=F32).astype(BF16)

    cos = cos_ref[...]
    sin = sin_ref[...]
    inv_d = 1.0 / QK_HEAD_DIM
    kr = _rope(k_rope * gkr_ref[...], cos, sin)
    kr_ss = jnp.sum(k_rope * k_rope, axis=-1, keepdims=True)
    gkn = gkn_ref[...]
    bias = bias_ref[...]
    for hd in range(MLA_HEADS):
        kh = k_nope[:, hd * HEAD_PAD:(hd + 1) * HEAD_PAD]
        rk = lax.rsqrt((jnp.sum(kh * kh, axis=-1, keepdims=True) + kr_ss) * inv_d + EPS)
        k_ref[0, hd] = ((kh * gkn + kr) * rk + bias).astype(BF16)


def _mla_proj(h, nw, p, tables):
    B, Tp, _ = h.shape
    tm = _pick_tile(Tp, (384, 128))
    row = lambda b, i: (b, i, 0)
    tab = lambda b, i: (i, 0)
    wide = MLA_HEADS * HEAD_PAD
    consts = (nw, p['w_in'], p['q_norm'], p['kv_norm'], p['w_q'], p['w_k'], p['w_v'], p['gkn'], p['gkr'])
    return pl.pallas_call(
        _mla_proj_kernel,
        grid=(B, Tp // tm),
        in_specs=[pl.BlockSpec((1, tm, D_MODEL), row)]
        + [_const_spec(a.shape) for a in consts]
        + [pl.BlockSpec((tm, HEAD_PAD), tab)] * 3,
        out_specs=[pl.BlockSpec((1, tm, wide), row),
                   pl.BlockSpec((1, MLA_HEADS, tm, HEAD_PAD), lambda b, i: (b, 0, i, 0)),
                   pl.BlockSpec((1, MLA_HEADS * V_HEAD_DIM, tm), lambda b, i: (b, 0, i))],
        out_shape=[jax.ShapeDtypeStruct((B, Tp, wide), F32),
                   jax.ShapeDtypeStruct((B, MLA_HEADS, Tp, HEAD_PAD), BF16),
                   jax.ShapeDtypeStruct((B, MLA_HEADS * V_HEAD_DIM, Tp), BF16)],
        compiler_params=_params(("parallel", "parallel")),
        name="mla_proj",
    )(h, *consts, tables['cos'], tables['sin'], _key_bias(tables, p['bound']))


def _flash_kernel(bound_ref, q_ref, k_ref, v_ref, cos_ref, sin_ref, gq_ref, one_ref, o_ref, *, key_chunk):
    Tp = k_ref.shape[2]
    tq = q_ref.shape[1]
    kq = lambda k, q: lax.dot_general(k, q, (((1,), (1,)), ((), ())), preferred_element_type=F32)

    chunks = range(0, Tp, key_chunk)

    def queries(j):
        lane0 = j * HEAD_PAD
        qh = q_ref[0, :, pl.ds(lane0 if isinstance(j, int) else pl.multiple_of(lane0, HEAD_PAD), HEAD_PAD)]
        rq = lax.rsqrt(jnp.sum(qh * qh, axis=-1, keepdims=True) * (1.0 / QK_HEAD_DIM) + EPS)
        return (_rope(qh * rq * gq_ref[...], cos_ref[...], sin_ref[...]) + one_ref[...]).astype(BF16)

    def head(j, q, bounded):
        r0 = j * V_HEAD_DIM
        rows = pl.ds(r0 if isinstance(j, int) else pl.multiple_of(r0, V_HEAD_DIM), V_HEAD_DIM)
        scores = lambda c0: kq(k_ref[0, j, c0:c0 + key_chunk, :], q)
        if bounded:
            shift = None
        else:
            shift = jnp.full((1, tq), -jnp.inf, F32)
            for c0 in chunks:
                shift = jnp.maximum(shift, jnp.max(scores(c0), axis=0, keepdims=True))
        l = jnp.zeros((1, tq), F32)
        o = jnp.zeros((V_HEAD_DIM, tq), F32)
        for c0 in chunks:
            s = scores(c0)
            p = jnp.exp2(s if bounded else s - shift)
            l = l + jnp.sum(p, axis=0, keepdims=True)
            o = o + jnp.dot(v_ref[0, rows, c0:c0 + key_chunk], p.astype(BF16), preferred_element_type=F32)
        o_ref[0, rows, :] = (o * (1.0 / l)).astype(BF16)

    bounded = bound_ref[0] <= SOFTMAX_BOUND_LIMIT

    @pl.when(bounded)
    def _():
        qs = [queries(j) for j in range(HEAD_GROUP)]
        for j in range(HEAD_GROUP):
            head(j, qs[j], True)

    @pl.when(jnp.logical_not(bounded))
    def _():
        def body(j, carry):
            head(j, queries(j), False)
            return carry
        lax.fori_loop(0, HEAD_GROUP, body, 0)


def _flash(bound, q_raw, k, v, p, tables):
    B, Tp, _ = q_raw.shape
    tq = _pick_tile(Tp, (1408, 384, 128))
    gw = HEAD_GROUP * V_HEAD_DIM
    tab = pl.BlockSpec((tq, HEAD_PAD), lambda b, g, i: (i, 0))
    return pl.pallas_call(
        functools.partial(_flash_kernel, key_chunk=_pick_tile(Tp, (1408, 384, 128))),
        grid=(B, MLA_HEADS // HEAD_GROUP, Tp // tq),
        in_specs=[pl.BlockSpec(memory_space=pltpu.SMEM),
                  pl.BlockSpec((1, tq, HEAD_GROUP * HEAD_PAD), lambda b, g, i: (b, i, g)),
                  pl.BlockSpec((1, HEAD_GROUP, Tp, HEAD_PAD), lambda b, g, i: (b, g, 0, 0)),
                  pl.BlockSpec((1, gw, Tp), lambda b, g, i: (b, g, 0)),
                  tab, tab, _const_spec(p['gq'].shape), _const_spec(tables['one'].shape)],
        out_specs=pl.BlockSpec((1, gw, tq), lambda b, g, i: (b, g, i)),
        out_shape=jax.ShapeDtypeStruct((B, MLA_HEADS * V_HEAD_DIM, Tp), BF16),
        compiler_params=_params(("parallel", "parallel", "parallel")),
        name="flash",
    )(bound, q_raw, k, v, tables['cos'], tables['sin'], p['gq'], tables['one'])


def _attn_out_kernel(ot_ref, w_ref, h_ref, o_ref):
    proj = lax.dot_general(ot_ref[0], w_ref[...], (((0,), (0,)), ((), ())), preferred_element_type=F32)
    o_ref[0] = h_ref[0] + proj


def _attn_out(o_t, w, h):
    B, Tp, _ = h.shape
    tm = _pick_tile(Tp, (1408, 384, 128))
    row = pl.BlockSpec((1, tm, D_MODEL), lambda b, i: (b, i, 0))
    return pl.pallas_call(
        _attn_out_kernel,
        grid=(B, Tp // tm),
        in_specs=[pl.BlockSpec((1, o_t.shape[1], tm), lambda b, i: (b, 0, i)), _const_spec(w.shape), row],
        out_specs=row,
        out_shape=jax.ShapeDtypeStruct((B, Tp, D_MODEL), F32),
        compiler_params=_params(("parallel", "parallel")),
        name="attn_out",
    )(o_t, w, h)


def _hidden_chunks():
    bounds, start = [], 0
    while start < FFN_HIDDEN:
        stop = min(start + 1536, FFN_HIDDEN)
        bounds.append((start, stop))
        start = stop
    return bounds


def _ffn_kernel(h_ref, nw_ref, wg_ref, wu_ref, wd_ref, fw_ref, o_ref, *, final):
    h = h_ref[...].reshape(-1, D_MODEL)
    u = _rms(h, nw_ref[...]).astype(BF16)
    acc = h
    for lo, hi in _hidden_chunks():
        g = jnp.dot(u, wg_ref[:, lo:hi], preferred_element_type=F32)
        up = jnp.dot(u, wu_ref[:, lo:hi], preferred_element_type=F32)
        a = (_silu(g) * up).astype(BF16)
        acc = acc + jnp.dot(a, wd_ref[lo:hi, :], preferred_element_type=F32)
    if final:
        acc = _rms(acc, fw_ref[...])
    o_ref[...] = acc.reshape(o_ref.shape)


def _ffn(h, nw, wg, wu, wd, fw):
    R = h.shape[0]
    tm = _pick_tile(R, (768, 512, 384, 256, 128))
    row = pl.BlockSpec((tm, D_MODEL), lambda i: (i, 0))
    return pl.pallas_call(
        functools.partial(_ffn_kernel, final=False),
        grid=(R // tm,),
        in_specs=[row] + [_const_spec(a.shape) for a in (nw, wg, wu, wd, fw)],
        out_specs=row,
        out_shape=jax.ShapeDtypeStruct((R, D_MODEL), F32),
        compiler_params=_params(("parallel",)),
        name="ffn",
    )(h, nw, wg, wu, wd, fw)


def _ffn_final(h, first, count, nw, wg, wu, wd, fw):
    _, Tp, _ = h.shape
    S = Tp - CHUNK
    tm = _pick_tile(S, (512, 384, 256, 128))
    return pl.pallas_call(
        functools.partial(_ffn_kernel, final=True),
        grid=(count, S // tm),
        in_specs=[pl.BlockSpec((pl.Element(1), pl.Element(tm), pl.Element(D_MODEL)),
                               lambda b, i: (b + first, (i * (tm // CHUNK) + 1) * CHUNK, 0))]
        + [_const_spec(a.shape) for a in (nw, wg, wu, wd, fw)],
        out_specs=pl.BlockSpec((1, tm, D_MODEL), lambda b, i: (b, i, 0)),
        out_shape=jax.ShapeDtypeStruct((count, S, D_MODEL), F32),
        compiler_params=_params(("parallel", "parallel")),
        name="ffn_final",
    )(h, nw, wg, wu, wd, fw)


SUBLANES = 8


def _ssd_inproj_kernel(h_ref, nw_ref, w_ref, dtb_ref, alog_ref, z_ref, xbc_ref, cs_ref, srct_ref, outwt_ref,
                       etot_ref):
    tm = h_ref.shape[1]
    u = _rms(h_ref[0], nw_ref[...]).astype(BF16)
    y = jnp.dot(u, w_ref[...], preferred_element_type=F32)
    row = pl.program_id(1) * tm + lax.broadcasted_iota(jnp.int32, (tm, 1), 0)
    valid = row >= FRONT_PAD
    z_ref[0] = _silu(y[:, :D_INNER]).astype(BF16)
    xbc_ref[0] = jnp.where(valid, y[:, D_INNER:D_INNER + CONV_DIM], 0.0).astype(BF16)
    dt_raw = y[:, D_INNER + CONV_DIM:] + dtb_ref[...]
    dt_all = jnp.where(valid, jnp.maximum(dt_raw, 0.0) + jnp.log(1.0 + jnp.exp(-jnp.abs(dt_raw))), 0.0)

    neg_a = -jnp.exp(alog_ref[...]) * LOG2E
    forward = lax.broadcasted_iota(jnp.int32, (1, LANES), 1) < SSD_HEADS
    r = lax.broadcasted_iota(jnp.int32, (CHUNK, CHUNK), 0)
    c = lax.broadcasted_iota(jnp.int32, (CHUNK, CHUNK), 1)
    tri_f = (c <= r).astype(BF16)
    tri_b = (c >= r).astype(BF16)
    for ch in range(tm // CHUNK):
        rows = slice(ch * CHUNK, (ch + 1) * CHUNK)
        dt = dt_all[rows]
        rest = dt * neg_a
        cs_f = jnp.zeros((CHUNK, LANES), F32)
        cs_b = jnp.zeros((CHUNK, LANES), F32)
        for _ in range(3):
            part = rest.astype(BF16)
            cs_f = cs_f + jnp.dot(tri_f, part, preferred_element_type=F32)
            cs_b = cs_b + jnp.dot(tri_b, part, preferred_element_type=F32)
            rest = rest - part.astype(F32)
        cs = jnp.where(forward, cs_f, cs_b)
        total = jnp.where(forward, cs_f[CHUNK - 1:CHUNK, :], cs_b[0:1, :])
        cs_ref[0, rows, :] = cs
        srct_ref[0, :, rows] = (cs - jnp.log2(dt)).T
        outwt_ref[0, :, rows] = (jnp.exp2(total - cs) * dt).T
        etot_ref[0, ch * SUBLANES:(ch + 1) * SUBLANES, :] = jnp.broadcast_to(jnp.exp2(total), (SUBLANES, LANES))


def _ssd_inproj(h, nw, w, dtb, a_log):
    B, Tp, _ = h.shape
    tm = _pick_tile(Tp, (384, 128))
    row = lambda b, i: (b, i, 0)
    col = lambda b, i: (b, 0, i)
    per_chunk = SUBLANES * tm // CHUNK
    return pl.pallas_call(
        _ssd_inproj_kernel,
        grid=(B, Tp // tm),
        in_specs=[pl.BlockSpec((1, tm, D_MODEL), row)] + [_const_spec(a.shape) for a in (nw, w, dtb, a_log)],
        out_specs=[pl.BlockSpec((1, tm, D_INNER), row), pl.BlockSpec((1, tm, CONV_DIM), row),
                   pl.BlockSpec((1, tm, LANES), row), pl.BlockSpec((1, LANES, tm), col),
                   pl.BlockSpec((1, LANES, tm), col), pl.BlockSpec((1, per_chunk, LANES), row)],
        out_shape=[jax.ShapeDtypeStruct((B, Tp, D_INNER), BF16),
                   jax.ShapeDtypeStruct((B, Tp, CONV_DIM), BF16),
                   jax.ShapeDtypeStruct((B, Tp, LANES), F32),
                   jax.ShapeDtypeStruct((B, LANES, Tp), F32),
                   jax.ShapeDtypeStruct((B, LANES, Tp), F32),
                   jax.ShapeDtypeStruct((B, SUBLANES * Tp // CHUNK, LANES), F32)],
        compiler_params=_params(("parallel", "parallel")),
        name="ssd_inproj",
    )(h, nw, w, dtb, a_log)


CONV_MARGIN = 16
CONV_ROWS = 128
CONV_SHIFTS = tuple(t - D_CONV // 2 for t in range(D_CONV) if t != D_CONV // 2)


def _conv_kernel(x_ref, w_ref, b_ref, o_ref, xs_ref):
    Tp, cw = x_ref.shape[1], x_ref.shape[2]
    win = CONV_ROWS + 2 * CONV_MARGIN
    zeros = jnp.zeros((CONV_MARGIN, cw), BF16)
    xs_ref[0:CONV_MARGIN, :] = zeros
    xs_ref[CONV_MARGIN + Tp:, :] = zeros
    xs_ref[CONV_MARGIN:CONV_MARGIN + Tp, :] = x_ref[0]
    r = lax.broadcasted_iota(jnp.int32, (len(CONV_SHIFTS) * CONV_ROWS, win), 0)
    j = lax.broadcasted_iota(jnp.int32, (len(CONV_SHIFTS) * CONV_ROWS, win), 1)
    select = jnp.zeros(r.shape, F32)
    for k, shift in enumerate(CONV_SHIFTS):
        hit = (r >= k * CONV_ROWS) & (r < (k + 1) * CONV_ROWS) & (j == r - k * CONV_ROWS + CONV_MARGIN + shift)
        select = jnp.where(hit, 1.0, select)
    select = select.astype(BF16)
    w = [jnp.broadcast_to(w_ref[t:t + 1, :], (CONV_ROWS, cw)) for t in range(D_CONV)]
    b = jnp.broadcast_to(b_ref[...], (CONV_ROWS, cw))
    centre = D_CONV // 2

    def body(i, carry):
        r0 = pl.multiple_of(i * CONV_ROWS, CONV_ROWS)
        shifted = jnp.dot(select, xs_ref[pl.ds(r0, win), :], preferred_element_type=F32)
        acc = b + w[centre] * xs_ref[pl.ds(r0 + CONV_MARGIN, CONV_ROWS), :].astype(F32)
        for k, shift in enumerate(CONV_SHIFTS):
            acc = acc + w[centre + shift] * shifted[k * CONV_ROWS:(k + 1) * CONV_ROWS, :]
        o_ref[0, pl.ds(r0, CONV_ROWS), :] = _silu(acc).astype(BF16)
        return carry

    n_blocks = Tp // CONV_ROWS
    lax.fori_loop(0, n_blocks, body, 0, unroll=next(u for u in (11, 3, 1) if n_blocks % u == 0))


def _ssd_conv(xbc, w, b):
    B, Tp, C = xbc.shape
    cw = 512
    blk = pl.BlockSpec((1, Tp, cw), lambda bi, ci: (bi, 0, ci))
    return pl.pallas_call(
        _conv_kernel,
        grid=(B, C // cw),
        in_specs=[blk, pl.BlockSpec((D_CONV, cw), lambda bi, ci: (0, ci)),
                  pl.BlockSpec((1, cw), lambda bi, ci: (0, ci))],
        out_specs=blk,
        out_shape=jax.ShapeDtypeStruct((B, Tp, C), BF16),
        scratch_shapes=[pltpu.VMEM((Tp + 2 * CONV_MARGIN, cw), BF16)],
        compiler_params=_params(("parallel", "parallel")),
        name="ssd_conv",
    )(xbc, w, b)


def _ssd_decays(cs_ref, srct_ref, outwt_ref, etot_ref, k, reverse):
    row = lax.broadcasted_iota(jnp.int32, (CHUNK, CHUNK), 0)
    col = lax.broadcasted_iota(jnp.int32, (CHUNK, CHUNK), 1)
    causal = (col >= row) if reverse else (col <= row)
    span = slice(k * CHUNK, (k + 1) * CHUNK)
    return dict(
        hide=jnp.where(causal, 0.0, -jnp.inf),
        cs=cs_ref[0, span, :], src_t=srct_ref[0, :, span], out_w_t=outwt_ref[0, :, span],
        e_tot=etot_ref[0, k * SUBLANES:k * SUBLANES + 1, :],
        rows=span, head0=SSD_HEADS if reverse else 0)


def _ssd_group(b_ref, c_ref, g, rows):
    bg = b_ref[0, rows, g * D_STATE:(g + 1) * D_STATE]
    cg = c_ref[0, rows, g * D_STATE:(g + 1) * D_STATE]
    gram = lax.dot_general(cg, bg, (((1,), (1,)), ((), ())), preferred_element_type=F32)
    return gram, cg.astype(F32), bg.astype(F32).T


def _ssd_pair(dec, grp, x_ref, y_ref, st_ref, pair, low):
    gram, cg_f, bg_t = grp
    rows = dec['rows']
    lanes = slice(pair * LANES, (pair + 1) * LANES)
    xp = x_ref[0, rows, lanes]
    state = st_ref[pair]
    rhs = jnp.concatenate([xp, state.astype(BF16)], axis=0)
    hd0 = dec['head0'] + 2 * pair
    ys, ups = [], []
    for hd in (hd0, hd0 + 1):
        cs_l = jnp.broadcast_to(dec['cs'][:, hd:hd + 1], (CHUNK, CHUNK))
        seg = cs_l - dec['src_t'][hd:hd + 1, :]
        within = gram * jnp.exp2(seg + dec['hide'])
        from_state = cg_f * jnp.exp2(cs_l)
        lhs = jnp.concatenate([within, from_state], axis=1).astype(BF16)
        ys.append(jnp.dot(lhs, rhs, preferred_element_type=F32))
        ups.append(jnp.dot((bg_t * dec['out_w_t'][hd:hd + 1, :]).astype(BF16), xp, preferred_element_type=F32))
    y_ref[0, rows, lanes] = jnp.where(low, ys[0], ys[1]).astype(y_ref.dtype)
    e_tot = dec['e_tot']
    keep = jnp.where(low[0:1, :], jnp.broadcast_to(e_tot[:, hd0:hd0 + 1], (1, LANES)),
                     jnp.broadcast_to(e_tot[:, hd0 + 1:hd0 + 2], (1, LANES)))
    st_ref[pair] = state * keep + jnp.where(low, ups[0], ups[1])


def _ssd_scan_kernel(xf_ref, bf_ref, cf_ref, csf_ref, srcf_ref, outf_ref, etotf_ref,
                     xb_ref, bb_ref, cb_ref, csb_ref, srcb_ref, outb_ref, etotb_ref,
                     yf_ref, yb_ref, stf_ref, stb_ref):
    @pl.when(pl.program_id(1) == 0)
    def _():
        stf_ref[...] = jnp.zeros(stf_ref.shape, F32)
        stb_ref[...] = jnp.zeros(stb_ref.shape, F32)

    low = lax.broadcasted_iota(jnp.int32, (CHUNK, LANES), 1) < SSD_HEAD_DIM
    per_step = xf_ref.shape[1] // CHUNK
    pairs_per_group = HEADS_PER_GROUP // 2
    for k in range(per_step):
        dec_f = _ssd_decays(csf_ref, srcf_ref, outf_ref, etotf_ref, k, reverse=False)
        dec_b = _ssd_decays(csb_ref, srcb_ref, outb_ref, etotb_ref, per_step - 1 - k, reverse=True)
        for g in range(SSD_GROUPS):
            grp_f = _ssd_group(bf_ref, cf_ref, g, dec_f['rows'])
            grp_b = _ssd_group(bb_ref, cb_ref, g, dec_b['rows'])
            for pr in range(pairs_per_group):
                pair = g * pairs_per_group + pr
                _ssd_pair(dec_f, grp_f, xf_ref, yf_ref, stf_ref, pair, low)
                _ssd_pair(dec_b, grp_b, xb_ref, yb_ref, stb_ref, pair, low)


def _ssd_scan(xbc, cs, src_t, out_w_t, e_tot):
    B, Tp, _ = xbc.shape
    nc = Tp // CHUNK
    per_step = 3 if nc % 3 == 0 else 1
    nb = nc // per_step
    rows = per_step * CHUNK
    bc_w = SSD_GROUPS * D_STATE
    b_blk = D_INNER // bc_w

    def specs(pos):
        return [pl.BlockSpec((1, rows, D_INNER), lambda b, c: (b, pos(c), 0)),
                pl.BlockSpec((1, rows, bc_w), lambda b, c: (b, pos(c), b_blk)),
                pl.BlockSpec((1, rows, bc_w), lambda b, c: (b, pos(c), b_blk + 1)),
                pl.BlockSpec((1, rows, LANES), lambda b, c: (b, pos(c), 0)),
                pl.BlockSpec((1, LANES, rows), lambda b, c: (b, 0, pos(c))),
                pl.BlockSpec((1, LANES, rows), lambda b, c: (b, 0, pos(c))),
                pl.BlockSpec((1, per_step * SUBLANES, LANES), lambda b, c: (b, pos(c), 0))]

    fwd = lambda c: c
    bwd = lambda c: nb - 1 - c
    y_shape = jax.ShapeDtypeStruct((B, Tp, D_INNER), BF16)
    state = pltpu.VMEM((SSD_HEADS // 2, D_STATE, LANES), F32)
    return pl.pallas_call(
        _ssd_scan_kernel,
        grid=(B, nb),
        in_specs=specs(fwd) + specs(bwd),
        out_specs=[pl.BlockSpec((1, rows, D_INNER), lambda b, c: (b, fwd(c), 0)),
                   pl.BlockSpec((1, rows, D_INNER), lambda b, c: (b, bwd(c), 0))],
        out_shape=[y_shape, y_shape],
        scratch_shapes=[state, state],
        compiler_params=_params(("parallel", "arbitrary")),
        name="ssd_scan",
    )(*(2 * (xbc, xbc, xbc, cs, src_t, out_w_t, e_tot)))


def _ssd_out_kernel(yf_ref, yb_ref, x_ref, z_ref, d_ref, nw_ref, w_ref, h_ref, o_ref):
    y = yf_ref[...].astype(F32) + yb_ref[...].astype(F32) + d_ref[...] * x_ref[...].astype(F32)
    y = y * z_ref[...].astype(F32)
    r = lax.rsqrt(jnp.mean(y * y, axis=-1, keepdims=True) + EPS)
    proj = jnp.dot((y * nw_ref[...]).astype(BF16), w_ref[...], preferred_element_type=F32)
    o_ref[...] = h_ref[...] + r * proj


def _ssd_out(yf, yb, xbc, z, d_e, nw, w, h):
    R = h.shape[0]
    tm = _pick_tile(R, (512, 384, 256, 128))
    wide = pl.BlockSpec((tm, D_INNER), lambda i: (i, 0))
    row = pl.BlockSpec((tm, D_MODEL), lambda i: (i, 0))
    return pl.pallas_call(
        _ssd_out_kernel,
        grid=(R // tm,),
        in_specs=[wide, wide, wide, wide] + [_const_spec(a.shape) for a in (d_e, nw, w)] + [row],
        out_specs=row,
        out_shape=jax.ShapeDtypeStruct((R, D_MODEL), F32),
        compiler_params=_params(("parallel",)),
        name="ssd_out",
    )(yf, yb, xbc, z, d_e, nw, w, h)


def _row(v, width=None):
    v = v.astype(F32).reshape(1, -1)
    if width is not None and v.shape[1] < width:
        v = jnp.pad(v, ((0, 0), (0, width - v.shape[1])))
    return v


def _prep_mla(w_in, q_norm, kv_norm, w_q_up, w_kv_up, q_head_norm, k_head_norm, w_out):
    lat = Q_LORA_RANK + KV_LORA_RANK
    rope_cols = jnp.pad(w_in[:, lat:], ((0, 0), (ROPE_LANE0, HEAD_PAD - QK_HEAD_DIM)))
    w_in_p = jnp.concatenate([w_in[:, :lat], rope_cols], axis=1).astype(BF16)
    w_q = w_q_up.reshape(Q_LORA_RANK, MLA_HEADS, QK_HEAD_DIM)
    w_q = jnp.pad(w_q, ((0, 0), (0, 0), (0, HEAD_PAD - QK_HEAD_DIM))).reshape(Q_LORA_RANK, -1).astype(BF16)
    w_kv = w_kv_up.reshape(KV_LORA_RANK, MLA_HEADS, QK_NOPE_DIM + V_HEAD_DIM)
    w_k = jnp.pad(w_kv[:, :, :QK_NOPE_DIM], ((0, 0), (0, 0), (0, HEAD_PAD - QK_NOPE_DIM)))
    w_k = w_k.reshape(KV_LORA_RANK, -1).astype(BF16)
    w_v = w_kv[:, :, QK_NOPE_DIM:].reshape(KV_LORA_RANK, -1).T.astype(BF16)
    scale = LOG2E / math.sqrt(QK_HEAD_DIM)
    gq = q_head_norm.astype(F32)
    gk = k_head_norm.astype(F32)
    bound = 1.02 * scale * QK_HEAD_DIM * jnp.max(jnp.abs(gq)) * jnp.max(jnp.abs(gk))
    return dict(
        w_in=w_in_p, q_norm=_row(q_norm), kv_norm=_row(kv_norm), w_q=w_q, w_k=w_k, w_v=w_v,
        bound=bound.reshape(1),
        gq=_row(gq * scale, HEAD_PAD),
        gkn=_row(gk[:QK_NOPE_DIM], HEAD_PAD),
        gkr=_row(jnp.pad(gk[QK_NOPE_DIM:], (ROPE_LANE0, 0)), HEAD_PAD),
        w_out=w_out.astype(BF16))


def _attention_tables(Tp):
    half = QK_ROPE_DIM // 2
    pos = jnp.arange(Tp, dtype=F32) - FRONT_PAD
    inv_freq = ROPE_BASE ** (-jnp.arange(half, dtype=F32) / half)
    ang = pos[:, None] * inv_freq[None, :]
    cos, sin = jnp.cos(ang), jnp.sin(ang)
    tail = HEAD_PAD - QK_HEAD_DIM
    cos_t = jnp.concatenate([jnp.ones((Tp, ROPE_LANE0), F32), cos, cos, jnp.ones((Tp, tail), F32)], axis=1)
    sin_t = jnp.concatenate([jnp.zeros((Tp, ROPE_LANE0), F32), -sin, sin, jnp.zeros((Tp, tail), F32)], axis=1)
    is_pad = (jnp.arange(Tp) < FRONT_PAD)[:, None]
    lane = jnp.arange(HEAD_PAD)[None, :]
    one = (lane == MASK_LANE).astype(F32)
    return dict(cos=cos_t, sin=sin_t, pad_mask=jnp.where(is_pad, MASK_VALUE, 0.0).astype(F32) * one, one=one)


def _key_bias(tables, bound):
    return jnp.where(tables['pad_mask'] < 0, tables['pad_mask'], -bound * tables['one'])


def _prep_ssd(w_in, conv_w, conv_b, dt_bias, a_log, d_skip, norm_w, w_out):
    w_in_p = jnp.pad(w_in, ((0, 0), (0, LANES - 2 * SSD_HEADS))).astype(BF16)
    return dict(
        w_in=w_in_p, conv_w=conv_w.astype(F32), conv_b=_row(conv_b),
        dt_bias=_row(dt_bias, LANES), a_log=_row(a_log, LANES),
        d_e=_row(jnp.repeat(d_skip.astype(F32), SSD_HEAD_DIM)),
        norm=_row(norm_w), w_out=w_out.astype(BF16))


def kernel(x_prompt, x_sample, meta_tokens, mix_norm, ffn_norm, mla_w_in, mla_q_norm, mla_kv_norm, mla_w_q_up, mla_w_kv_up, mla_q_head_norm, mla_k_head_norm, mla_w_out, ssd_w_in, ssd_conv_w, ssd_conv_b, ssd_dt_bias, ssd_a_log, ssd_d, ssd_norm, ssd_w_out, ffn_w_gate, ffn_w_up, ffn_w_down, final_norm):
    assert x_prompt.shape[1:] == x_sample.shape[1:] and x_prompt.shape[2] == D_MODEL
    assert x_prompt.shape[1] % CHUNK == 0
    depth = mix_norm.shape[0]
    n_prompt = x_prompt.shape[0]
    B = n_prompt + x_sample.shape[0]
    S = x_prompt.shape[1]
    Tp = FRONT_PAD + N_META + S
    head = jnp.concatenate([jnp.zeros((FRONT_PAD, D_MODEL), F32), meta_tokens.astype(F32)], axis=0)
    h = jnp.concatenate([jnp.broadcast_to(head[None], (B, CHUNK, D_MODEL)),
                         jnp.concatenate([x_prompt, x_sample], axis=0)], axis=1)
    tables = _attention_tables(Tp)
    attn_p = (mla_w_in, mla_q_norm, mla_kv_norm, mla_w_q_up, mla_w_kv_up,
              mla_q_head_norm, mla_k_head_norm, mla_w_out)
    ssd_p = (ssd_w_in, ssd_conv_w, ssd_conv_b, ssd_dt_bias, ssd_a_log, ssd_d, ssd_norm, ssd_w_out)
    flat = lambda t: t.reshape(B * Tp, t.shape[-1])
    ia = ib = 0
    for layer in range(depth):
        nw = _row(mix_norm[layer])
        if layer % 2 == 0:
            p = _prep_mla(*[t[ia] for t in attn_p])
            ia += 1
            q, k, v = _mla_proj(h, nw, p, tables)
            o = _flash(p['bound'], q, k, v, p, tables)
            h2 = flat(_attn_out(o, p['w_out'], h))
        else:
            p = _prep_ssd(*[t[ib] for t in ssd_p])
            ib += 1
            z, xbc, *decays = _ssd_inproj(h, nw, p['w_in'], p['dt_bias'], p['a_log'])
            xbc = _ssd_conv(xbc, p['conv_w'], p['conv_b'])
            yf, yb = _ssd_scan(xbc, *decays)
            h2 = _ssd_out(flat(yf), flat(yb), flat(xbc), flat(z), p['d_e'], p['norm'], p['w_out'], flat(h))
        ffn_args = (_row(ffn_norm[layer]), ffn_w_gate[layer].astype(BF16), ffn_w_up[layer].astype(BF16),
                    ffn_w_down[layer].astype(BF16), _row(final_norm))
        if layer < depth - 1:
            h = _ffn(h2, *ffn_args).reshape(B, Tp, D_MODEL)
    h = h2.reshape(B, Tp, D_MODEL)
    return (_ffn_final(h, 0, n_prompt, *ffn_args), _ffn_final(h, n_prompt, B - n_prompt, *ffn_args))
```

```python
import functools
import math

import jax
import jax.numpy as jnp
from jax import lax
from jax.experimental import pallas as pl
from jax.experimental.pallas import tpu as pltpu

F32 = jnp.float32
BF16 = jnp.bfloat16

D_MODEL = 1024
N_META = 16
EPS = 1e-6
MLA_HEADS = 16
QK_NOPE_DIM = 64
QK_ROPE_DIM = 32
QK_HEAD_DIM = QK_NOPE_DIM + QK_ROPE_DIM
V_HEAD_DIM = 64
Q_LORA_RANK = 384
KV_LORA_RANK = 256
ROPE_BASE = 10000.0
D_INNER = 2 * D_MODEL
SSD_HEAD_DIM = 64
SSD_HEADS = D_INNER // SSD_HEAD_DIM
SSD_GROUPS = 4
HEADS_PER_GROUP = SSD_HEADS // SSD_GROUPS
D_STATE = 128
D_CONV = 5
CONV_DIM = D_INNER + 2 * SSD_GROUPS * D_STATE
CHUNK = 128
FFN_HIDDEN = -(-8 * D_MODEL // (3 * 256)) * 256

LANES = 128
FRONT_PAD = CHUNK - N_META
HEAD_PAD = LANES
ROPE_LANE0 = QK_NOPE_DIM
MASK_LANE = QK_HEAD_DIM
MASK_VALUE = -1e30
LOG2E = 1.0 / math.log(2.0)
SOFTMAX_BOUND_LIMIT = 40.0
HEAD_GROUP = 4
VMEM_LIMIT = 56 * 1024 * 1024


def _pick_tile(n, candidates):
    for c in candidates:
        if n % c == 0:
            return c
    return n


def _params(sem, vmem=VMEM_LIMIT):
    return pltpu.CompilerParams(dimension_semantics=sem, vmem_limit_bytes=vmem)


def _const_spec(shape):
    nd = len(shape)
    return pl.BlockSpec(shape, lambda *_: (0,) * nd, pipeline_mode=pl.Buffered(1))


def _rms(x, w):
    return x * lax.rsqrt(jnp.mean(x * x, axis=-1, keepdims=True) + EPS) * w


def _silu(x):
    half = 0.5 * x
    return half + half * jnp.tanh(half)


def _rope(x, cos, sin):
    lane = lax.broadcasted_iota(jnp.int32, x.shape, 1)
    half = QK_ROPE_DIM // 2
    first_half = (lane >= ROPE_LANE0) & (lane < ROPE_LANE0 + half)
    swapped = jnp.where(first_half, pltpu.roll(x, HEAD_PAD - half, 1), pltpu.roll(x, half, 1))
    return x * cos + swapped * sin


def _mla_proj_kernel(h_ref, nw_ref, win_ref, qn_ref, kvn_ref, wq_ref, wk_ref, wv_ref,
                     gkn_ref, gkr_ref, cos_ref, sin_ref, bias_ref, q_ref, k_ref, v_ref):
    u = _rms(h_ref[0], nw_ref[...]).astype(BF16)
    c = jnp.dot(u, win_ref[...], preferred_element_type=F32)
    cq = _rms(c[:, :Q_LORA_RANK], qn_ref[...]).astype(BF16)
    ckv = _rms(c[:, Q_LORA_RANK:Q_LORA_RANK + KV_LORA_RANK], kvn_ref[...]).astype(BF16)
    k_rope = c[:, Q_LORA_RANK + KV_LORA_RANK:]
    q_ref[0] = jnp.dot(cq, wq_ref[...], preferred_element_type=F32)
    k_nope = jnp.dot(ckv, wk_ref[...], preferred_element_type=F32)
    v_ref[0] = lax.dot_general(wv_ref[...], ckv, (((1,), (1,)), ((), ())),
                               preferred_element_type=F32).astype(BF16)

    cos = cos_ref[...]
    sin = sin_ref[...]
    inv_d = 1.0 / QK_HEAD_DIM
    kr = _rope(k_rope * gkr_ref[...], cos, sin)
    kr_ss = jnp.sum(k_rope * k_rope, axis=-1, keepdims=True)
    gkn = gkn_ref[...]
    bias = bias_ref[...]
    for hd in range(MLA_HEADS):
        kh = k_nope[:, hd * HEAD_PAD:(hd + 1) * HEAD_PAD]
        rk = lax.rsqrt((jnp.sum(kh * kh, axis=-1, keepdims=True) + kr_ss) * inv_d + EPS)
        k_ref[0, hd] = ((kh * gkn + kr) * rk + bias).astype(BF16)


def _mla_proj(h, nw, p, tables):
    B, Tp, _ = h.shape
    tm = _pick_tile(Tp, (384, 128))
    row = lambda b, i: (b, i, 0)
    tab = lambda b, i: (i, 0)
    wide = MLA_HEADS * HEAD_PAD
    consts = (nw, p['w_in'], p['q_norm'], p['kv_norm'], p['w_q'], p['w_k'], p['w_v'], p['gkn'], p['gkr'])
    return pl.pallas_call(
        _mla_proj_kernel,
        grid=(B, Tp // tm),
        in_specs=[pl.BlockSpec((1, tm, D_MODEL), row)]
        + [_const_spec(a.shape) for a in consts]
        + [pl.BlockSpec((tm, HEAD_PAD), tab)] * 3,
        out_specs=[pl.BlockSpec((1, tm, wide), row),
                   pl.BlockSpec((1, MLA_HEADS, tm, HEAD_PAD), lambda b, i: (b, 0, i, 0)),
                   pl.BlockSpec((1, MLA_HEADS * V_HEAD_DIM, tm), lambda b, i: (b, 0, i))],
        out_shape=[jax.ShapeDtypeStruct((B, Tp, wide), F32),
                   jax.ShapeDtypeStruct((B, MLA_HEADS, Tp, HEAD_PAD), BF16),
                   jax.ShapeDtypeStruct((B, MLA_HEADS * V_HEAD_DIM, Tp), BF16)],
        compiler_params=_params(("parallel", "parallel")),
        name="mla_proj",
    )(h, *consts, tables['cos'], tables['sin'], _key_bias(tables, p['bound']))


def _flash_kernel(bound_ref, q_ref, k_ref, v_ref, cos_ref, sin_ref, gq_ref, one_ref, o_ref, *, key_chunk):
    Tp = k_ref.shape[2]
    tq = q_ref.shape[1]
    kq = lambda k, q: lax.dot_general(k, q, (((1,), (1,)), ((), ())), preferred_element_type=F32)

    chunks = range(0, Tp, key_chunk)

    def queries(j):
        lane0 = j * HEAD_PAD
        qh = q_ref[0, :, pl.ds(lane0 if isinstance(j, int) else pl.multiple_of(lane0, HEAD_PAD), HEAD_PAD)]
        rq = lax.rsqrt(jnp.sum(qh * qh, axis=-1, keepdims=True) * (1.0 / QK_HEAD_DIM) + EPS)
        return (_rope(qh * rq * gq_ref[...], cos_ref[...], sin_ref[...]) + one_ref[...]).astype(BF16)

    def head(j, q, bounded):
        r0 = j * V_HEAD_DIM
        rows = pl.ds(r0 if isinstance(j, int) else pl.multiple_of(r0, V_HEAD_DIM), V_HEAD_DIM)
        scores = lambda c0: kq(k_ref[0, j, c0:c0 + key_chunk, :], q)
        if bounded:
            shift = None
        else:
            shift = jnp.full((1, tq), -jnp.inf, F32)
            for c0 in chunks:
                shift = jnp.maximum(shift, jnp.max(scores(c0), axis=0, keepdims=True))
        l = jnp.zeros((1, tq), F32)
        o = jnp.zeros((V_HEAD_DIM, tq), F32)
        for c0 in chunks:
            s = scores(c0)
            p = jnp.exp2(s if bounded else s - shift)
            l = l + jnp.sum(p, axis=0, keepdims=True)
            o = o + jnp.dot(v_ref[0, rows, c0:c0 + key_chunk], p.astype(BF16), preferred_element_type=F32)
        o_ref[0, rows, :] = (o * (1.0 / l)).astype(BF16)

    bounded = bound_ref[0] <= SOFTMAX_BOUND_LIMIT

    @pl.when(bounded)
    def _():
        qs = [queries(j) for j in range(HEAD_GROUP)]
        for j in range(HEAD_GROUP):
            head(j, qs[j], True)

    @pl.when(jnp.logical_not(bounded))
    def _():
        def body(j, carry):
            head(j, queries(j), False)
            return carry
        lax.fori_loop(0, HEAD_GROUP, body, 0)


def _flash(bound, q_raw, k, v, p, tables):
    B, Tp, _ = q_raw.shape
    tq = _pick_tile(Tp, (1408, 384, 128))
    gw = HEAD_GROUP * V_HEAD_DIM
    tab = pl.BlockSpec((tq, HEAD_PAD), lambda b, g, i: (i, 0))
    return pl.pallas_call(
        functools.partial(_flash_kernel, key_chunk=_pick_tile(Tp, (1408, 384, 128))),
        grid=(B, MLA_HEADS // HEAD_GROUP, Tp // tq),
        in_specs=[pl.BlockSpec(memory_space=pltpu.SMEM),
                  pl.BlockSpec((1, tq, HEAD_GROUP * HEAD_PAD), lambda b, g, i: (b, i, g)),
                  pl.BlockSpec((1, HEAD_GROUP, Tp, HEAD_PAD), lambda b, g, i: (b, g, 0, 0)),
                  pl.BlockSpec((1, gw, Tp), lambda b, g, i: (b, g, 0)),
                  tab, tab, _const_spec(p['gq'].shape), _const_spec(tables['one'].shape)],
        out_specs=pl.BlockSpec((1, gw, tq), lambda b, g, i: (b, g, i)),
        out_shape=jax.ShapeDtypeStruct((B, MLA_HEADS * V_HEAD_DIM, Tp), BF16),
        compiler_params=_params(("parallel", "parallel", "parallel")),
        name="flash",
    )(bound, q_raw, k, v, tables['cos'], tables['sin'], p['gq'], tables['one'])


def _attn_out_kernel(ot_ref, w_ref, h_ref, o_ref):
    proj = lax.dot_general(ot_ref[0], w_ref[...], (((0,), (0,)), ((), ())), preferred_element_type=F32)
    o_ref[0] = h_ref[0] + proj


def _attn_out(o_t, w, h):
    B, Tp, _ = h.shape
    tm = _pick_tile(Tp, (1408, 384, 128))
    row = pl.BlockSpec((1, tm, D_MODEL), lambda b, i: (b, i, 0))
    return pl.pallas_call(
        _attn_out_kernel,
        grid=(B, Tp // tm),
        in_specs=[pl.BlockSpec((1, o_t.shape[1], tm), lambda b, i: (b, 0, i)), _const_spec(w.shape), row],
        out_specs=row,
        out_shape=jax.ShapeDtypeStruct((B, Tp, D_MODEL), F32),
        compiler_params=_params(("parallel", "parallel")),
        name="attn_out",
    )(o_t, w, h)


def _hidden_chunks():
    bounds, start = [], 0
    while start < FFN_HIDDEN:
        stop = min(start + 1536, FFN_HIDDEN)
        bounds.append((start, stop))
        start = stop
    return bounds


def _ffn_kernel(h_ref, nw_ref, wg_ref, wu_ref, wd_ref, fw_ref, o_ref, *, final):
    h = h_ref[...].reshape(-1, D_MODEL)
    u = _rms(h, nw_ref[...]).astype(BF16)
    acc = h
    for lo, hi in _hidden_chunks():
        g = jnp.dot(u, wg_ref[:, lo:hi], preferred_element_type=F32)
        up = jnp.dot(u, wu_ref[:, lo:hi], preferred_element_type=F32)
        a = (_silu(g) * up).astype(BF16)
        acc = acc + jnp.dot(a, wd_ref[lo:hi, :], preferred_element_type=F32)
    if final:
        acc = _rms(acc, fw_ref[...])
    o_ref[...] = acc.reshape(o_ref.shape)


def _ffn(h, nw, wg, wu, wd, fw):
    R = h.shape[0]
    tm = _pick_tile(R, (768, 512, 384, 256, 128))
    row = pl.BlockSpec((tm, D_MODEL), lambda i: (i, 0))
    return pl.pallas_call(
        functools.partial(_ffn_kernel, final=False),
        grid=(R // tm,),
        in_specs=[row] + [_const_spec(a.shape) for a in (nw, wg, wu, wd, fw)],
        out_specs=row,
        out_shape=jax.ShapeDtypeStruct((R, D_MODEL), F32),
        compiler_params=_params(("parallel",)),
        name="ffn",
    )(h, nw, wg, wu, wd, fw)


def _ffn_final(h, first, count, nw, wg, wu, wd, fw):
    _, Tp, _ = h.shape
    S = Tp - CHUNK
    tm = _pick_tile(S, (512, 384, 256, 128))
    return pl.pallas_call(
        functools.partial(_ffn_kernel, final=True),
        grid=(count, S // tm),
        in_specs=[pl.BlockSpec((pl.Element(1), pl.Element(tm), pl.Element(D_MODEL)),
                               lambda b, i: (b + first, (i * (tm // CHUNK) + 1) * CHUNK, 0))]
        + [_const_spec(a.shape) for a in (nw, wg, wu, wd, fw)],
        out_specs=pl.BlockSpec((1, tm, D_MODEL), lambda b, i: (b, i, 0)),
        out_shape=jax.ShapeDtypeStruct((count, S, D_MODEL), F32),
        compiler_params=_params(("parallel", "parallel")),
        name="ffn_final",
    )(h, nw, wg, wu, wd, fw)


SUBLANES = 8


def _ssd_inproj_kernel(h_ref, nw_ref, w_ref, dtb_ref, alog_ref, z_ref, xbc_ref, cs_ref, srct_ref, outwt_ref,
                       etot_ref):
    tm = h_ref.shape[1]
    u = _rms(h_ref[0], nw_ref[...]).astype(BF16)
    y = jnp.dot(u, w_ref[...], preferred_element_type=F32)
    row = pl.program_id(1) * tm + lax.broadcasted_iota(jnp.int32, (tm, 1), 0)
    valid = row >= FRONT_PAD
    z_ref[0] = _silu(y[:, :D_INNER]).astype(BF16)
    xbc_ref[0] = jnp.where(valid, y[:, D_INNER:D_INNER + CONV_DIM], 0.0).astype(BF16)
    dt_raw = y[:, D_INNER + CONV_DIM:] + dtb_ref[...]
    dt_all = jnp.where(valid, jnp.maximum(dt_raw, 0.0) + jnp.log(1.0 + jnp.exp(-jnp.abs(dt_raw))), 0.0)

    neg_a = -jnp.exp(alog_ref[...]) * LOG2E
    forward = lax.broadcasted_iota(jnp.int32, (1, LANES), 1) < SSD_HEADS
    r = lax.broadcasted_iota(jnp.int32, (CHUNK, CHUNK), 0)
    c = lax.broadcasted_iota(jnp.int32, (CHUNK, CHUNK), 1)
    tri_f = (c <= r).astype(BF16)
    tri_b = (c >= r).astype(BF16)
    for ch in range(tm // CHUNK):
        rows = slice(ch * CHUNK, (ch + 1) * CHUNK)
        dt = dt_all[rows]
        rest = dt * neg_a
        cs_f = jnp.zeros((CHUNK, LANES), F32)
        cs_b = jnp.zeros((CHUNK, LANES), F32)
        for _ in range(3):
            part = rest.astype(BF16)
            cs_f = cs_f + jnp.dot(tri_f, part, preferred_element_type=F32)
            cs_b = cs_b + jnp.dot(tri_b, part, preferred_element_type=F32)
            rest = rest - part.astype(F32)
        cs = jnp.where(forward, cs_f, cs_b)
        total = jnp.where(forward, cs_f[CHUNK - 1:CHUNK, :], cs_b[0:1, :])
        cs_ref[0, rows, :] = cs
        srct_ref[0, :, rows] = (cs - jnp.log2(dt)).T
        outwt_ref[0, :, rows] = (jnp.exp2(total - cs) * dt).T
        etot_ref[0, ch * SUBLANES:(ch + 1) * SUBLANES, :] = jnp.broadcast_to(jnp.exp2(total), (SUBLANES, LANES))


def _ssd_inproj(h, nw, w, dtb, a_log):
    B, Tp, _ = h.shape
    tm = _pick_tile(Tp, (384, 128))
    row = lambda b, i: (b, i, 0)
    col = lambda b, i: (b, 0, i)
    per_chunk = SUBLANES * tm // CHUNK
    return pl.pallas_call(
        _ssd_inproj_kernel,
        grid=(B, Tp // tm),
        in_specs=[pl.BlockSpec((1, tm, D_MODEL), row)] + [_const_spec(a.shape) for a in (nw, w, dtb, a_log)],
        out_specs=[pl.BlockSpec((1, tm, D_INNER), row), pl.BlockSpec((1, tm, CONV_DIM), row),
                   pl.BlockSpec((1, tm, LANES), row), pl.BlockSpec((1, LANES, tm), col),
                   pl.BlockSpec((1, LANES, tm), col), pl.BlockSpec((1, per_chunk, LANES), row)],
        out_shape=[jax.ShapeDtypeStruct((B, Tp, D_INNER), BF16),
                   jax.ShapeDtypeStruct((B, Tp, CONV_DIM), BF16),
                   jax.ShapeDtypeStruct((B, Tp, LANES), F32),
                   jax.ShapeDtypeStruct((B, LANES, Tp), F32),
                   jax.ShapeDtypeStruct((B, LANES, Tp), F32),
                   jax.ShapeDtypeStruct((B, SUBLANES * Tp // CHUNK, LANES), F32)],
        compiler_params=_params(("parallel", "parallel")),
        name="ssd_inproj",
    )(h, nw, w, dtb, a_log)


CONV_MARGIN = 16
CONV_ROWS = 128
CONV_SHIFTS = tuple(t - D_CONV // 2 for t in range(D_CONV) if t != D_CONV // 2)


def _conv_kernel(x_ref, w_ref, b_ref, o_ref, xs_ref):
    Tp, cw = x_ref.shape[1], x_ref.shape[2]
    win = CONV_ROWS + 2 * CONV_MARGIN
    zeros = jnp.zeros((CONV_MARGIN, cw), BF16)
    xs_ref[0:CONV_MARGIN, :] = zeros
    xs_ref[CONV_MARGIN + Tp:, :] = zeros
    xs_ref[CONV_MARGIN:CONV_MARGIN + Tp, :] = x_ref[0]
    r = lax.broadcasted_iota(jnp.int32, (len(CONV_SHIFTS) * CONV_ROWS, win), 0)
    j = lax.broadcasted_iota(jnp.int32, (len(CONV_SHIFTS) * CONV_ROWS, win), 1)
    select = jnp.zeros(r.shape, F32)
    for k, shift in enumerate(CONV_SHIFTS):
        hit = (r >= k * CONV_ROWS) & (r < (k + 1) * CONV_ROWS) & (j == r - k * CONV_ROWS + CONV_MARGIN + shift)
        select = jnp.where(hit, 1.0, select)
    select = select.astype(BF16)
    w = [jnp.broadcast_to(w_ref[t:t + 1, :], (CONV_ROWS, cw)) for t in range(D_CONV)]
    b = jnp.broadcast_to(b_ref[...], (CONV_ROWS, cw))
    centre = D_CONV // 2

    def body(i, carry):
        r0 = pl.multiple_of(i * CONV_ROWS, CONV_ROWS)
        shifted = jnp.dot(select, xs_ref[pl.ds(r0, win), :], preferred_element_type=F32)
        acc = b + w[centre] * xs_ref[pl.ds(r0 + CONV_MARGIN, CONV_ROWS), :].astype(F32)
        for k, shift in enumerate(CONV_SHIFTS):
            acc = acc + w[centre + shift] * shifted[k * CONV_ROWS:(k + 1) * CONV_ROWS, :]
        o_ref[0, pl.ds(r0, CONV_ROWS), :] = _silu(acc).astype(BF16)
        return carry

    n_blocks = Tp // CONV_ROWS
    lax.fori_loop(0, n_blocks, body, 0, unroll=next(u for u in (11, 3, 1) if n_blocks % u == 0))


def _ssd_conv(xbc, w, b):
    B, Tp, C = xbc.shape
    cw = 512
    blk = pl.BlockSpec((1, Tp, cw), lambda bi, ci: (bi, 0, ci))
    return pl.pallas_call(
        _conv_kernel,
        grid=(B, C // cw),
        in_specs=[blk, pl.BlockSpec((D_CONV, cw), lambda bi, ci: (0, ci)),
                  pl.BlockSpec((1, cw), lambda bi, ci: (0, ci))],
        out_specs=blk,
        out_shape=jax.ShapeDtypeStruct((B, Tp, C), BF16),
        scratch_shapes=[pltpu.VMEM((Tp + 2 * CONV_MARGIN, cw), BF16)],
        compiler_params=_params(("parallel", "parallel")),
        name="ssd_conv",
    )(xbc, w, b)


def _ssd_decays(cs_ref, srct_ref, outwt_ref, etot_ref, k, reverse):
    row = lax.broadcasted_iota(jnp.int32, (CHUNK, CHUNK), 0)
    col = lax.broadcasted_iota(jnp.int32, (CHUNK, CHUNK), 1)
    causal = (col >= row) if reverse else (col <= row)
    span = slice(k * CHUNK, (k + 1) * CHUNK)
    return dict(
        hide=jnp.where(causal, 0.0, -jnp.inf),
        cs=cs_ref[0, span, :], src_t=srct_ref[0, :, span], out_w_t=outwt_ref[0, :, span],
        e_tot=etot_ref[0, k * SUBLANES:k * SUBLANES + 1, :],
        rows=span, head0=SSD_HEADS if reverse else 0)


def _ssd_group(b_ref, c_ref, g, rows):
    bg = b_ref[0, rows, g * D_STATE:(g + 1) * D_STATE]
    cg = c_ref[0, rows, g * D_STATE:(g + 1) * D_STATE]
    gram = lax.dot_general(cg, bg, (((1,), (1,)), ((), ())), preferred_element_type=F32)
    return gram, cg.astype(F32), bg.astype(F32).T


def _ssd_pair(dec, grp, x_ref, y_ref, st_ref, pair, low):
    gram, cg_f, bg_t = grp
    rows = dec['rows']
    lanes = slice(pair * LANES, (pair + 1) * LANES)
    xp = x_ref[0, rows, lanes]
    state = st_ref[pair]
    rhs = jnp.concatenate([xp, state.astype(BF16)], axis=0)
    hd0 = dec['head0'] + 2 * pair
    ys, ups = [], []
    for hd in (hd0, hd0 + 1):
        cs_l = jnp.broadcast_to(dec['cs'][:, hd:hd + 1], (CHUNK, CHUNK))
        seg = cs_l - dec['src_t'][hd:hd + 1, :]
        within = gram * jnp.exp2(seg + dec['hide'])
        from_state = cg_f * jnp.exp2(cs_l)
        lhs = jnp.concatenate([within, from_state], axis=1).astype(BF16)
        ys.append(jnp.dot(lhs, rhs, preferred_element_type=F32))
        ups.append(jnp.dot((bg_t * dec['out_w_t'][hd:hd + 1, :]).astype(BF16), xp, preferred_element_type=F32))
    y_ref[0, rows, lanes] = jnp.where(low, ys[0], ys[1]).astype(y_ref.dtype)
    e_tot = dec['e_tot']
    keep = jnp.where(low[0:1, :], jnp.broadcast_to(e_tot[:, hd0:hd0 + 1], (1, LANES)),
                     jnp.broadcast_to(e_tot[:, hd0 + 1:hd0 + 2], (1, LANES)))
    st_ref[pair] = state * keep + jnp.where(low, ups[0], ups[1])


def _ssd_scan_kernel(xf_ref, bf_ref, cf_ref, csf_ref, srcf_ref, outf_ref, etotf_ref,
                     xb_ref, bb_ref, cb_ref, csb_ref, srcb_ref, outb_ref, etotb_ref,
                     yf_ref, yb_ref, stf_ref, stb_ref):
    @pl.when(pl.program_id(1) == 0)
    def _():
        stf_ref[...] = jnp.zeros(stf_ref.shape, F32)
        stb_ref[...] = jnp.zeros(stb_ref.shape, F32)

    low = lax.broadcasted_iota(jnp.int32, (CHUNK, LANES), 1) < SSD_HEAD_DIM
    per_step = xf_ref.shape[1] // CHUNK
    pairs_per_group = HEADS_PER_GROUP // 2
    for k in range(per_step):
        dec_f = _ssd_decays(csf_ref, srcf_ref, outf_ref, etotf_ref, k, reverse=False)
        dec_b = _ssd_decays(csb_ref, srcb_ref, outb_ref, etotb_ref, per_step - 1 - k, reverse=True)
        for g in range(SSD_GROUPS):
            grp_f = _ssd_group(bf_ref, cf_ref, g, dec_f['rows'])
            grp_b = _ssd_group(bb_ref, cb_ref, g, dec_b['rows'])
            for pr in range(pairs_per_group):
                pair = g * pairs_per_group + pr
                _ssd_pair(dec_f, grp_f, xf_ref, yf_ref, stf_ref, pair, low)
                _ssd_pair(dec_b, grp_b, xb_ref, yb_ref, stb_ref, pair, low)


def _ssd_scan(xbc, cs, src_t, out_w_t, e_tot):
    B, Tp, _ = xbc.shape
    nc = Tp // CHUNK
    per_step = 3 if nc % 3 == 0 else 1
    nb = nc // per_step
    rows = per_step * CHUNK
    bc_w = SSD_GROUPS * D_STATE
    b_blk = D_INNER // bc_w

    def specs(pos):
        return [pl.BlockSpec((1, rows, D_INNER), lambda b, c: (b, pos(c), 0)),
                pl.BlockSpec((1, rows, bc_w), lambda b, c: (b, pos(c), b_blk)),
                pl.BlockSpec((1, rows, bc_w), lambda b, c: (b, pos(c), b_blk + 1)),
                pl.BlockSpec((1, rows, LANES), lambda b, c: (b, pos(c), 0)),
                pl.BlockSpec((1, LANES, rows), lambda b, c: (b, 0, pos(c))),
                pl.BlockSpec((1, LANES, rows), lambda b, c: (b, 0, pos(c))),
                pl.BlockSpec((1, per_step * SUBLANES, LANES), lambda b, c: (b, pos(c), 0))]

    fwd = lambda c: c
    bwd = lambda c: nb - 1 - c
    y_shape = jax.ShapeDtypeStruct((B, Tp, D_INNER), BF16)
    state = pltpu.VMEM((SSD_HEADS // 2, D_STATE, LANES), F32)
    return pl.pallas_call(
        _ssd_scan_kernel,
        grid=(B, nb),
        in_specs=specs(fwd) + specs(bwd),
        out_specs=[pl.BlockSpec((1, rows, D_INNER), lambda b, c: (b, fwd(c), 0)),
                   pl.BlockSpec((1, rows, D_INNER), lambda b, c: (b, bwd(c), 0))],
        out_shape=[y_shape, y_shape],
        scratch_shapes=[state, state],
        compiler_params=_params(("parallel", "arbitrary")),
        name="ssd_scan",
    )(*(2 * (xbc, xbc, xbc, cs, src_t, out_w_t, e_tot)))


def _ssd_out_kernel(yf_ref, yb_ref, x_ref, z_ref, d_ref, nw_ref, w_ref, h_ref, o_ref):
    y = yf_ref[...].astype(F32) + yb_ref[...].astype(F32) + d_ref[...] * x_ref[...].astype(F32)
    y = y * z_ref[...].astype(F32)
    r = lax.rsqrt(jnp.mean(y * y, axis=-1, keepdims=True) + EPS)
    proj = jnp.dot((y * nw_ref[...]).astype(BF16), w_ref[...], preferred_element_type=F32)
    o_ref[...] = h_ref[...] + r * proj


def _ssd_out(yf, yb, xbc, z, d_e, nw, w, h):
    R = h.shape[0]
    tm = _pick_tile(R, (512, 384, 256, 128))
    wide = pl.BlockSpec((tm, D_INNER), lambda i: (i, 0))
    row = pl.BlockSpec((tm, D_MODEL), lambda i: (i, 0))
    return pl.pallas_call(
        _ssd_out_kernel,
        grid=(R // tm,),
        in_specs=[wide, wide, wide, wide] + [_const_spec(a.shape) for a in (d_e, nw, w)] + [row],
        out_specs=row,
        out_shape=jax.ShapeDtypeStruct((R, D_MODEL), F32),
        compiler_params=_params(("parallel",)),
        name="ssd_out",
    )(yf, yb, xbc, z, d_e, nw, w, h)


def _row(v, width=None):
    v = v.astype(F32).reshape(1, -1)
    if width is not None and v.shape[1] < width:
        v = jnp.pad(v, ((0, 0), (0, width - v.shape[1])))
    return v


def _prep_mla(w_in, q_norm, kv_norm, w_q_up, w_kv_up, q_head_norm, k_head_norm, w_out):
    lat = Q_LORA_RANK + KV_LORA_RANK
    rope_cols = jnp.pad(w_in[:, lat:], ((0, 0), (ROPE_LANE0, HEAD_PAD - QK_HEAD_DIM)))
    w_in_p = jnp.concatenate([w_in[:, :lat], rope_cols], axis=1).astype(BF16)
    w_q = w_q_up.reshape(Q_LORA_RANK, MLA_HEADS, QK_HEAD_DIM)
    w_q = jnp.pad(w_q, ((0, 0), (0, 0), (0, HEAD_PAD - QK_HEAD_DIM))).reshape(Q_LORA_RANK, -1).astype(BF16)
    w_kv = w_kv_up.reshape(KV_LORA_RANK, MLA_HEADS, QK_NOPE_DIM + V_HEAD_DIM)
    w_k = jnp.pad(w_kv[:, :, :QK_NOPE_DIM], ((0, 0), (0, 0), (0, HEAD_PAD - QK_NOPE_DIM)))
    w_k = w_k.reshape(KV_LORA_RANK, -1).astype(BF16)
    w_v = w_kv[:, :, QK_NOPE_DIM:].reshape(KV_LORA_RANK, -1).T.astype(BF16)
    scale = LOG2E / math.sqrt(QK_HEAD_DIM)
    gq = q_head_norm.astype(F32)
    gk = k_head_norm.astype(F32)
    bound = 1.02 * scale * QK_HEAD_DIM * jnp.max(jnp.abs(gq)) * jnp.max(jnp.abs(gk))
    return dict(
        w_in=w_in_p, q_norm=_row(q_norm), kv_norm=_row(kv_norm), w_q=w_q, w_k=w_k, w_v=w_v,
        bound=bound.reshape(1),
        gq=_row(gq * scale, HEAD_PAD),
        gkn=_row(gk[:QK_NOPE_DIM], HEAD_PAD),
        gkr=_row(jnp.pad(gk[QK_NOPE_DIM:], (ROPE_LANE0, 0)), HEAD_PAD),
        w_out=w_out.astype(BF16))


def _attention_tables(Tp):
    half = QK_ROPE_DIM // 2
    pos = jnp.arange(Tp, dtype=F32) - FRONT_PAD
    inv_freq = ROPE_BASE ** (-jnp.arange(half, dtype=F32) / half)
    ang = pos[:, None] * inv_freq[None, :]
    cos, sin = jnp.cos(ang), jnp.sin(ang)
    tail = HEAD_PAD - QK_HEAD_DIM
    cos_t = jnp.concatenate([jnp.ones((Tp, ROPE_LANE0), F32), cos, cos, jnp.ones((Tp, tail), F32)], axis=1)
    sin_t = jnp.concatenate([jnp.zeros((Tp, ROPE_LANE0), F32), -sin, sin, jnp.zeros((Tp, tail), F32)], axis=1)
    is_pad = (jnp.arange(Tp) < FRONT_PAD)[:, None]
    lane = jnp.arange(HEAD_PAD)[None, :]
    one = (lane == MASK_LANE).astype(F32)
    return dict(cos=cos_t, sin=sin_t, pad_mask=jnp.where(is_pad, MASK_VALUE, 0.0).astype(F32) * one, one=one)


def _key_bias(tables, bound):
    return jnp.where(tables['pad_mask'] < 0, tables['pad_mask'], -bound * tables['one'])


def _prep_ssd(w_in, conv_w, conv_b, dt_bias, a_log, d_skip, norm_w, w_out):
    w_in_p = jnp.pad(w_in, ((0, 0), (0, LANES - 2 * SSD_HEADS))).astype(BF16)
    return dict(
        w_in=w_in_p, conv_w=conv_w.astype(F32), conv_b=_row(conv_b),
        dt_bias=_row(dt_bias, LANES), a_log=_row(a_log, LANES),
        d_e=_row(jnp.repeat(d_skip.astype(F32), SSD_HEAD_DIM)),
        norm=_row(norm_w), w_out=w_out.astype(BF16))


def kernel(x_prompt, x_sample, meta_tokens, mix_norm, ffn_norm, mla_w_in, mla_q_norm, mla_kv_norm, mla_w_q_up, mla_w_kv_up, mla_q_head_norm, mla_k_head_norm, mla_w_out, ssd_w_in, ssd_conv_w, ssd_conv_b, ssd_dt_bias, ssd_a_log, ssd_d, ssd_norm, ssd_w_out, ffn_w_gate, ffn_w_up, ffn_w_down, final_norm):
    assert x_prompt.shape[1:] == x_sample.shape[1:] and x_prompt.shape[2] == D_MODEL
    assert x_prompt.shape[1] % CHUNK == 0
    depth = mix_norm.shape[0]
    n_prompt = x_prompt.shape[0]
    B = n_prompt + x_sample.shape[0]
    S = x_prompt.shape[1]
    Tp = FRONT_PAD + N_META + S
    head = jnp.concatenate([jnp.zeros((FRONT_PAD, D_MODEL), F32), meta_tokens.astype(F32)], axis=0)
    h = jnp.concatenate([jnp.broadcast_to(head[None], (B, CHUNK, D_MODEL)),
                         jnp.concatenate([x_prompt, x_sample], axis=0)], axis=1)
    tables = _attention_tables(Tp)
    attn_p = (mla_w_in, mla_q_norm, mla_kv_norm, mla_w_q_up, mla_w_kv_up,
              mla_q_head_norm, mla_k_head_norm, mla_w_out)
    ssd_p = (ssd_w_in, ssd_conv_w, ssd_conv_b, ssd_dt_bias, ssd_a_log, ssd_d, ssd_norm, ssd_w_out)
    flat = lambda t: t.reshape(B * Tp, t.shape[-1])
    ia = ib = 0
    for layer in range(depth):
        nw = _row(mix_norm[layer])
        if layer % 2 == 0:
            p = _prep_mla(*[t[ia] for t in attn_p])
            ia += 1
            q, k, v = _mla_proj(h, nw, p, tables)
            o = _flash(p['bound'], q, k, v, p, tables)
            h2 = flat(_attn_out(o, p['w_out'], h))
        else:
            p = _prep_ssd(*[t[ib] for t in ssd_p])
            ib += 1
            z, xbc, *decays = _ssd_inproj(h, nw, p['w_in'], p['dt_bias'], p['a_log'])
            xbc = _ssd_conv(xbc, p['conv_w'], p['conv_b'])
            yf, yb = _ssd_scan(xbc, *decays)
            h2 = _ssd_out(flat(yf), flat(yb), flat(xbc), flat(z), p['d_e'], p['norm'], p['w_out'], flat(h))
        ffn_args = (_row(ffn_norm[layer]), ffn_w_gate[layer].astype(BF16), ffn_w_up[layer].astype(BF16),
                    ffn_w_down[layer].astype(BF16), _row(final_norm))
        if layer < depth - 1:
            h = _ffn(h2, *ffn_args).reshape(B, Tp, D_MODEL)
    h = h2.reshape(B, Tp, D_MODEL)
    return (_ffn_final(h, 0, n_prompt, *ffn_args), _ffn_final(h, n_prompt, B - n_prompt, *ffn_args))
```

```python
import functools
import math

import jax
import jax.numpy as jnp
from jax import lax
from jax.experimental import pallas as pl
from jax.experimental.pallas import tpu as pltpu

F32 = jnp.float32
BF16 = jnp.bfloat16

D_MODEL = 1024
N_META = 16
EPS = 1e-6
MLA_HEADS = 16
QK_NOPE_DIM = 64
QK_ROPE_DIM = 32
QK_HEAD_DIM = QK_NOPE_DIM + QK_ROPE_DIM
V_HEAD_DIM = 64
Q_LORA_RANK = 384
KV_LORA_RANK = 256
ROPE_BASE = 10000.0
D_INNER = 2 * D_MODEL
SSD_HEAD_DIM = 64
SSD_HEADS = D_INNER // SSD_HEAD_DIM
SSD_GROUPS = 4
HEADS_PER_GROUP = SSD_HEADS // SSD_GROUPS
D_STATE = 128
D_CONV = 5
CONV_DIM = D_INNER + 2 * SSD_GROUPS * D_STATE
CHUNK = 128
FFN_HIDDEN = -(-8 * D_MODEL // (3 * 256)) * 256

LANES = 128
FRONT_PAD = CHUNK - N_META
HEAD_PAD = LANES
ROPE_LANE0 = QK_NOPE_DIM
MASK_LANE = QK_HEAD_DIM
MASK_VALUE = -1e30
LOG2E = 1.0 / math.log(2.0)
SOFTMAX_BOUND_LIMIT = 40.0
HEAD_GROUP = 4
VMEM_LIMIT = 56 * 1024 * 1024


def _pick_tile(n, candidates):
    for c in candidates:
        if n % c == 0:
            return c
    return n


def _params(sem, vmem=VMEM_LIMIT):
    return pltpu.CompilerParams(dimension_semantics=sem, vmem_limit_bytes=vmem)


def _const_spec(shape):
    nd = len(shape)
    return pl.BlockSpec(shape, lambda *_: (0,) * nd, pipeline_mode=pl.Buffered(1))


def _rms(x, w):
    return x * lax.rsqrt(jnp.mean(x * x, axis=-1, keepdims=True) + EPS) * w


def _silu(x):
    half = 0.5 * x
    return half + half * jnp.tanh(half)


def _rope(x, cos, sin):
    lane = lax.broadcasted_iota(jnp.int32, x.shape, 1)
    half = QK_ROPE_DIM // 2
    first_half = (lane >= ROPE_LANE0) & (lane < ROPE_LANE0 + half)
    swapped = jnp.where(first_half, pltpu.roll(x, HEAD_PAD - half, 1), pltpu.roll(x, half, 1))
    return x * cos + swapped * sin


def _mla_proj_kernel(h_ref, nw_ref, win_ref, qn_ref, kvn_ref, wq_ref, wk_ref, wv_ref,
                     gkn_ref, gkr_ref, cos_ref, sin_ref, bias_ref, q_ref, k_ref, v_ref):
    u = _rms(h_ref[0], nw_ref[...]).astype(BF16)
    c = jnp.dot(u, win_ref[...], preferred_element_type=F32)
    cq = _rms(c[:, :Q_LORA_RANK], qn_ref[...]).astype(BF16)
    ckv = _rms(c[:, Q_LORA_RANK:Q_LORA_RANK + KV_LORA_RANK], kvn_ref[...]).astype(BF16)
    k_rope = c[:, Q_LORA_RANK + KV_LORA_RANK:]
    q_ref[0] = jnp.dot(cq, wq_ref[...], preferred_element_type=F32)
    k_nope = jnp.dot(ckv, wk_ref[...], preferred_element_type=F32)
    v_ref[0] = lax.dot_general(wv_ref[...], ckv, (((1,), (1,)), ((), ())),
                               preferred_element_type=F32).astype(BF16)

    cos = cos_ref[...]
    sin = sin_ref[...]
    inv_d = 1.0 / QK_HEAD_DIM
    kr = _rope(k_rope * gkr_ref[...], cos, sin)
    kr_ss = jnp.sum(k_rope * k_rope, axis=-1, keepdims=True)
    gkn = gkn_ref[...]
    bias = bias_ref[...]
    for hd in range(MLA_HEADS):
        kh = k_nope[:, hd * HEAD_PAD:(hd + 1) * HEAD_PAD]
        rk = lax.rsqrt((jnp.sum(kh * kh, axis=-1, keepdims=True) + kr_ss) * inv_d + EPS)
        k_ref[0, hd] = ((kh * gkn + kr) * rk + bias).astype(BF16)


def _mla_proj(h, nw, p, tables):
    B, Tp, _ = h.shape
    tm = _pick_tile(Tp, (384, 128))
    row = lambda b, i: (b, i, 0)
    tab = lambda b, i: (i, 0)
    wide = MLA_HEADS * HEAD_PAD
    consts = (nw, p['w_in'], p['q_norm'], p['kv_norm'], p['w_q'], p['w_k'], p['w_v'], p['gkn'], p['gkr'])
    return pl.pallas_call(
        _mla_proj_kernel,
        grid=(B, Tp // tm),
        in_specs=[pl.BlockSpec((1, tm, D_MODEL), row)]
        + [_const_spec(a.shape) for a in consts]
        + [pl.BlockSpec((tm, HEAD_PAD), tab)] * 3,
        out_specs=[pl.BlockSpec((1, tm, wide), row),
                   pl.BlockSpec((1, MLA_HEADS, tm, HEAD_PAD), lambda b, i: (b, 0, i, 0)),
                   pl.BlockSpec((1, MLA_HEADS * V_HEAD_DIM, tm), lambda b, i: (b, 0, i))],
        out_shape=[jax.ShapeDtypeStruct((B, Tp, wide), F32),
                   jax.ShapeDtypeStruct((B, MLA_HEADS, Tp, HEAD_PAD), BF16),
                   jax.ShapeDtypeStruct((B, MLA_HEADS * V_HEAD_DIM, Tp), BF16)],
        compiler_params=_params(("parallel", "parallel")),
        name="mla_proj",
    )(h, *consts, tables['cos'], tables['sin'], _key_bias(tables, p['bound']))


def _flash_kernel(bound_ref, q_ref, k_ref, v_ref, cos_ref, sin_ref, gq_ref, one_ref, o_ref, *, key_chunk):
    Tp = k_ref.shape[2]
    tq = q_ref.shape[1]
    kq = lambda k, q: lax.dot_general(k, q, (((1,), (1,)), ((), ())), preferred_element_type=F32)

    chunks = range(0, Tp, key_chunk)

    def queries(j):
        lane0 = j * HEAD_PAD
        qh = q_ref[0, :, pl.ds(lane0 if isinstance(j, int) else pl.multiple_of(lane0, HEAD_PAD), HEAD_PAD)]
        rq = lax.rsqrt(jnp.sum(qh * qh, axis=-1, keepdims=True) * (1.0 / QK_HEAD_DIM) + EPS)
        return (_rope(qh * rq * gq_ref[...], cos_ref[...], sin_ref[...]) + one_ref[...]).astype(BF16)

    def head(j, q, bounded):
        r0 = j * V_HEAD_DIM
        rows = pl.ds(r0 if isinstance(j, int) else pl.multiple_of(r0, V_HEAD_DIM), V_HEAD_DIM)
        scores = lambda c0: kq(k_ref[0, j, c0:c0 + key_chunk, :], q)
        if bounded:
            shift = None
        else:
            shift = jnp.full((1, tq), -jnp.inf, F32)
            for c0 in chunks:
                shift = jnp.maximum(shift, jnp.max(scores(c0), axis=0, keepdims=True))
        l = jnp.zeros((1, tq), F32)
        o = jnp.zeros((V_HEAD_DIM, tq), F32)
        for c0 in chunks:
            s = scores(c0)
            p = jnp.exp2(s if bounded else s - shift)
            l = l + jnp.sum(p, axis=0, keepdims=True)
            o = o + jnp.dot(v_ref[0, rows, c0:c0 + key_chunk], p.astype(BF16), preferred_element_type=F32)
        o_ref[0, rows, :] = (o * (1.0 / l)).astype(BF16)

    bounded = bound_ref[0] <= SOFTMAX_BOUND_LIMIT

    @pl.when(bounded)
    def _():
        qs = [queries(j) for j in range(HEAD_GROUP)]
        for j in range(HEAD_GROUP):
            head(j, qs[j], True)

    @pl.when(jnp.logical_not(bounded))
    def _():
        def body(j, carry):
            head(j, queries(j), False)
            return carry
        lax.fori_loop(0, HEAD_GROUP, body, 0)


def _flash(bound, q_raw, k, v, p, tables):
    B, Tp, _ = q_raw.shape
    tq = _pick_tile(Tp, (1408, 384, 128))
    gw = HEAD_GROUP * V_HEAD_DIM
    tab = pl.BlockSpec((tq, HEAD_PAD), lambda b, g, i: (i, 0))
    return pl.pallas_call(
        functools.partial(_flash_kernel, key_chunk=_pick_tile(Tp, (2112, 1408, 384, 128))),
        grid=(B, MLA_HEADS // HEAD_GROUP, Tp // tq),
        in_specs=[pl.BlockSpec(memory_space=pltpu.SMEM),
                  pl.BlockSpec((1, tq, HEAD_GROUP * HEAD_PAD), lambda b, g, i: (b, i, g)),
                  pl.BlockSpec((1, HEAD_GROUP, Tp, HEAD_PAD), lambda b, g, i: (b, g, 0, 0)),
                  pl.BlockSpec((1, gw, Tp), lambda b, g, i: (b, g, 0)),
                  tab, tab, _const_spec(p['gq'].shape), _const_spec(tables['one'].shape)],
        out_specs=pl.BlockSpec((1, gw, tq), lambda b, g, i: (b, g, i)),
        out_shape=jax.ShapeDtypeStruct((B, MLA_HEADS * V_HEAD_DIM, Tp), BF16),
        compiler_params=_params(("parallel", "parallel", "parallel")),
        name="flash",
    )(bound, q_raw, k, v, tables['cos'], tables['sin'], p['gq'], tables['one'])


def _attn_out_kernel(ot_ref, w_ref, h_ref, o_ref):
    proj = lax.dot_general(ot_ref[0], w_ref[...], (((0,), (0,)), ((), ())), preferred_element_type=F32)
    o_ref[0] = h_ref[0] + proj


def _attn_out(o_t, w, h):
    B, Tp, _ = h.shape
    tm = _pick_tile(Tp, (1408, 384, 128))
    row = pl.BlockSpec((1, tm, D_MODEL), lambda b, i: (b, i, 0))
    return pl.pallas_call(
        _attn_out_kernel,
        grid=(B, Tp // tm),
        in_specs=[pl.BlockSpec((1, o_t.shape[1], tm), lambda b, i: (b, 0, i)), _const_spec(w.shape), row],
        out_specs=row,
        out_shape=jax.ShapeDtypeStruct((B, Tp, D_MODEL), F32),
        compiler_params=_params(("parallel", "parallel")),
        name="attn_out",
    )(o_t, w, h)


def _hidden_chunks():
    bounds, start = [], 0
    while start < FFN_HIDDEN:
        stop = min(start + 1536, FFN_HIDDEN)
        bounds.append((start, stop))
        start = stop
    return bounds


def _ffn_kernel(h_ref, nw_ref, wg_ref, wu_ref, wd_ref, fw_ref, o_ref, *, final):
    h = h_ref[...].reshape(-1, D_MODEL)
    u = _rms(h, nw_ref[...]).astype(BF16)
    acc = h
    for lo, hi in _hidden_chunks():
        g = jnp.dot(u, wg_ref[:, lo:hi], preferred_element_type=F32)
        up = jnp.dot(u, wu_ref[:, lo:hi], preferred_element_type=F32)
        a = (_silu(g) * up).astype(BF16)
        acc = acc + jnp.dot(a, wd_ref[lo:hi, :], preferred_element_type=F32)
    if final:
        acc = _rms(acc, fw_ref[...])
    o_ref[...] = acc.reshape(o_ref.shape)


def _ffn(h, nw, wg, wu, wd, fw):
    R = h.shape[0]
    tm = _pick_tile(R, (768, 512, 384, 256, 128))
    row = pl.BlockSpec((tm, D_MODEL), lambda i: (i, 0))
    return pl.pallas_call(
        functools.partial(_ffn_kernel, final=False),
        grid=(R // tm,),
        in_specs=[row] + [_const_spec(a.shape) for a in (nw, wg, wu, wd, fw)],
        out_specs=row,
        out_shape=jax.ShapeDtypeStruct((R, D_MODEL), F32),
        compiler_params=_params(("parallel",)),
        name="ffn",
    )(h, nw, wg, wu, wd, fw)


def _ffn_final(h, first, count, nw, wg, wu, wd, fw):
    _, Tp, _ = h.shape
    S = Tp - CHUNK
    tm = _pick_tile(S, (512, 384, 256, 128))
    return pl.pallas_call(
        functools.partial(_ffn_kernel, final=True),
        grid=(count, S // tm),
        in_specs=[pl.BlockSpec((pl.Element(1), pl.Element(tm), pl.Element(D_MODEL)),
                               lambda b, i: (b + first, (i * (tm // CHUNK) + 1) * CHUNK, 0))]
        + [_const_spec(a.shape) for a in (nw, wg, wu, wd, fw)],
        out_specs=pl.BlockSpec((1, tm, D_MODEL), lambda b, i: (b, i, 0)),
        out_shape=jax.ShapeDtypeStruct((count, S, D_MODEL), F32),
        compiler_params=_params(("parallel", "parallel")),
        name="ffn_final",
    )(h, nw, wg, wu, wd, fw)


SUBLANES = 8


def _ssd_inproj_kernel(h_ref, nw_ref, w_ref, dtb_ref, alog_ref, z_ref, xbc_ref, cs_ref, srct_ref, outwt_ref,
                       etot_ref):
    tm = h_ref.shape[1]
    u = _rms(h_ref[0], nw_ref[...]).astype(BF16)
    y = jnp.dot(u, w_ref[...], preferred_element_type=F32)
    row = pl.program_id(1) * tm + lax.broadcasted_iota(jnp.int32, (tm, 1), 0)
    valid = row >= FRONT_PAD
    z_ref[0] = _silu(y[:, :D_INNER]).astype(BF16)
    xbc_ref[0] = jnp.where(valid, y[:, D_INNER:D_INNER + CONV_DIM], 0.0).astype(BF16)
    dt_raw = y[:, D_INNER + CONV_DIM:] + dtb_ref[...]
    dt_all = jnp.where(valid, jnp.maximum(dt_raw, 0.0) + jnp.log(1.0 + jnp.exp(-jnp.abs(dt_raw))), 0.0)

    neg_a = -jnp.exp(alog_ref[...]) * LOG2E
    forward = lax.broadcasted_iota(jnp.int32, (1, LANES), 1) < SSD_HEADS
    r = lax.broadcasted_iota(jnp.int32, (CHUNK, CHUNK), 0)
    c = lax.broadcasted_iota(jnp.int32, (CHUNK, CHUNK), 1)
    tri_f = (c <= r).astype(BF16)
    tri_b = (c >= r).astype(BF16)
    for ch in range(tm // CHUNK):
        rows = slice(ch * CHUNK, (ch + 1) * CHUNK)
        dt = dt_all[rows]
        rest = dt * neg_a
        cs_f = jnp.zeros((CHUNK, LANES), F32)
        cs_b = jnp.zeros((CHUNK, LANES), F32)
        for _ in range(3):
            part = rest.astype(BF16)
            cs_f = cs_f + jnp.dot(tri_f, part, preferred_element_type=F32)
            cs_b = cs_b + jnp.dot(tri_b, part, preferred_element_type=F32)
            rest = rest - part.astype(F32)
        cs = jnp.where(forward, cs_f, cs_b)
        total = jnp.where(forward, cs_f[CHUNK - 1:CHUNK, :], cs_b[0:1, :])
        cs_ref[0, rows, :] = cs
        srct_ref[0, :, rows] = (cs - jnp.log2(dt)).T
        outwt_ref[0, :, rows] = (jnp.exp2(total - cs) * dt).T
        etot_ref[0, ch * SUBLANES:(ch + 1) * SUBLANES, :] = jnp.broadcast_to(jnp.exp2(total), (SUBLANES, LANES))


def _ssd_inproj(h, nw, w, dtb, a_log):
    B, Tp, _ = h.shape
    tm = _pick_tile(Tp, (384, 128))
    row = lambda b, i: (b, i, 0)
    col = lambda b, i: (b, 0, i)
    per_chunk = SUBLANES * tm // CHUNK
    return pl.pallas_call(
        _ssd_inproj_kernel,
        grid=(B, Tp // tm),
        in_specs=[pl.BlockSpec((1, tm, D_MODEL), row)] + [_const_spec(a.shape) for a in (nw, w, dtb, a_log)],
        out_specs=[pl.BlockSpec((1, tm, D_INNER), row), pl.BlockSpec((1, tm, CONV_DIM), row),
                   pl.BlockSpec((1, tm, LANES), row), pl.BlockSpec((1, LANES, tm), col),
                   pl.BlockSpec((1, LANES, tm), col), pl.BlockSpec((1, per_chunk, LANES), row)],
        out_shape=[jax.ShapeDtypeStruct((B, Tp, D_INNER), BF16),
                   jax.ShapeDtypeStruct((B, Tp, CONV_DIM), BF16),
                   jax.ShapeDtypeStruct((B, Tp, LANES), F32),
                   jax.ShapeDtypeStruct((B, LANES, Tp), F32),
                   jax.ShapeDtypeStruct((B, LANES, Tp), F32),
                   jax.ShapeDtypeStruct((B, SUBLANES * Tp // CHUNK, LANES), F32)],
        compiler_params=_params(("parallel", "parallel")),
        name="ssd_inproj",
    )(h, nw, w, dtb, a_log)


CONV_MARGIN = 16
CONV_ROWS = 128
CONV_SHIFTS = tuple(t - D_CONV // 2 for t in range(D_CONV) if t != D_CONV // 2)


def _conv_kernel(x_ref, w_ref, b_ref, o_ref, xs_ref):
    Tp, cw = x_ref.shape[1], x_ref.shape[2]
    win = CONV_ROWS + 2 * CONV_MARGIN
    zeros = jnp.zeros((CONV_MARGIN, cw), BF16)
    xs_ref[0:CONV_MARGIN, :] = zeros
    xs_ref[CONV_MARGIN + Tp:, :] = zeros
    xs_ref[CONV_MARGIN:CONV_MARGIN + Tp, :] = x_ref[0]
    r = lax.broadcasted_iota(jnp.int32, (len(CONV_SHIFTS) * CONV_ROWS, win), 0)
    j = lax.broadcasted_iota(jnp.int32, (len(CONV_SHIFTS) * CONV_ROWS, win), 1)
    select = jnp.zeros(r.shape, F32)
    for k, shift in enumerate(CONV_SHIFTS):
        hit = (r >= k * CONV_ROWS) & (r < (k + 1) * CONV_ROWS) & (j == r - k * CONV_ROWS + CONV_MARGIN + shift)
        select = jnp.where(hit, 1.0, select)
    select = select.astype(BF16)
    w = [jnp.broadcast_to(w_ref[t:t + 1, :], (CONV_ROWS, cw)) for t in range(D_CONV)]
    b = jnp.broadcast_to(b_ref[...], (CONV_ROWS, cw))
    centre = D_CONV // 2

    def body(i, carry):
        r0 = pl.multiple_of(i * CONV_ROWS, CONV_ROWS)
        shifted = jnp.dot(select, xs_ref[pl.ds(r0, win), :], preferred_element_type=F32)
        acc = b + w[centre] * xs_ref[pl.ds(r0 + CONV_MARGIN, CONV_ROWS), :].astype(F32)
        for k, shift in enumerate(CONV_SHIFTS):
            acc = acc + w[centre + shift] * shifted[k * CONV_ROWS:(k + 1) * CONV_ROWS, :]
        o_ref[0, pl.ds(r0, CONV_ROWS), :] = _silu(acc).astype(BF16)
        return carry

    n_blocks = Tp // CONV_ROWS
    lax.fori_loop(0, n_blocks, body, 0, unroll=next(u for u in (11, 3, 1) if n_blocks % u == 0))


def _ssd_conv(xbc, w, b):
    B, Tp, C = xbc.shape
    cw = 512
    blk = pl.BlockSpec((1, Tp, cw), lambda bi, ci: (bi, 0, ci))
    return pl.pallas_call(
        _conv_kernel,
        grid=(B, C // cw),
        in_specs=[blk, pl.BlockSpec((D_CONV, cw), lambda bi, ci: (0, ci)),
                  pl.BlockSpec((1, cw), lambda bi, ci: (0, ci))],
        out_specs=blk,
        out_shape=jax.ShapeDtypeStruct((B, Tp, C), BF16),
        scratch_shapes=[pltpu.VMEM((Tp + 2 * CONV_MARGIN, cw), BF16)],
        compiler_params=_params(("parallel", "parallel")),
        name="ssd_conv",
    )(xbc, w, b)


def _ssd_decays(cs_ref, srct_ref, outwt_ref, etot_ref, k, reverse):
    row = lax.broadcasted_iota(jnp.int32, (CHUNK, CHUNK), 0)
    col = lax.broadcasted_iota(jnp.int32, (CHUNK, CHUNK), 1)
    causal = (col >= row) if reverse else (col <= row)
    span = slice(k * CHUNK, (k + 1) * CHUNK)
    return dict(
        hide=jnp.where(causal, 0.0, -jnp.inf),
        cs=cs_ref[0, span, :], src_t=srct_ref[0, :, span], out_w_t=outwt_ref[0, :, span],
        e_tot=etot_ref[0, k * SUBLANES:k * SUBLANES + 1, :],
        rows=span, head0=SSD_HEADS if reverse else 0)


def _ssd_group(b_ref, c_ref, g, rows):
    bg = b_ref[0, rows, g * D_STATE:(g + 1) * D_STATE]
    cg = c_ref[0, rows, g * D_STATE:(g + 1) * D_STATE]
    gram = lax.dot_general(cg, bg, (((1,), (1,)), ((), ())), preferred_element_type=F32)
    return gram, cg.astype(F32), bg.astype(F32).T


def _ssd_pair(dec, grp, x_ref, y_ref, st_ref, pair, low):
    gram, cg_f, bg_t = grp
    rows = dec['rows']
    lanes = slice(pair * LANES, (pair + 1) * LANES)
    xp = x_ref[0, rows, lanes]
    state = st_ref[pair]
    rhs = jnp.concatenate([xp, state.astype(BF16)], axis=0)
    hd0 = dec['head0'] + 2 * pair
    ys, ups = [], []
    for hd in (hd0, hd0 + 1):
        cs_l = jnp.broadcast_to(dec['cs'][:, hd:hd + 1], (CHUNK, CHUNK))
        seg = cs_l - dec['src_t'][hd:hd + 1, :]
        within = gram * jnp.exp2(seg + dec['hide'])
        from_state = cg_f * jnp.exp2(cs_l)
        lhs = jnp.concatenate([within, from_state], axis=1).astype(BF16)
        ys.append(jnp.dot(lhs, rhs, preferred_element_type=F32))
        ups.append(jnp.dot((bg_t * dec['out_w_t'][hd:hd + 1, :]).astype(BF16), xp, preferred_element_type=F32))
    y_ref[0, rows, lanes] = jnp.where(low, ys[0], ys[1]).astype(y_ref.dtype)
    e_tot = dec['e_tot']
    keep = jnp.where(low[0:1, :], jnp.broadcast_to(e_tot[:, hd0:hd0 + 1], (1, LANES)),
                     jnp.broadcast_to(e_tot[:, hd0 + 1:hd0 + 2], (1, LANES)))
    st_ref[pair] = state * keep + jnp.where(low, ups[0], ups[1])


def _ssd_scan_kernel(xf_ref, bf_ref, cf_ref, csf_ref, srcf_ref, outf_ref, etotf_ref,
                     xb_ref, bb_ref, cb_ref, csb_ref, srcb_ref, outb_ref, etotb_ref,
                     yf_ref, yb_ref, stf_ref, stb_ref):
    @pl.when(pl.program_id(1) == 0)
    def _():
        stf_ref[...] = jnp.zeros(stf_ref.shape, F32)
        stb_ref[...] = jnp.zeros(stb_ref.shape, F32)

    low = lax.broadcasted_iota(jnp.int32, (CHUNK, LANES), 1) < SSD_HEAD_DIM
    per_step = xf_ref.shape[1] // CHUNK
    pairs_per_group = HEADS_PER_GROUP // 2
    for k in range(per_step):
        dec_f = _ssd_decays(csf_ref, srcf_ref, outf_ref, etotf_ref, k, reverse=False)
        dec_b = _ssd_decays(csb_ref, srcb_ref, outb_ref, etotb_ref, per_step - 1 - k, reverse=True)
        for g in range(SSD_GROUPS):
            grp_f = _ssd_group(bf_ref, cf_ref, g, dec_f['rows'])
            grp_b = _ssd_group(bb_ref, cb_ref, g, dec_b['rows'])
            for pr in range(pairs_per_group):
                pair = g * pairs_per_group + pr
                _ssd_pair(dec_f, grp_f, xf_ref, yf_ref, stf_ref, pair, low)
                _ssd_pair(dec_b, grp_b, xb_ref, yb_ref, stb_ref, pair, low)


def _ssd_scan(xbc, cs, src_t, out_w_t, e_tot):
    B, Tp, _ = xbc.shape
    nc = Tp // CHUNK
    per_step = 3 if nc % 3 == 0 else 1
    nb = nc // per_step
    rows = per_step * CHUNK
    bc_w = SSD_GROUPS * D_STATE
    b_blk = D_INNER // bc_w

    def specs(pos):
        return [pl.BlockSpec((1, rows, D_INNER), lambda b, c: (b, pos(c), 0)),
                pl.BlockSpec((1, rows, bc_w), lambda b, c: (b, pos(c), b_blk)),
                pl.BlockSpec((1, rows, bc_w), lambda b, c: (b, pos(c), b_blk + 1)),
                pl.BlockSpec((1, rows, LANES), lambda b, c: (b, pos(c), 0)),
                pl.BlockSpec((1, LANES, rows), lambda b, c: (b, 0, pos(c))),
                pl.BlockSpec((1, LANES, rows), lambda b, c: (b, 0, pos(c))),
                pl.BlockSpec((1, per_step * SUBLANES, LANES), lambda b, c: (b, pos(c), 0))]

    fwd = lambda c: c
    bwd = lambda c: nb - 1 - c
    y_shape = jax.ShapeDtypeStruct((B, Tp, D_INNER), BF16)
    state = pltpu.VMEM((SSD_HEADS // 2, D_STATE, LANES), F32)
    return pl.pallas_call(
        _ssd_scan_kernel,
        grid=(B, nb),
        in_specs=specs(fwd) + specs(bwd),
        out_specs=[pl.BlockSpec((1, rows, D_INNER), lambda b, c: (b, fwd(c), 0)),
                   pl.BlockSpec((1, rows, D_INNER), lambda b, c: (b, bwd(c), 0))],
        out_shape=[y_shape, y_shape],
        scratch_shapes=[state, state],
        compiler_params=_params(("parallel", "arbitrary")),
        name="ssd_scan",
    )(*(2 * (xbc, xbc, xbc, cs, src_t, out_w_t, e_tot)))


def _ssd_out_kernel(yf_ref, yb_ref, x_ref, z_ref, d_ref, nw_ref, w_ref, h_ref, o_ref):
    y = yf_ref[...].astype(F32) + yb_ref[...].astype(F32) + d_ref[...] * x_ref[...].astype(F32)
    y = y * z_ref[...].astype(F32)
    r = lax.rsqrt(jnp.mean(y * y, axis=-1, keepdims=True) + EPS)
    proj = jnp.dot((y * nw_ref[...]).astype(BF16), w_ref[...], preferred_element_type=F32)
    o_ref[...] = h_ref[...] + r * proj


def _ssd_out(yf, yb, xbc, z, d_e, nw, w, h):
    R = h.shape[0]
    tm = _pick_tile(R, (512, 384, 256, 128))
    wide = pl.BlockSpec((tm, D_INNER), lambda i: (i, 0))
    row = pl.BlockSpec((tm, D_MODEL), lambda i: (i, 0))
    return pl.pallas_call(
        _ssd_out_kernel,
        grid=(R // tm,),
        in_specs=[wide, wide, wide, wide] + [_const_spec(a.shape) for a in (d_e, nw, w)] + [row],
        out_specs=row,
        out_shape=jax.ShapeDtypeStruct((R, D_MODEL), F32),
        compiler_params=_params(("parallel",)),
        name="ssd_out",
    )(yf, yb, xbc, z, d_e, nw, w, h)


def _row(v, width=None):
    v = v.astype(F32).reshape(1, -1)
    if width is not None and v.shape[1] < width:
        v = jnp.pad(v, ((0, 0), (0, width - v.shape[1])))
    return v


def _prep_mla(w_in, q_norm, kv_norm, w_q_up, w_kv_up, q_head_norm, k_head_norm, w_out):
    lat = Q_LORA_RANK + KV_LORA_RANK
    rope_cols = jnp.pad(w_in[:, lat:], ((0, 0), (ROPE_LANE0, HEAD_PAD - QK_HEAD_DIM)))
    w_in_p = jnp.concatenate([w_in[:, :lat], rope_cols], axis=1).astype(BF16)
    w_q = w_q_up.reshape(Q_LORA_RANK, MLA_HEADS, QK_HEAD_DIM)
    w_q = jnp.pad(w_q, ((0, 0), (0, 0), (0, HEAD_PAD - QK_HEAD_DIM))).reshape(Q_LORA_RANK, -1).astype(BF16)
    w_kv = w_kv_up.reshape(KV_LORA_RANK, MLA_HEADS, QK_NOPE_DIM + V_HEAD_DIM)
    w_k = jnp.pad(w_kv[:, :, :QK_NOPE_DIM], ((0, 0), (0, 0), (0, HEAD_PAD - QK_NOPE_DIM)))
    w_k = w_k.reshape(KV_LORA_RANK, -1).astype(BF16)
    w_v = w_kv[:, :, QK_NOPE_DIM:].reshape(KV_LORA_RANK, -1).T.astype(BF16)
    scale = LOG2E / math.sqrt(QK_HEAD_DIM)
    gq = q_head_norm.astype(F32)
    gk = k_head_norm.astype(F32)
    bound = 1.02 * scale * QK_HEAD_DIM * jnp.max(jnp.abs(gq)) * jnp.max(jnp.abs(gk))
    return dict(
        w_in=w_in_p, q_norm=_row(q_norm), kv_norm=_row(kv_norm), w_q=w_q, w_k=w_k, w_v=w_v,
        bound=bound.reshape(1),
        gq=_row(gq * scale, HEAD_PAD),
        gkn=_row(gk[:QK_NOPE_DIM], HEAD_PAD),
        gkr=_row(jnp.pad(gk[QK_NOPE_DIM:], (ROPE_LANE0, 0)), HEAD_PAD),
        w_out=w_out.astype(BF16))


def _attention_tables(Tp):
    half = QK_ROPE_DIM // 2
    pos = jnp.arange(Tp, dtype=F32) - FRONT_PAD
    inv_freq = ROPE_BASE ** (-jnp.arange(half, dtype=F32) / half)
    ang = pos[:, None] * inv_freq[None, :]
    cos, sin = jnp.cos(ang), jnp.sin(ang)
    tail = HEAD_PAD - QK_HEAD_DIM
    cos_t = jnp.concatenate([jnp.ones((Tp, ROPE_LANE0), F32), cos, cos, jnp.ones((Tp, tail), F32)], axis=1)
    sin_t = jnp.concatenate([jnp.zeros((Tp, ROPE_LANE0), F32), -sin, sin, jnp.zeros((Tp, tail), F32)], axis=1)
    is_pad = (jnp.arange(Tp) < FRONT_PAD)[:, None]
    lane = jnp.arange(HEAD_PAD)[None, :]
    one = (lane == MASK_LANE).astype(F32)
    return dict(cos=cos_t, sin=sin_t, pad_mask=jnp.where(is_pad, MASK_VALUE, 0.0).astype(F32) * one, one=one)


def _key_bias(tables, bound):
    return jnp.where(tables['pad_mask'] < 0, tables['pad_mask'], -bound * tables['one'])


def _prep_ssd(w_in, conv_w, conv_b, dt_bias, a_log, d_skip, norm_w, w_out):
    w_in_p = jnp.pad(w_in, ((0, 0), (0, LANES - 2 * SSD_HEADS))).astype(BF16)
    return dict(
        w_in=w_in_p, conv_w=conv_w.astype(F32), conv_b=_row(conv_b),
        dt_bias=_row(dt_bias, LANES), a_log=_row(a_log, LANES),
        d_e=_row(jnp.repeat(d_skip.astype(F32), SSD_HEAD_DIM)),
        norm=_row(norm_w), w_out=w_out.astype(BF16))


def kernel(x_prompt, x_sample, meta_tokens, mix_norm, ffn_norm, mla_w_in, mla_q_norm, mla_kv_norm, mla_w_q_up, mla_w_kv_up, mla_q_head_norm, mla_k_head_norm, mla_w_out, ssd_w_in, ssd_conv_w, ssd_conv_b, ssd_dt_bias, ssd_a_log, ssd_d, ssd_norm, ssd_w_out, ffn_w_gate, ffn_w_up, ffn_w_down, final_norm):
    assert x_prompt.shape[1:] == x_sample.shape[1:] and x_prompt.shape[2] == D_MODEL
    assert x_prompt.shape[1] % CHUNK == 0
    depth = mix_norm.shape[0]
    n_prompt = x_prompt.shape[0]
    B = n_prompt + x_sample.shape[0]
    S = x_prompt.shape[1]
    Tp = FRONT_PAD + N_META + S
    head = jnp.concatenate([jnp.zeros((FRONT_PAD, D_MODEL), F32), meta_tokens.astype(F32)], axis=0)
    h = jnp.concatenate([jnp.broadcast_to(head[None], (B, CHUNK, D_MODEL)),
                         jnp.concatenate([x_prompt, x_sample], axis=0)], axis=1)
    tables = _attention_tables(Tp)
    attn_p = (mla_w_in, mla_q_norm, mla_kv_norm, mla_w_q_up, mla_w_kv_up,
              mla_q_head_norm, mla_k_head_norm, mla_w_out)
    ssd_p = (ssd_w_in, ssd_conv_w, ssd_conv_b, ssd_dt_bias, ssd_a_log, ssd_d, ssd_norm, ssd_w_out)
    flat = lambda t: t.reshape(B * Tp, t.shape[-1])
    ia = ib = 0
    for layer in range(depth):
        nw = _row(mix_norm[layer])
        if layer % 2 == 0:
            p = _prep_mla(*[t[ia] for t in attn_p])
            ia += 1
            q, k, v = _mla_proj(h, nw, p, tables)
            o = _flash(p['bound'], q, k, v, p, tables)
            h2 = flat(_attn_out(o, p['w_out'], h))
        else:
            p = _prep_ssd(*[t[ib] for t in ssd_p])
            ib += 1
            z, xbc, *decays = _ssd_inproj(h, nw, p['w_in'], p['dt_bias'], p['a_log'])
            xbc = _ssd_conv(xbc, p['conv_w'], p['conv_b'])
            yf, yb = _ssd_scan(xbc, *decays)
            h2 = _ssd_out(flat(yf), flat(yb), flat(xbc), flat(z), p['d_e'], p['norm'], p['w_out'], flat(h))
        ffn_args = (_row(ffn_norm[layer]), ffn_w_gate[layer].astype(BF16), ffn_w_up[layer].astype(BF16),
                    ffn_w_down[layer].astype(BF16), _row(final_norm))
        if layer < depth - 1:
            h = _ffn(h2, *ffn_args).reshape(B, Tp, D_MODEL)
    h = h2.reshape(B, Tp, D_MODEL)
    return (_ffn_final(h, 0, n_prompt, *ffn_args), _ffn_final(h, n_prompt, B - n_prompt, *ffn_args))
```

```python
import functools
import math

import jax
import jax.numpy as jnp
from jax import lax
from jax.experimental import pallas as pl
from jax.experimental.pallas import tpu as pltpu

F32 = jnp.float32
BF16 = jnp.bfloat16

D_MODEL = 1024
N_META = 16
EPS = 1e-6
MLA_HEADS = 16
QK_NOPE_DIM = 64
QK_ROPE_DIM = 32
QK_HEAD_DIM = QK_NOPE_DIM + QK_ROPE_DIM
V_HEAD_DIM = 64
Q_LORA_RANK = 384
KV_LORA_RANK = 256
ROPE_BASE = 10000.0
D_INNER = 2 * D_MODEL
SSD_HEAD_DIM = 64
SSD_HEADS = D_INNER // SSD_HEAD_DIM
SSD_GROUPS = 4
HEADS_PER_GROUP = SSD_HEADS // SSD_GROUPS
D_STATE = 128
D_CONV = 5
CONV_DIM = D_INNER + 2 * SSD_GROUPS * D_STATE
CHUNK = 128
FFN_HIDDEN = -(-8 * D_MODEL // (3 * 256)) * 256

LANES = 128
FRONT_PAD = CHUNK - N_META
HEAD_PAD = LANES
ROPE_LANE0 = QK_NOPE_DIM
MASK_LANE = QK_HEAD_DIM
MASK_VALUE = -1e30
LOG2E = 1.0 / math.log(2.0)
SOFTMAX_BOUND_LIMIT = 40.0
HEAD_GROUP = 4
VMEM_LIMIT = 56 * 1024 * 1024


def _pick_tile(n, candidates):
    for c in candidates:
        if n % c == 0:
            return c
    return n


def _params(sem, vmem=VMEM_LIMIT):
    return pltpu.CompilerParams(dimension_semantics=sem, vmem_limit_bytes=vmem)


def _const_spec(shape):
    nd = len(shape)
    return pl.BlockSpec(shape, lambda *_: (0,) * nd, pipeline_mode=pl.Buffered(1))


def _rms(x, w):
    return x * lax.rsqrt(jnp.mean(x * x, axis=-1, keepdims=True) + EPS) * w


def _silu(x):
    half = 0.5 * x
    return half + half * jnp.tanh(half)


def _rope(x, cos, sin):
    lane = lax.broadcasted_iota(jnp.int32, x.shape, 1)
    half = QK_ROPE_DIM // 2
    first_half = (lane >= ROPE_LANE0) & (lane < ROPE_LANE0 + half)
    swapped = jnp.where(first_half, pltpu.roll(x, HEAD_PAD - half, 1), pltpu.roll(x, half, 1))
    return x * cos + swapped * sin


def _mla_proj_kernel(h_ref, nw_ref, win_ref, qn_ref, kvn_ref, wq_ref, wk_ref, wv_ref,
                     gkn_ref, gkr_ref, cos_ref, sin_ref, bias_ref, q_ref, k_ref, v_ref):
    u = _rms(h_ref[0], nw_ref[...]).astype(BF16)
    c = jnp.dot(u, win_ref[...], preferred_element_type=F32)
    cq = _rms(c[:, :Q_LORA_RANK], qn_ref[...]).astype(BF16)
    ckv = _rms(c[:, Q_LORA_RANK:Q_LORA_RANK + KV_LORA_RANK], kvn_ref[...]).astype(BF16)
    k_rope = c[:, Q_LORA_RANK + KV_LORA_RANK:]
    q_ref[0] = jnp.dot(cq, wq_ref[...], preferred_element_type=F32)
    k_nope = jnp.dot(ckv, wk_ref[...], preferred_element_type=F32)
    v_ref[0] = lax.dot_general(wv_ref[...], ckv, (((1,), (1,)), ((), ())),
                               preferred_element_type=F32).astype(BF16)

    cos = cos_ref[...]
    sin = sin_ref[...]
    inv_d = 1.0 / QK_HEAD_DIM
    kr = _rope(k_rope * gkr_ref[...], cos, sin)
    kr_ss = jnp.sum(k_rope * k_rope, axis=-1, keepdims=True)
    gkn = gkn_ref[...]
    bias = bias_ref[...]
    for hd in range(MLA_HEADS):
        kh = k_nope[:, hd * HEAD_PAD:(hd + 1) * HEAD_PAD]
        rk = lax.rsqrt((jnp.sum(kh * kh, axis=-1, keepdims=True) + kr_ss) * inv_d + EPS)
        k_ref[0, hd] = ((kh * gkn + kr) * rk + bias).astype(BF16)


def _mla_proj(h, nw, p, tables):
    B, Tp, _ = h.shape
    tm = _pick_tile(Tp, (384, 128))
    row = lambda b, i: (b, i, 0)
    tab = lambda b, i: (i, 0)
    wide = MLA_HEADS * HEAD_PAD
    consts = (nw, p['w_in'], p['q_norm'], p['kv_norm'], p['w_q'], p['w_k'], p['w_v'], p['gkn'], p['gkr'])
    return pl.pallas_call(
        _mla_proj_kernel,
        grid=(B, Tp // tm),
        in_specs=[pl.BlockSpec((1, tm, D_MODEL), row)]
        + [_const_spec(a.shape) for a in consts]
        + [pl.BlockSpec((tm, HEAD_PAD), tab)] * 3,
        out_specs=[pl.BlockSpec((1, tm, wide), row),
                   pl.BlockSpec((1, MLA_HEADS, tm, HEAD_PAD), lambda b, i: (b, 0, i, 0)),
                   pl.BlockSpec((1, MLA_HEADS * V_HEAD_DIM, tm), lambda b, i: (b, 0, i))],
        out_shape=[jax.ShapeDtypeStruct((B, Tp, wide), F32),
                   jax.ShapeDtypeStruct((B, MLA_HEADS, Tp, HEAD_PAD), BF16),
                   jax.ShapeDtypeStruct((B, MLA_HEADS * V_HEAD_DIM, Tp), BF16)],
        compiler_params=_params(("parallel", "parallel")),
        name="mla_proj",
    )(h, *consts, tables['cos'], tables['sin'], _key_bias(tables, p['bound']))


def _flash_kernel(bound_ref, q_ref, k_ref, v_ref, cos_ref, sin_ref, gq_ref, one_ref, o_ref, *, key_chunk):
    Tp = k_ref.shape[2]
    tq = q_ref.shape[1]
    kq = lambda k, q: lax.dot_general(k, q, (((1,), (1,)), ((), ())), preferred_element_type=F32)

    chunks = range(0, Tp, key_chunk)

    def queries(j):
        lane0 = j * HEAD_PAD
        qh = q_ref[0, :, pl.ds(lane0 if isinstance(j, int) else pl.multiple_of(lane0, HEAD_PAD), HEAD_PAD)]
        rq = lax.rsqrt(jnp.sum(qh * qh, axis=-1, keepdims=True) * (1.0 / QK_HEAD_DIM) + EPS)
        return (_rope(qh * rq * gq_ref[...], cos_ref[...], sin_ref[...]) + one_ref[...]).astype(BF16)

    def head(j, q, bounded):
        r0 = j * V_HEAD_DIM
        rows = pl.ds(r0 if isinstance(j, int) else pl.multiple_of(r0, V_HEAD_DIM), V_HEAD_DIM)
        scores = lambda c0: kq(k_ref[0, j, c0:c0 + key_chunk, :], q)
        if bounded:
            shift = None
        else:
            shift = jnp.full((1, tq), -jnp.inf, F32)
            for c0 in chunks:
                shift = jnp.maximum(shift, jnp.max(scores(c0), axis=0, keepdims=True))
        l = jnp.zeros((1, tq), F32)
        o = jnp.zeros((V_HEAD_DIM, tq), F32)
        for c0 in chunks:
            s = scores(c0)
            p = jnp.exp2(s if bounded else s - shift)
            l = l + jnp.sum(p, axis=0, keepdims=True)
            o = o + jnp.dot(v_ref[0, rows, c0:c0 + key_chunk], p.astype(BF16), preferred_element_type=F32)
        o_ref[0, rows, :] = (o * (1.0 / l)).astype(BF16)

    bounded = bound_ref[0] <= SOFTMAX_BOUND_LIMIT

    @pl.when(bounded)
    def _():
        qs = [queries(j) for j in range(HEAD_GROUP)]
        for j in range(HEAD_GROUP):
            head(j, qs[j], True)

    @pl.when(jnp.logical_not(bounded))
    def _():
        def body(j, carry):
            head(j, queries(j), False)
            return carry
        lax.fori_loop(0, HEAD_GROUP, body, 0)


def _flash(bound, q_raw, k, v, p, tables):
    B, Tp, _ = q_raw.shape
    tq = _pick_tile(Tp, (1408, 384, 128))
    gw = HEAD_GROUP * V_HEAD_DIM
    tab = pl.BlockSpec((tq, HEAD_PAD), lambda b, g, i: (i, 0))
    return pl.pallas_call(
        functools.partial(_flash_kernel, key_chunk=_pick_tile(Tp, (1408, 384, 128))),
        grid=(B, MLA_HEADS // HEAD_GROUP, Tp // tq),
        in_specs=[pl.BlockSpec(memory_space=pltpu.SMEM),
                  pl.BlockSpec((1, tq, HEAD_GROUP * HEAD_PAD), lambda b, g, i: (b, i, g)),
                  pl.BlockSpec((1, HEAD_GROUP, Tp, HEAD_PAD), lambda b, g, i: (b, g, 0, 0)),
                  pl.BlockSpec((1, gw, Tp), lambda b, g, i: (b, g, 0)),
                  tab, tab, _const_spec(p['gq'].shape), _const_spec(tables['one'].shape)],
        out_specs=pl.BlockSpec((1, gw, tq), lambda b, g, i: (b, g, i)),
        out_shape=jax.ShapeDtypeStruct((B, MLA_HEADS * V_HEAD_DIM, Tp), BF16),
        compiler_params=_params(("parallel", "parallel", "parallel")),
        name="flash",
    )(bound, q_raw, k, v, tables['cos'], tables['sin'], p['gq'], tables['one'])


def _attn_out_kernel(ot_ref, w_ref, h_ref, o_ref):
    proj = lax.dot_general(ot_ref[0], w_ref[...], (((0,), (0,)), ((), ())), preferred_element_type=F32)
    o_ref[0] = h_ref[0] + proj


def _attn_out(o_t, w, h):
    B, Tp, _ = h.shape
    tm = _pick_tile(Tp, (1408, 384, 128))
    row = pl.BlockSpec((1, tm, D_MODEL), lambda b, i: (b, i, 0))
    return pl.pallas_call(
        _attn_out_kernel,
        grid=(B, Tp // tm),
        in_specs=[pl.BlockSpec((1, o_t.shape[1], tm), lambda b, i: (b, 0, i)), _const_spec(w.shape), row],
        out_specs=row,
        out_shape=jax.ShapeDtypeStruct((B, Tp, D_MODEL), F32),
        compiler_params=_params(("parallel", "parallel")),
        name="attn_out",
    )(o_t, w, h)


def _hidden_chunks():
    bounds, start = [], 0
    while start < FFN_HIDDEN:
        stop = min(start + 1536, FFN_HIDDEN)
        bounds.append((start, stop))
        start = stop
    return bounds


def _ffn_kernel(h_ref, nw_ref, wg_ref, wu_ref, wd_ref, fw_ref, o_ref, *, final):
    h = h_ref[...].reshape(-1, D_MODEL)
    u = _rms(h, nw_ref[...]).astype(BF16)
    acc = h
    for lo, hi in _hidden_chunks():
        g = jnp.dot(u, wg_ref[:, lo:hi], preferred_element_type=F32)
        up = jnp.dot(u, wu_ref[:, lo:hi], preferred_element_type=F32)
        a = (_silu(g) * up).astype(BF16)
        acc = acc + jnp.dot(a, wd_ref[lo:hi, :], preferred_element_type=F32)
    if final:
        acc = _rms(acc, fw_ref[...])
    o_ref[...] = acc.reshape(o_ref.shape)


def _ffn(h, nw, wg, wu, wd, fw):
    R = h.shape[0]
    tm = _pick_tile(R, (768, 512, 384, 256, 128))
    row = pl.BlockSpec((tm, D_MODEL), lambda i: (i, 0))
    return pl.pallas_call(
        functools.partial(_ffn_kernel, final=False),
        grid=(R // tm,),
        in_specs=[row] + [_const_spec(a.shape) for a in (nw, wg, wu, wd, fw)],
        out_specs=row,
        out_shape=jax.ShapeDtypeStruct((R, D_MODEL), F32),
        compiler_params=_params(("parallel",)),
        name="ffn",
    )(h, nw, wg, wu, wd, fw)


def _ffn_final(h, first, count, nw, wg, wu, wd, fw):
    _, Tp, _ = h.shape
    S = Tp - CHUNK
    tm = _pick_tile(S, (512, 384, 256, 128))
    return pl.pallas_call(
        functools.partial(_ffn_kernel, final=True),
        grid=(count, S // tm),
        in_specs=[pl.BlockSpec((pl.Element(1), pl.Element(tm), pl.Element(D_MODEL)),
                               lambda b, i: (b + first, (i * (tm // CHUNK) + 1) * CHUNK, 0))]
        + [_const_spec(a.shape) for a in (nw, wg, wu, wd, fw)],
        out_specs=pl.BlockSpec((1, tm, D_MODEL), lambda b, i: (b, i, 0)),
        out_shape=jax.ShapeDtypeStruct((count, S, D_MODEL), F32),
        compiler_params=_params(("parallel", "parallel")),
        name="ffn_final",
    )(h, nw, wg, wu, wd, fw)


SUBLANES = 8


def _ssd_inproj_kernel(h_ref, nw_ref, w_ref, dtb_ref, alog_ref, z_ref, xbc_ref, cs_ref, srct_ref, outwt_ref,
                       etot_ref):
    tm = h_ref.shape[1]
    u = _rms(h_ref[0], nw_ref[...]).astype(BF16)
    y = jnp.dot(u, w_ref[...], preferred_element_type=F32)
    row = pl.program_id(1) * tm + lax.broadcasted_iota(jnp.int32, (tm, 1), 0)
    valid = row >= FRONT_PAD
    z_ref[0] = _silu(y[:, :D_INNER]).astype(BF16)
    xbc_ref[0] = jnp.where(valid, y[:, D_INNER:D_INNER + CONV_DIM], 0.0).astype(BF16)
    dt_raw = y[:, D_INNER + CONV_DIM:] + dtb_ref[...]
    dt_all = jnp.where(valid, jnp.maximum(dt_raw, 0.0) + jnp.log(1.0 + jnp.exp(-jnp.abs(dt_raw))), 0.0)

    neg_a = -jnp.exp(alog_ref[...]) * LOG2E
    forward = lax.broadcasted_iota(jnp.int32, (1, LANES), 1) < SSD_HEADS
    r = lax.broadcasted_iota(jnp.int32, (CHUNK, CHUNK), 0)
    c = lax.broadcasted_iota(jnp.int32, (CHUNK, CHUNK), 1)
    tri_f = (c <= r).astype(BF16)
    tri_b = (c >= r).astype(BF16)
    for ch in range(tm // CHUNK):
        rows = slice(ch * CHUNK, (ch + 1) * CHUNK)
        dt = dt_all[rows]
        rest = dt * neg_a
        cs_f = jnp.zeros((CHUNK, LANES), F32)
        cs_b = jnp.zeros((CHUNK, LANES), F32)
        for _ in range(3):
            part = rest.astype(BF16)
            cs_f = cs_f + jnp.dot(tri_f, part, preferred_element_type=F32)
            cs_b = cs_b + jnp.dot(tri_b, part, preferred_element_type=F32)
            rest = rest - part.astype(F32)
        cs = jnp.where(forward, cs_f, cs_b)
        total = jnp.where(forward, cs_f[CHUNK - 1:CHUNK, :], cs_b[0:1, :])
        cs_ref[0, rows, :] = cs
        srct_ref[0, :, rows] = (cs - jnp.log2(dt)).T
        outwt_ref[0, :, rows] = (jnp.exp2(total - cs) * dt).T
        etot_ref[0, ch * SUBLANES:(ch + 1) * SUBLANES, :] = jnp.broadcast_to(jnp.exp2(total), (SUBLANES, LANES))


def _ssd_inproj(h, nw, w, dtb, a_log):
    B, Tp, _ = h.shape
    tm = _pick_tile(Tp, (384, 128))
    row = lambda b, i: (b, i, 0)
    col = lambda b, i: (b, 0, i)
    per_chunk = SUBLANES * tm // CHUNK
    return pl.pallas_call(
        _ssd_inproj_kernel,
        grid=(B, Tp // tm),
        in_specs=[pl.BlockSpec((1, tm, D_MODEL), row)] + [_const_spec(a.shape) for a in (nw, w, dtb, a_log)],
        out_specs=[pl.BlockSpec((1, tm, D_INNER), row), pl.BlockSpec((1, tm, CONV_DIM), row),
                   pl.BlockSpec((1, tm, LANES), row), pl.BlockSpec((1, LANES, tm), col),
                   pl.BlockSpec((1, LANES, tm), col), pl.BlockSpec((1, per_chunk, LANES), row)],
        out_shape=[jax.ShapeDtypeStruct((B, Tp, D_INNER), BF16),
                   jax.ShapeDtypeStruct((B, Tp, CONV_DIM), BF16),
                   jax.ShapeDtypeStruct((B, Tp, LANES), F32),
                   jax.ShapeDtypeStruct((B, LANES, Tp), F32),
                   jax.ShapeDtypeStruct((B, LANES, Tp), F32),
                   jax.ShapeDtypeStruct((B, SUBLANES * Tp // CHUNK, LANES), F32)],
        compiler_params=_params(("parallel", "parallel")),
        name="ssd_inproj",
    )(h, nw, w, dtb, a_log)


CONV_MARGIN = 16
CONV_ROWS = 128
CONV_SHIFTS = tuple(t - D_CONV // 2 for t in range(D_CONV) if t != D_CONV // 2)


def _conv_kernel(x_ref, w_ref, b_ref, o_ref, xs_ref):
    Tp, cw = x_ref.shape[1], x_ref.shape[2]
    win = CONV_ROWS + 2 * CONV_MARGIN
    zeros = jnp.zeros((CONV_MARGIN, cw), BF16)
    xs_ref[0:CONV_MARGIN, :] = zeros
    xs_ref[CONV_MARGIN + Tp:, :] = zeros
    xs_ref[CONV_MARGIN:CONV_MARGIN + Tp, :] = x_ref[0]
    r = lax.broadcasted_iota(jnp.int32, (len(CONV_SHIFTS) * CONV_ROWS, win), 0)
    j = lax.broadcasted_iota(jnp.int32, (len(CONV_SHIFTS) * CONV_ROWS, win), 1)
    select = jnp.zeros(r.shape, F32)
    for k, shift in enumerate(CONV_SHIFTS):
        hit = (r >= k * CONV_ROWS) & (r < (k + 1) * CONV_ROWS) & (j == r - k * CONV_ROWS + CONV_MARGIN + shift)
        select = jnp.where(hit, 1.0, select)
    select = select.astype(BF16)
    w = [jnp.broadcast_to(w_ref[t:t + 1, :], (CONV_ROWS, cw)) for t in range(D_CONV)]
    b = jnp.broadcast_to(b_ref[...], (CONV_ROWS, cw))
    centre = D_CONV // 2

    def body(i, carry):
        r0 = pl.multiple_of(i * CONV_ROWS, CONV_ROWS)
        shifted = jnp.dot(select, xs_ref[pl.ds(r0, win), :], preferred_element_type=F32)
        acc = b + w[centre] * xs_ref[pl.ds(r0 + CONV_MARGIN, CONV_ROWS), :].astype(F32)
        for k, shift in enumerate(CONV_SHIFTS):
            acc = acc + w[centre + shift] * shifted[k * CONV_ROWS:(k + 1) * CONV_ROWS, :]
        o_ref[0, pl.ds(r0, CONV_ROWS), :] = _silu(acc).astype(BF16)
        return carry

    n_blocks = Tp // CONV_ROWS
    lax.fori_loop(0, n_blocks, body, 0, unroll=next(u for u in (33, 11, 3, 1) if n_blocks % u == 0))


def _ssd_conv(xbc, w, b):
    B, Tp, C = xbc.shape
    cw = 512
    blk = pl.BlockSpec((1, Tp, cw), lambda bi, ci: (bi, 0, ci))
    return pl.pallas_call(
        _conv_kernel,
        grid=(B, C // cw),
        in_specs=[blk, pl.BlockSpec((D_CONV, cw), lambda bi, ci: (0, ci)),
                  pl.BlockSpec((1, cw), lambda bi, ci: (0, ci))],
        out_specs=blk,
        out_shape=jax.ShapeDtypeStruct((B, Tp, C), BF16),
        scratch_shapes=[pltpu.VMEM((Tp + 2 * CONV_MARGIN, cw), BF16)],
        compiler_params=_params(("parallel", "parallel")),
        name="ssd_conv",
    )(xbc, w, b)


def _ssd_decays(cs_ref, srct_ref, outwt_ref, etot_ref, k, reverse):
    row = lax.broadcasted_iota(jnp.int32, (CHUNK, CHUNK), 0)
    col = lax.broadcasted_iota(jnp.int32, (CHUNK, CHUNK), 1)
    causal = (col >= row) if reverse else (col <= row)
    span = slice(k * CHUNK, (k + 1) * CHUNK)
    return dict(
        hide=jnp.where(causal, 0.0, -jnp.inf),
        cs=cs_ref[0, span, :], src_t=srct_ref[0, :, span], out_w_t=outwt_ref[0, :, span],
        e_tot=etot_ref[0, k * SUBLANES:k * SUBLANES + 1, :],
        rows=span, head0=SSD_HEADS if reverse else 0)


def _ssd_group(b_ref, c_ref, g, rows):
    bg = b_ref[0, rows, g * D_STATE:(g + 1) * D_STATE]
    cg = c_ref[0, rows, g * D_STATE:(g + 1) * D_STATE]
    gram = lax.dot_general(cg, bg, (((1,), (1,)), ((), ())), preferred_element_type=F32)
    return gram, cg.astype(F32), bg.astype(F32).T


def _ssd_pair(dec, grp, x_ref, y_ref, st_ref, pair, low):
    gram, cg_f, bg_t = grp
    rows = dec['rows']
    lanes = slice(pair * LANES, (pair + 1) * LANES)
    xp = x_ref[0, rows, lanes]
    state = st_ref[pair]
    rhs = jnp.concatenate([xp, state.astype(BF16)], axis=0)
    hd0 = dec['head0'] + 2 * pair
    ys, ups = [], []
    for hd in (hd0, hd0 + 1):
        cs_l = jnp.broadcast_to(dec['cs'][:, hd:hd + 1], (CHUNK, CHUNK))
        seg = cs_l - dec['src_t'][hd:hd + 1, :]
        within = gram * jnp.exp2(seg + dec['hide'])
        from_state = cg_f * jnp.exp2(cs_l)
        lhs = jnp.concatenate([within, from_state], axis=1).astype(BF16)
        ys.append(jnp.dot(lhs, rhs, preferred_element_type=F32))
        ups.append(jnp.dot((bg_t * dec['out_w_t'][hd:hd + 1, :]).astype(BF16), xp, preferred_element_type=F32))
    y_ref[0, rows, lanes] = jnp.where(low, ys[0], ys[1]).astype(y_ref.dtype)
    e_tot = dec['e_tot']
    keep = jnp.where(low[0:1, :], jnp.broadcast_to(e_tot[:, hd0:hd0 + 1], (1, LANES)),
                     jnp.broadcast_to(e_tot[:, hd0 + 1:hd0 + 2], (1, LANES)))
    st_ref[pair] = state * keep + jnp.where(low, ups[0], ups[1])


def _ssd_scan_kernel(xf_ref, bf_ref, cf_ref, csf_ref, srcf_ref, outf_ref, etotf_ref,
                     xb_ref, bb_ref, cb_ref, csb_ref, srcb_ref, outb_ref, etotb_ref,
                     yf_ref, yb_ref, stf_ref, stb_ref):
    @pl.when(pl.program_id(1) == 0)
    def _():
        stf_ref[...] = jnp.zeros(stf_ref.shape, F32)
        stb_ref[...] = jnp.zeros(stb_ref.shape, F32)

    low = lax.broadcasted_iota(jnp.int32, (CHUNK, LANES), 1) < SSD_HEAD_DIM
    per_step = xf_ref.shape[1] // CHUNK
    pairs_per_group = HEADS_PER_GROUP // 2
    for k in range(per_step):
        dec_f = _ssd_decays(csf_ref, srcf_ref, outf_ref, etotf_ref, k, reverse=False)
        dec_b = _ssd_decays(csb_ref, srcb_ref, outb_ref, etotb_ref, per_step - 1 - k, reverse=True)
        for g in range(SSD_GROUPS):
            grp_f = _ssd_group(bf_ref, cf_ref, g, dec_f['rows'])
            grp_b = _ssd_group(bb_ref, cb_ref, g, dec_b['rows'])
            for pr in range(pairs_per_group):
                pair = g * pairs_per_group + pr
                _ssd_pair(dec_f, grp_f, xf_ref, yf_ref, stf_ref, pair, low)
                _ssd_pair(dec_b, grp_b, xb_ref, yb_ref, stb_ref, pair, low)


def _ssd_scan(xbc, cs, src_t, out_w_t, e_tot):
    B, Tp, _ = xbc.shape
    nc = Tp // CHUNK
    per_step = 3 if nc % 3 == 0 else 1
    nb = nc // per_step
    rows = per_step * CHUNK
    bc_w = SSD_GROUPS * D_STATE
    b_blk = D_INNER // bc_w

    def specs(pos):
        return [pl.BlockSpec((1, rows, D_INNER), lambda b, c: (b, pos(c), 0)),
                pl.BlockSpec((1, rows, bc_w), lambda b, c: (b, pos(c), b_blk)),
                pl.BlockSpec((1, rows, bc_w), lambda b, c: (b, pos(c), b_blk + 1)),
                pl.BlockSpec((1, rows, LANES), lambda b, c: (b, pos(c), 0)),
                pl.BlockSpec((1, LANES, rows), lambda b, c: (b, 0, pos(c))),
                pl.BlockSpec((1, LANES, rows), lambda b, c: (b, 0, pos(c))),
                pl.BlockSpec((1, per_step * SUBLANES, LANES), lambda b, c: (b, pos(c), 0))]

    fwd = lambda c: c
    bwd = lambda c: nb - 1 - c
    y_shape = jax.ShapeDtypeStruct((B, Tp, D_INNER), BF16)
    state = pltpu.VMEM((SSD_HEADS // 2, D_STATE, LANES), F32)
    return pl.pallas_call(
        _ssd_scan_kernel,
        grid=(B, nb),
        in_specs=specs(fwd) + specs(bwd),
        out_specs=[pl.BlockSpec((1, rows, D_INNER), lambda b, c: (b, fwd(c), 0)),
                   pl.BlockSpec((1, rows, D_INNER), lambda b, c: (b, bwd(c), 0))],
        out_shape=[y_shape, y_shape],
        scratch_shapes=[state, state],
        compiler_params=_params(("parallel", "arbitrary")),
        name="ssd_scan",
    )(*(2 * (xbc, xbc, xbc, cs, src_t, out_w_t, e_tot)))


def _ssd_out_kernel(yf_ref, yb_ref, x_ref, z_ref, d_ref, nw_ref, w_ref, h_ref, o_ref):
    y = yf_ref[...].astype(F32) + yb_ref[...].astype(F32) + d_ref[...] * x_ref[...].astype(F32)
    y = y * z_ref[...].astype(F32)
    r = lax.rsqrt(jnp.mean(y * y, axis=-1, keepdims=True) + EPS)
    proj = jnp.dot((y * nw_ref[...]).astype(BF16), w_ref[...], preferred_element_type=F32)
    o_ref[...] = h_ref[...] + r * proj


def _ssd_out(yf, yb, xbc, z, d_e, nw, w, h):
    R = h.shape[0]
    tm = _pick_tile(R, (512, 384, 256, 128))
    wide = pl.BlockSpec((tm, D_INNER), lambda i: (i, 0))
    row = pl.BlockSpec((tm, D_MODEL), lambda i: (i, 0))
    return pl.pallas_call(
        _ssd_out_kernel,
        grid=(R // tm,),
        in_specs=[wide, wide, wide, wide] + [_const_spec(a.shape) for a in (d_e, nw, w)] + [row],
        out_specs=row,
        out_shape=jax.ShapeDtypeStruct((R, D_MODEL), F32),
        compiler_params=_params(("parallel",)),
        name="ssd_out",
    )(yf, yb, xbc, z, d_e, nw, w, h)


def _row(v, width=None):
    v = v.astype(F32).reshape(1, -1)
    if width is not None and v.shape[1] < width:
        v = jnp.pad(v, ((0, 0), (0, width - v.shape[1])))
    return v


def _prep_mla(w_in, q_norm, kv_norm, w_q_up, w_kv_up, q_head_norm, k_head_norm, w_out):
    lat = Q_LORA_RANK + KV_LORA_RANK
    rope_cols = jnp.pad(w_in[:, lat:], ((0, 0), (ROPE_LANE0, HEAD_PAD - QK_HEAD_DIM)))
    w_in_p = jnp.concatenate([w_in[:, :lat], rope_cols], axis=1).astype(BF16)
    w_q = w_q_up.reshape(Q_LORA_RANK, MLA_HEADS, QK_HEAD_DIM)
    w_q = jnp.pad(w_q, ((0, 0), (0, 0), (0, HEAD_PAD - QK_HEAD_DIM))).reshape(Q_LORA_RANK, -1).astype(BF16)
    w_kv = w_kv_up.reshape(KV_LORA_RANK, MLA_HEADS, QK_NOPE_DIM + V_HEAD_DIM)
    w_k = jnp.pad(w_kv[:, :, :QK_NOPE_DIM], ((0, 0), (0, 0), (0, HEAD_PAD - QK_NOPE_DIM)))
    w_k = w_k.reshape(KV_LORA_RANK, -1).astype(BF16)
    w_v = w_kv[:, :, QK_NOPE_DIM:].reshape(KV_LORA_RANK, -1).T.astype(BF16)
    scale = LOG2E / math.sqrt(QK_HEAD_DIM)
    gq = q_head_norm.astype(F32)
    gk = k_head_norm.astype(F32)
    bound = 1.02 * scale * QK_HEAD_DIM * jnp.max(jnp.abs(gq)) * jnp.max(jnp.abs(gk))
    return dict(
        w_in=w_in_p, q_norm=_row(q_norm), kv_norm=_row(kv_norm), w_q=w_q, w_k=w_k, w_v=w_v,
        bound=bound.reshape(1),
        gq=_row(gq * scale, HEAD_PAD),
        gkn=_row(gk[:QK_NOPE_DIM], HEAD_PAD),
        gkr=_row(jnp.pad(gk[QK_NOPE_DIM:], (ROPE_LANE0, 0)), HEAD_PAD),
        w_out=w_out.astype(BF16))


def _attention_tables(Tp):
    half = QK_ROPE_DIM // 2
    pos = jnp.arange(Tp, dtype=F32) - FRONT_PAD
    inv_freq = ROPE_BASE ** (-jnp.arange(half, dtype=F32) / half)
    ang = pos[:, None] * inv_freq[None, :]
    cos, sin = jnp.cos(ang), jnp.sin(ang)
    tail = HEAD_PAD - QK_HEAD_DIM
    cos_t = jnp.concatenate([jnp.ones((Tp, ROPE_LANE0), F32), cos, cos, jnp.ones((Tp, tail), F32)], axis=1)
    sin_t = jnp.concatenate([jnp.zeros((Tp, ROPE_LANE0), F32), -sin, sin, jnp.zeros((Tp, tail), F32)], axis=1)
    is_pad = (jnp.arange(Tp) < FRONT_PAD)[:, None]
    lane = jnp.arange(HEAD_PAD)[None, :]
    one = (lane == MASK_LANE).astype(F32)
    return dict(cos=cos_t, sin=sin_t, pad_mask=jnp.where(is_pad, MASK_VALUE, 0.0).astype(F32) * one, one=one)


def _key_bias(tables, bound):
    return jnp.where(tables['pad_mask'] < 0, tables['pad_mask'], -bound * tables['one'])


def _prep_ssd(w_in, conv_w, conv_b, dt_bias, a_log, d_skip, norm_w, w_out):
    w_in_p = jnp.pad(w_in, ((0, 0), (0, LANES - 2 * SSD_HEADS))).astype(BF16)
    return dict(
        w_in=w_in_p, conv_w=conv_w.astype(F32), conv_b=_row(conv_b),
        dt_bias=_row(dt_bias, LANES), a_log=_row(a_log, LANES),
        d_e=_row(jnp.repeat(d_skip.astype(F32), SSD_HEAD_DIM)),
        norm=_row(norm_w), w_out=w_out.astype(BF16))


def kernel(x_prompt, x_sample, meta_tokens, mix_norm, ffn_norm, mla_w_in, mla_q_norm, mla_kv_norm, mla_w_q_up, mla_w_kv_up, mla_q_head_norm, mla_k_head_norm, mla_w_out, ssd_w_in, ssd_conv_w, ssd_conv_b, ssd_dt_bias, ssd_a_log, ssd_d, ssd_norm, ssd_w_out, ffn_w_gate, ffn_w_up, ffn_w_down, final_norm):
    assert x_prompt.shape[1:] == x_sample.shape[1:] and x_prompt.shape[2] == D_MODEL
    assert x_prompt.shape[1] % CHUNK == 0
    depth = mix_norm.shape[0]
    n_prompt = x_prompt.shape[0]
    B = n_prompt + x_sample.shape[0]
    S = x_prompt.shape[1]
    Tp = FRONT_PAD + N_META + S
    head = jnp.concatenate([jnp.zeros((FRONT_PAD, D_MODEL), F32), meta_tokens.astype(F32)], axis=0)
    h = jnp.concatenate([jnp.broadcast_to(head[None], (B, CHUNK, D_MODEL)),
                         jnp.concatenate([x_prompt, x_sample], axis=0)], axis=1)
    tables = _attention_tables(Tp)
    attn_p = (mla_w_in, mla_q_norm, mla_kv_norm, mla_w_q_up, mla_w_kv_up,
              mla_q_head_norm, mla_k_head_norm, mla_w_out)
    ssd_p = (ssd_w_in, ssd_conv_w, ssd_conv_b, ssd_dt_bias, ssd_a_log, ssd_d, ssd_norm, ssd_w_out)
    flat = lambda t: t.reshape(B * Tp, t.shape[-1])
    ia = ib = 0
    for layer in range(depth):
        nw = _row(mix_norm[layer])
        if layer % 2 == 0:
            p = _prep_mla(*[t[ia] for t in attn_p])
            ia += 1
            q, k, v = _mla_proj(h, nw, p, tables)
            o = _flash(p['bound'], q, k, v, p, tables)
            h2 = flat(_attn_out(o, p['w_out'], h))
        else:
            p = _prep_ssd(*[t[ib] for t in ssd_p])
            ib += 1
            z, xbc, *decays = _ssd_inproj(h, nw, p['w_in'], p['dt_bias'], p['a_log'])
            xbc = _ssd_conv(xbc, p['conv_w'], p['conv_b'])
            yf, yb = _ssd_scan(xbc, *decays)
            h2 = _ssd_out(flat(yf), flat(yb), flat(xbc), flat(z), p['d_e'], p['norm'], p['w_out'], flat(h))
        ffn_args = (_row(ffn_norm[layer]), ffn_w_gate[layer].astype(BF16), ffn_w_up[layer].astype(BF16),
                    ffn_w_down[layer].astype(BF16), _row(final_norm))
        if layer < depth - 1:
            h = _ffn(h2, *ffn_args).reshape(B, Tp, D_MODEL)
    h = h2.reshape(B, Tp, D_MODEL)
    return (_ffn_final(h, 0, n_prompt, *ffn_args), _ffn_final(h, n_prompt, B - n_prompt, *ffn_args))
```

```python
import functools
import math

import jax
import jax.numpy as jnp
from jax import lax
from jax.experimental import pallas as pl
from jax.experimental.pallas import tpu as pltpu

F32 = jnp.float32
BF16 = jnp.bfloat16

D_MODEL = 1024
N_META = 16
EPS = 1e-6
MLA_HEADS = 16
QK_NOPE_DIM = 64
QK_ROPE_DIM = 32
QK_HEAD_DIM = QK_NOPE_DIM + QK_ROPE_DIM
V_HEAD_DIM = 64
Q_LORA_RANK = 384
KV_LORA_RANK = 256
ROPE_BASE = 10000.0
D_INNER = 2 * D_MODEL
SSD_HEAD_DIM = 64
SSD_HEADS = D_INNER // SSD_HEAD_DIM
SSD_GROUPS = 4
HEADS_PER_GROUP = SSD_HEADS // SSD_GROUPS
D_STATE = 128
D_CONV = 5
CONV_DIM = D_INNER + 2 * SSD_GROUPS * D_STATE
CHUNK = 128
FFN_HIDDEN = -(-8 * D_MODEL // (3 * 256)) * 256

LANES = 128
FRONT_PAD = CHUNK - N_META
HEAD_PAD = LANES
ROPE_LANE0 = QK_NOPE_DIM
MASK_LANE = QK_HEAD_DIM
MASK_VALUE = -1e30
LOG2E = 1.0 / math.log(2.0)
SOFTMAX_BOUND_LIMIT = 40.0
HEAD_GROUP = 4
VMEM_LIMIT = 56 * 1024 * 1024


def _pick_tile(n, candidates):
    for c in candidates:
        if n % c == 0:
            return c
    return n


def _params(sem, vmem=VMEM_LIMIT):
    return pltpu.CompilerParams(dimension_semantics=sem, vmem_limit_bytes=vmem)


def _const_spec(shape):
    nd = len(shape)
    return pl.BlockSpec(shape, lambda *_: (0,) * nd, pipeline_mode=pl.Buffered(1))


def _rms(x, w):
    return x * lax.rsqrt(jnp.mean(x * x, axis=-1, keepdims=True) + EPS) * w


def _silu(x):
    half = 0.5 * x
    return half + half * jnp.tanh(half)


def _rope(x, cos, sin):
    lane = lax.broadcasted_iota(jnp.int32, x.shape, 1)
    half = QK_ROPE_DIM // 2
    first_half = (lane >= ROPE_LANE0) & (lane < ROPE_LANE0 + half)
    swapped = jnp.where(first_half, pltpu.roll(x, HEAD_PAD - half, 1), pltpu.roll(x, half, 1))
    return x * cos + swapped * sin


def _mla_proj_kernel(h_ref, nw_ref, win_ref, qn_ref, kvn_ref, wq_ref, wk_ref, wv_ref,
                     gkn_ref, gkr_ref, cos_ref, sin_ref, bias_ref, q_ref, k_ref, v_ref):
    u = _rms(h_ref[0], nw_ref[...]).astype(BF16)
    c = jnp.dot(u, win_ref[...], preferred_element_type=F32)
    cq = _rms(c[:, :Q_LORA_RANK], qn_ref[...]).astype(BF16)
    ckv = _rms(c[:, Q_LORA_RANK:Q_LORA_RANK + KV_LORA_RANK], kvn_ref[...]).astype(BF16)
    k_rope = c[:, Q_LORA_RANK + KV_LORA_RANK:]
    q_ref[0] = jnp.dot(cq, wq_ref[...], preferred_element_type=F32)
    k_nope = jnp.dot(ckv, wk_ref[...], preferred_element_type=F32)
    v_ref[0] = lax.dot_general(wv_ref[...], ckv, (((1,), (1,)), ((), ())),
                               preferred_element_type=F32).astype(BF16)

    cos = cos_ref[...]
    sin = sin_ref[...]
    inv_d = 1.0 / QK_HEAD_DIM
    kr = _rope(k_rope * gkr_ref[...], cos, sin)
    kr_ss = jnp.sum(k_rope * k_rope, axis=-1, keepdims=True)
    gkn = gkn_ref[...]
    bias = bias_ref[...]
    for hd in range(MLA_HEADS):
        kh = k_nope[:, hd * HEAD_PAD:(hd + 1) * HEAD_PAD]
        rk = lax.rsqrt((jnp.sum(kh * kh, axis=-1, keepdims=True) + kr_ss) * inv_d + EPS)
        k_ref[0, hd] = ((kh * gkn + kr) * rk + bias).astype(BF16)


def _mla_proj(h, nw, p, tables):
    B, Tp, _ = h.shape
    tm = _pick_tile(Tp, (384, 128))
    row = lambda b, i: (b, i, 0)
    tab = lambda b, i: (i, 0)
    wide = MLA_HEADS * HEAD_PAD
    consts = (nw, p['w_in'], p['q_norm'], p['kv_norm'], p['w_q'], p['w_k'], p['w_v'], p['gkn'], p['gkr'])
    return pl.pallas_call(
        _mla_proj_kernel,
        grid=(B, Tp // tm),
        in_specs=[pl.BlockSpec((1, tm, D_MODEL), row)]
        + [_const_spec(a.shape) for a in consts]
        + [pl.BlockSpec((tm, HEAD_PAD), tab)] * 3,
        out_specs=[pl.BlockSpec((1, tm, wide), row),
                   pl.BlockSpec((1, MLA_HEADS, tm, HEAD_PAD), lambda b, i: (b, 0, i, 0)),
                   pl.BlockSpec((1, MLA_HEADS * V_HEAD_DIM, tm), lambda b, i: (b, 0, i))],
        out_shape=[jax.ShapeDtypeStruct((B, Tp, wide), F32),
                   jax.ShapeDtypeStruct((B, MLA_HEADS, Tp, HEAD_PAD), BF16),
                   jax.ShapeDtypeStruct((B, MLA_HEADS * V_HEAD_DIM, Tp), BF16)],
        compiler_params=_params(("parallel", "parallel")),
        name="mla_proj",
    )(h, *consts, tables['cos'], tables['sin'], _key_bias(tables, p['bound']))


def _flash_kernel(bound_ref, q_ref, k_ref, v_ref, cos_ref, sin_ref, gq_ref, one_ref, o_ref, *, key_chunk):
    Tp = k_ref.shape[2]
    tq = q_ref.shape[1]
    kq = lambda k, q: lax.dot_general(k, q, (((1,), (1,)), ((), ())), preferred_element_type=F32)

    chunks = range(0, Tp, key_chunk)

    def queries(j):
        lane0 = j * HEAD_PAD
        qh = q_ref[0, :, pl.ds(lane0 if isinstance(j, int) else pl.multiple_of(lane0, HEAD_PAD), HEAD_PAD)]
        rq = lax.rsqrt(jnp.sum(qh * qh, axis=-1, keepdims=True) * (1.0 / QK_HEAD_DIM) + EPS)
        return (_rope(qh * rq * gq_ref[...], cos_ref[...], sin_ref[...]) + one_ref[...]).astype(BF16)

    def head(j, q, bounded):
        r0 = j * V_HEAD_DIM
        rows = pl.ds(r0 if isinstance(j, int) else pl.multiple_of(r0, V_HEAD_DIM), V_HEAD_DIM)
        scores = lambda c0: kq(k_ref[0, j, c0:c0 + key_chunk, :], q)
        if bounded:
            shift = None
        else:
            shift = jnp.full((1, tq), -jnp.inf, F32)
            for c0 in chunks:
                shift = jnp.maximum(shift, jnp.max(scores(c0), axis=0, keepdims=True))
        l = jnp.zeros((1, tq), F32)
        o = jnp.zeros((V_HEAD_DIM, tq), F32)
        for c0 in chunks:
            s = scores(c0)
            p = jnp.exp2(s if bounded else s - shift)
            l = l + jnp.sum(p, axis=0, keepdims=True)
            o = o + jnp.dot(v_ref[0, rows, c0:c0 + key_chunk], p.astype(BF16), preferred_element_type=F32)
        o_ref[0, rows, :] = (o * (1.0 / l)).astype(BF16)

    bounded = bound_ref[0] <= SOFTMAX_BOUND_LIMIT

    @pl.when(bounded)
    def _():
        qs = [queries(j) for j in range(HEAD_GROUP)]
        for j in range(HEAD_GROUP):
            head(j, qs[j], True)

    @pl.when(jnp.logical_not(bounded))
    def _():
        def body(j, carry):
            head(j, queries(j), False)
            return carry
        lax.fori_loop(0, HEAD_GROUP, body, 0)


def _flash(bound, q_raw, k, v, p, tables):
    B, Tp, _ = q_raw.shape
    tq = _pick_tile(Tp, (1408, 384, 128))
    gw = HEAD_GROUP * V_HEAD_DIM
    tab = pl.BlockSpec((tq, HEAD_PAD), lambda b, g, i: (i, 0))
    return pl.pallas_call(
        functools.partial(_flash_kernel, key_chunk=_pick_tile(Tp, (1408, 384, 128))),
        grid=(B, MLA_HEADS // HEAD_GROUP, Tp // tq),
        in_specs=[pl.BlockSpec(memory_space=pltpu.SMEM),
                  pl.BlockSpec((1, tq, HEAD_GROUP * HEAD_PAD), lambda b, g, i: (b, i, g)),
                  pl.BlockSpec((1, HEAD_GROUP, Tp, HEAD_PAD), lambda b, g, i: (b, g, 0, 0)),
                  pl.BlockSpec((1, gw, Tp), lambda b, g, i: (b, g, 0)),
                  tab, tab, _const_spec(p['gq'].shape), _const_spec(tables['one'].shape)],
        out_specs=pl.BlockSpec((1, gw, tq), lambda b, g, i: (b, g, i)),
        out_shape=jax.ShapeDtypeStruct((B, MLA_HEADS * V_HEAD_DIM, Tp), BF16),
        compiler_params=_params(("parallel", "parallel", "parallel")),
        name="flash",
    )(bound, q_raw, k, v, tables['cos'], tables['sin'], p['gq'], tables['one'])


def _attn_out_kernel(ot_ref, w_ref, h_ref, o_ref):
    proj = lax.dot_general(ot_ref[0], w_ref[...], (((0,), (0,)), ((), ())), preferred_element_type=F32)
    o_ref[0] = h_ref[0] + proj


def _attn_out(o_t, w, h):
    B, Tp, _ = h.shape
    tm = _pick_tile(Tp, (1408, 384, 128))
    row = pl.BlockSpec((1, tm, D_MODEL), lambda b, i: (b, i, 0))
    return pl.pallas_call(
        _attn_out_kernel,
        grid=(B, Tp // tm),
        in_specs=[pl.BlockSpec((1, o_t.shape[1], tm), lambda b, i: (b, 0, i)), _const_spec(w.shape), row],
        out_specs=row,
        out_shape=jax.ShapeDtypeStruct((B, Tp, D_MODEL), F32),
        compiler_params=_params(("parallel", "parallel")),
        name="attn_out",
    )(o_t, w, h)


def _hidden_chunks():
    bounds, start = [], 0
    while start < FFN_HIDDEN:
        stop = min(start + 1536, FFN_HIDDEN)
        bounds.append((start, stop))
        start = stop
    return bounds


def _ffn_kernel(h_ref, nw_ref, wg_ref, wu_ref, wd_ref, fw_ref, o_ref, *, final):
    h = h_ref[...].reshape(-1, D_MODEL)
    u = _rms(h, nw_ref[...]).astype(BF16)
    acc = h
    for lo, hi in _hidden_chunks():
        g = jnp.dot(u, wg_ref[:, lo:hi], preferred_element_type=F32)
        up = jnp.dot(u, wu_ref[:, lo:hi], preferred_element_type=F32)
        a = (_silu(g) * up).astype(BF16)
        acc = acc + jnp.dot(a, wd_ref[lo:hi, :], preferred_element_type=F32)
    if final:
        acc = _rms(acc, fw_ref[...])
    o_ref[...] = acc.reshape(o_ref.shape)


def _ffn(h, nw, wg, wu, wd, fw):
    R = h.shape[0]
    tm = _pick_tile(R, (768, 512, 384, 256, 128))
    row = pl.BlockSpec((tm, D_MODEL), lambda i: (i, 0))
    return pl.pallas_call(
        functools.partial(_ffn_kernel, final=False),
        grid=(R // tm,),
        in_specs=[row] + [_const_spec(a.shape) for a in (nw, wg, wu, wd, fw)],
        out_specs=row,
        out_shape=jax.ShapeDtypeStruct((R, D_MODEL), F32),
        compiler_params=_params(("parallel",)),
        name="ffn",
    )(h, nw, wg, wu, wd, fw)


def _ffn_final(h, first, count, nw, wg, wu, wd, fw):
    _, Tp, _ = h.shape
    S = Tp - CHUNK
    tm = _pick_tile(S, (512, 384, 256, 128))
    return pl.pallas_call(
        functools.partial(_ffn_kernel, final=True),
        grid=(count, S // tm),
        in_specs=[pl.BlockSpec((pl.Element(1), pl.Element(tm), pl.Element(D_MODEL)),
                               lambda b, i: (b + first, (i * (tm // CHUNK) + 1) * CHUNK, 0))]
        + [_const_spec(a.shape) for a in (nw, wg, wu, wd, fw)],
        out_specs=pl.BlockSpec((1, tm, D_MODEL), lambda b, i: (b, i, 0)),
        out_shape=jax.ShapeDtypeStruct((count, S, D_MODEL), F32),
        compiler_params=_params(("parallel", "parallel")),
        name="ffn_final",
    )(h, nw, wg, wu, wd, fw)


SUBLANES = 8


def _ssd_inproj_kernel(h_ref, nw_ref, w_ref, dtb_ref, alog_ref, z_ref, xbc_ref, cs_ref, srct_ref, outwt_ref,
                       etot_ref):
    tm = h_ref.shape[1]
    u = _rms(h_ref[0], nw_ref[...]).astype(BF16)
    y = jnp.dot(u, w_ref[...], preferred_element_type=F32)
    row = pl.program_id(1) * tm + lax.broadcasted_iota(jnp.int32, (tm, 1), 0)
    valid = row >= FRONT_PAD
    z_ref[0] = _silu(y[:, :D_INNER]).astype(BF16)
    xbc_ref[0] = jnp.where(valid, y[:, D_INNER:D_INNER + CONV_DIM], 0.0).astype(BF16)
    dt_raw = y[:, D_INNER + CONV_DIM:] + dtb_ref[...]
    dt_all = jnp.where(valid, jnp.maximum(dt_raw, 0.0) + jnp.log(1.0 + jnp.exp(-jnp.abs(dt_raw))), 0.0)

    neg_a = -jnp.exp(alog_ref[...]) * LOG2E
    forward = lax.broadcasted_iota(jnp.int32, (1, LANES), 1) < SSD_HEADS
    r = lax.broadcasted_iota(jnp.int32, (CHUNK, CHUNK), 0)
    c = lax.broadcasted_iota(jnp.int32, (CHUNK, CHUNK), 1)
    tri_f = (c <= r).astype(BF16)
    tri_b = (c >= r).astype(BF16)
    for ch in range(tm // CHUNK):
        rows = slice(ch * CHUNK, (ch + 1) * CHUNK)
        dt = dt_all[rows]
        rest = dt * neg_a
        cs_f = jnp.zeros((CHUNK, LANES), F32)
        cs_b = jnp.zeros((CHUNK, LANES), F32)
        for _ in range(3):
            part = rest.astype(BF16)
            cs_f = cs_f + jnp.dot(tri_f, part, preferred_element_type=F32)
            cs_b = cs_b + jnp.dot(tri_b, part, preferred_element_type=F32)
            rest = rest - part.astype(F32)
        cs = jnp.where(forward, cs_f, cs_b)
        total = jnp.where(forward, cs_f[CHUNK - 1:CHUNK, :], cs_b[0:1, :])
        cs_ref[0, rows, :] = cs
        srct_ref[0, :, rows] = (cs - jnp.log2(dt)).T
        outwt_ref[0, :, rows] = (jnp.exp2(total - cs) * dt).T
        etot_ref[0, ch * SUBLANES:(ch + 1) * SUBLANES, :] = jnp.broadcast_to(jnp.exp2(total), (SUBLANES, LANES))


def _ssd_inproj(h, nw, w, dtb, a_log):
    B, Tp, _ = h.shape
    tm = _pick_tile(Tp, (384, 128))
    row = lambda b, i: (b, i, 0)
    col = lambda b, i: (b, 0, i)
    per_chunk = SUBLANES * tm // CHUNK
    return pl.pallas_call(
        _ssd_inproj_kernel,
        grid=(B, Tp // tm),
        in_specs=[pl.BlockSpec((1, tm, D_MODEL), row)] + [_const_spec(a.shape) for a in (nw, w, dtb, a_log)],
        out_specs=[pl.BlockSpec((1, tm, D_INNER), row), pl.BlockSpec((1, tm, CONV_DIM), row),
                   pl.BlockSpec((1, tm, LANES), row), pl.BlockSpec((1, LANES, tm), col),
                   pl.BlockSpec((1, LANES, tm), col), pl.BlockSpec((1, per_chunk, LANES), row)],
        out_shape=[jax.ShapeDtypeStruct((B, Tp, D_INNER), BF16),
                   jax.ShapeDtypeStruct((B, Tp, CONV_DIM), BF16),
                   jax.ShapeDtypeStruct((B, Tp, LANES), F32),
                   jax.ShapeDtypeStruct((B, LANES, Tp), F32),
                   jax.ShapeDtypeStruct((B, LANES, Tp), F32),
                   jax.ShapeDtypeStruct((B, SUBLANES * Tp // CHUNK, LANES), F32)],
        compiler_params=_params(("parallel", "parallel")),
        name="ssd_inproj",
    )(h, nw, w, dtb, a_log)


CONV_MARGIN = 16
CONV_ROWS = 128
CONV_SHIFTS = tuple(t - D_CONV // 2 for t in range(D_CONV) if t != D_CONV // 2)


def _conv_kernel(x_ref, w_ref, b_ref, o_ref, xs_ref):
    Tp, cw = x_ref.shape[1], x_ref.shape[2]
    win = CONV_ROWS + 2 * CONV_MARGIN
    zeros = jnp.zeros((CONV_MARGIN, cw), BF16)
    xs_ref[0:CONV_MARGIN, :] = zeros
    xs_ref[CONV_MARGIN + Tp:, :] = zeros
    xs_ref[CONV_MARGIN:CONV_MARGIN + Tp, :] = x_ref[0]
    r = lax.broadcasted_iota(jnp.int32, (len(CONV_SHIFTS) * CONV_ROWS, win), 0)
    j = lax.broadcasted_iota(jnp.int32, (len(CONV_SHIFTS) * CONV_ROWS, win), 1)
    select = jnp.zeros(r.shape, F32)
    for k, shift in enumerate(CONV_SHIFTS):
        hit = (r >= k * CONV_ROWS) & (r < (k + 1) * CONV_ROWS) & (j == r - k * CONV_ROWS + CONV_MARGIN + shift)
        select = jnp.where(hit, 1.0, select)
    select = select.astype(BF16)
    w = [jnp.broadcast_to(w_ref[t:t + 1, :], (CONV_ROWS, cw)) for t in range(D_CONV)]
    b = jnp.broadcast_to(b_ref[...], (CONV_ROWS, cw))
    centre = D_CONV // 2

    def body(i, carry):
        r0 = pl.multiple_of(i * CONV_ROWS, CONV_ROWS)
        shifted = jnp.dot(select, xs_ref[pl.ds(r0, win), :], preferred_element_type=F32)
        acc = b + w[centre] * xs_ref[pl.ds(r0 + CONV_MARGIN, CONV_ROWS), :].astype(F32)
        for k, shift in enumerate(CONV_SHIFTS):
            acc = acc + w[centre + shift] * shifted[k * CONV_ROWS:(k + 1) * CONV_ROWS, :]
        o_ref[0, pl.ds(r0, CONV_ROWS), :] = _silu(acc).astype(BF16)
        return carry

    n_blocks = Tp // CONV_ROWS
    lax.fori_loop(0, n_blocks, body, 0, unroll=next(u for u in (33, 11, 3, 1) if n_blocks % u == 0))


def _ssd_conv(xbc, w, b):
    B, Tp, C = xbc.shape
    cw = 1024
    blk = pl.BlockSpec((1, Tp, cw), lambda bi, ci: (bi, 0, ci))
    return pl.pallas_call(
        _conv_kernel,
        grid=(B, C // cw),
        in_specs=[blk, pl.BlockSpec((D_CONV, cw), lambda bi, ci: (0, ci)),
                  pl.BlockSpec((1, cw), lambda bi, ci: (0, ci))],
        out_specs=blk,
        out_shape=jax.ShapeDtypeStruct((B, Tp, C), BF16),
        scratch_shapes=[pltpu.VMEM((Tp + 2 * CONV_MARGIN, cw), BF16)],
        compiler_params=_params(("parallel", "parallel")),
        name="ssd_conv",
    )(xbc, w, b)


def _ssd_decays(cs_ref, srct_ref, outwt_ref, etot_ref, k, reverse):
    row = lax.broadcasted_iota(jnp.int32, (CHUNK, CHUNK), 0)
    col = lax.broadcasted_iota(jnp.int32, (CHUNK, CHUNK), 1)
    causal = (col >= row) if reverse else (col <= row)
    span = slice(k * CHUNK, (k + 1) * CHUNK)
    return dict(
        hide=jnp.where(causal, 0.0, -jnp.inf),
        cs=cs_ref[0, span, :], src_t=srct_ref[0, :, span], out_w_t=outwt_ref[0, :, span],
        e_tot=etot_ref[0, k * SUBLANES:k * SUBLANES + 1, :],
        rows=span, head0=SSD_HEADS if reverse else 0)


def _ssd_group(b_ref, c_ref, g, rows):
    bg = b_ref[0, rows, g * D_STATE:(g + 1) * D_STATE]
    cg = c_ref[0, rows, g * D_STATE:(g + 1) * D_STATE]
    gram = lax.dot_general(cg, bg, (((1,), (1,)), ((), ())), preferred_element_type=F32)
    return gram, cg.astype(F32), bg.astype(F32).T


def _ssd_pair(dec, grp, x_ref, y_ref, st_ref, pair, low):
    gram, cg_f, bg_t = grp
    rows = dec['rows']
    lanes = slice(pair * LANES, (pair + 1) * LANES)
    xp = x_ref[0, rows, lanes]
    state = st_ref[pair]
    rhs = jnp.concatenate([xp, state.astype(BF16)], axis=0)
    hd0 = dec['head0'] + 2 * pair
    ys, ups = [], []
    for hd in (hd0, hd0 + 1):
        cs_l = jnp.broadcast_to(dec['cs'][:, hd:hd + 1], (CHUNK, CHUNK))
        seg = cs_l - dec['src_t'][hd:hd + 1, :]
        within = gram * jnp.exp2(seg + dec['hide'])
        from_state = cg_f * jnp.exp2(cs_l)
        lhs = jnp.concatenate([within, from_state], axis=1).astype(BF16)
        ys.append(jnp.dot(lhs, rhs, preferred_element_type=F32))
        ups.append(jnp.dot((bg_t * dec['out_w_t'][hd:hd + 1, :]).astype(BF16), xp, preferred_element_type=F32))
    y_ref[0, rows, lanes] = jnp.where(low, ys[0], ys[1]).astype(y_ref.dtype)
    e_tot = dec['e_tot']
    keep = jnp.where(low[0:1, :], jnp.broadcast_to(e_tot[:, hd0:hd0 + 1], (1, LANES)),
                     jnp.broadcast_to(e_tot[:, hd0 + 1:hd0 + 2], (1, LANES)))
    st_ref[pair] = state * keep + jnp.where(low, ups[0], ups[1])


def _ssd_scan_kernel(xf_ref, bf_ref, cf_ref, csf_ref, srcf_ref, outf_ref, etotf_ref,
                     xb_ref, bb_ref, cb_ref, csb_ref, srcb_ref, outb_ref, etotb_ref,
                     yf_ref, yb_ref, stf_ref, stb_ref):
    @pl.when(pl.program_id(1) == 0)
    def _():
        stf_ref[...] = jnp.zeros(stf_ref.shape, F32)
        stb_ref[...] = jnp.zeros(stb_ref.shape, F32)

    low = lax.broadcasted_iota(jnp.int32, (CHUNK, LANES), 1) < SSD_HEAD_DIM
    per_step = xf_ref.shape[1] // CHUNK
    pairs_per_group = HEADS_PER_GROUP // 2
    for k in range(per_step):
        dec_f = _ssd_decays(csf_ref, srcf_ref, outf_ref, etotf_ref, k, reverse=False)
        dec_b = _ssd_decays(csb_ref, srcb_ref, outb_ref, etotb_ref, per_step - 1 - k, reverse=True)
        for g in range(SSD_GROUPS):
            grp_f = _ssd_group(bf_ref, cf_ref, g, dec_f['rows'])
            grp_b = _ssd_group(bb_ref, cb_ref, g, dec_b['rows'])
            for pr in range(pairs_per_group):
                pair = g * pairs_per_group + pr
                _ssd_pair(dec_f, grp_f, xf_ref, yf_ref, stf_ref, pair, low)
                _ssd_pair(dec_b, grp_b, xb_ref, yb_ref, stb_ref, pair, low)


def _ssd_scan(xbc, cs, src_t, out_w_t, e_tot):
    B, Tp, _ = xbc.shape
    nc = Tp // CHUNK
    per_step = 3 if nc % 3 == 0 else 1
    nb = nc // per_step
    rows = per_step * CHUNK
    bc_w = SSD_GROUPS * D_STATE
    b_blk = D_INNER // bc_w

    def specs(pos):
        return [pl.BlockSpec((1, rows, D_INNER), lambda b, c: (b, pos(c), 0)),
                pl.BlockSpec((1, rows, bc_w), lambda b, c: (b, pos(c), b_blk)),
                pl.BlockSpec((1, rows, bc_w), lambda b, c: (b, pos(c), b_blk + 1)),
                pl.BlockSpec((1, rows, LANES), lambda b, c: (b, pos(c), 0)),
                pl.BlockSpec((1, LANES, rows), lambda b, c: (b, 0, pos(c))),
                pl.BlockSpec((1, LANES, rows), lambda b, c: (b, 0, pos(c))),
                pl.BlockSpec((1, per_step * SUBLANES, LANES), lambda b, c: (b, pos(c), 0))]

    fwd = lambda c: c
    bwd = lambda c: nb - 1 - c
    y_shape = jax.ShapeDtypeStruct((B, Tp, D_INNER), BF16)
    state = pltpu.VMEM((SSD_HEADS // 2, D_STATE, LANES), F32)
    return pl.pallas_call(
        _ssd_scan_kernel,
        grid=(B, nb),
        in_specs=specs(fwd) + specs(bwd),
        out_specs=[pl.BlockSpec((1, rows, D_INNER), lambda b, c: (b, fwd(c), 0)),
                   pl.BlockSpec((1, rows, D_INNER), lambda b, c: (b, bwd(c), 0))],
        out_shape=[y_shape, y_shape],
        scratch_shapes=[state, state],
        compiler_params=_params(("parallel", "arbitrary")),
        name="ssd_scan",
    )(*(2 * (xbc, xbc, xbc, cs, src_t, out_w_t, e_tot)))


def _ssd_out_kernel(yf_ref, yb_ref, x_ref, z_ref, d_ref, nw_ref, w_ref, h_ref, o_ref):
    y = yf_ref[...].astype(F32) + yb_ref[...].astype(F32) + d_ref[...] * x_ref[...].astype(F32)
    y = y * z_ref[...].astype(F32)
    r = lax.rsqrt(jnp.mean(y * y, axis=-1, keepdims=True) + EPS)
    proj = jnp.dot((y * nw_ref[...]).astype(BF16), w_ref[...], preferred_element_type=F32)
    o_ref[...] = h_ref[...] + r * proj


def _ssd_out(yf, yb, xbc, z, d_e, nw, w, h):
    R = h.shape[0]
    tm = _pick_tile(R, (512, 384, 256, 128))
    wide = pl.BlockSpec((tm, D_INNER), lambda i: (i, 0))
    row = pl.BlockSpec((tm, D_MODEL), lambda i: (i, 0))
    return pl.pallas_call(
        _ssd_out_kernel,
        grid=(R // tm,),
        in_specs=[wide, wide, wide, wide] + [_const_spec(a.shape) for a in (d_e, nw, w)] + [row],
        out_specs=row,
        out_shape=jax.ShapeDtypeStruct((R, D_MODEL), F32),
        compiler_params=_params(("parallel",)),
        name="ssd_out",
    )(yf, yb, xbc, z, d_e, nw, w, h)


def _row(v, width=None):
    v = v.astype(F32).reshape(1, -1)
    if width is not None and v.shape[1] < width:
        v = jnp.pad(v, ((0, 0), (0, width - v.shape[1])))
    return v


def _prep_mla(w_in, q_norm, kv_norm, w_q_up, w_kv_up, q_head_norm, k_head_norm, w_out):
    lat = Q_LORA_RANK + KV_LORA_RANK
    rope_cols = jnp.pad(w_in[:, lat:], ((0, 0), (ROPE_LANE0, HEAD_PAD - QK_HEAD_DIM)))
    w_in_p = jnp.concatenate([w_in[:, :lat], rope_cols], axis=1).astype(BF16)
    w_q = w_q_up.reshape(Q_LORA_RANK, MLA_HEADS, QK_HEAD_DIM)
    w_q = jnp.pad(w_q, ((0, 0), (0, 0), (0, HEAD_PAD - QK_HEAD_DIM))).reshape(Q_LORA_RANK, -1).astype(BF16)
    w_kv = w_kv_up.reshape(KV_LORA_RANK, MLA_HEADS, QK_NOPE_DIM + V_HEAD_DIM)
    w_k = jnp.pad(w_kv[:, :, :QK_NOPE_DIM], ((0, 0), (0, 0), (0, HEAD_PAD - QK_NOPE_DIM)))
    w_k = w_k.reshape(KV_LORA_RANK, -1).astype(BF16)
    w_v = w_kv[:, :, QK_NOPE_DIM:].reshape(KV_LORA_RANK, -1).T.astype(BF16)
    scale = LOG2E / math.sqrt(QK_HEAD_DIM)
    gq = q_head_norm.astype(F32)
    gk = k_head_norm.astype(F32)
    bound = 1.02 * scale * QK_HEAD_DIM * jnp.max(jnp.abs(gq)) * jnp.max(jnp.abs(gk))
    return dict(
        w_in=w_in_p, q_norm=_row(q_norm), kv_norm=_row(kv_norm), w_q=w_q, w_k=w_k, w_v=w_v,
        bound=bound.reshape(1),
        gq=_row(gq * scale, HEAD_PAD),
        gkn=_row(gk[:QK_NOPE_DIM], HEAD_PAD),
        gkr=_row(jnp.pad(gk[QK_NOPE_DIM:], (ROPE_LANE0, 0)), HEAD_PAD),
        w_out=w_out.astype(BF16))


def _attention_tables(Tp):
    half = QK_ROPE_DIM // 2
    pos = jnp.arange(Tp, dtype=F32) - FRONT_PAD
    inv_freq = ROPE_BASE ** (-jnp.arange(half, dtype=F32) / half)
    ang = pos[:, None] * inv_freq[None, :]
    cos, sin = jnp.cos(ang), jnp.sin(ang)
    tail = HEAD_PAD - QK_HEAD_DIM
    cos_t = jnp.concatenate([jnp.ones((Tp, ROPE_LANE0), F32), cos, cos, jnp.ones((Tp, tail), F32)], axis=1)
    sin_t = jnp.concatenate([jnp.zeros((Tp, ROPE_LANE0), F32), -sin, sin, jnp.zeros((Tp, tail), F32)], axis=1)
    is_pad = (jnp.arange(Tp) < FRONT_PAD)[:, None]
    lane = jnp.arange(HEAD_PAD)[None, :]
    one = (lane == MASK_LANE).astype(F32)
    return dict(cos=cos_t, sin=sin_t, pad_mask=jnp.where(is_pad, MASK_VALUE, 0.0).astype(F32) * one, one=one)


def _key_bias(tables, bound):
    return jnp.where(tables['pad_mask'] < 0, tables['pad_mask'], -bound * tables['one'])


def _prep_ssd(w_in, conv_w, conv_b, dt_bias, a_log, d_skip, norm_w, w_out):
    w_in_p = jnp.pad(w_in, ((0, 0), (0, LANES - 2 * SSD_HEADS))).astype(BF16)
    return dict(
        w_in=w_in_p, conv_w=conv_w.astype(F32), conv_b=_row(conv_b),
        dt_bias=_row(dt_bias, LANES), a_log=_row(a_log, LANES),
        d_e=_row(jnp.repeat(d_skip.astype(F32), SSD_HEAD_DIM)),
        norm=_row(norm_w), w_out=w_out.astype(BF16))


def kernel(x_prompt, x_sample, meta_tokens, mix_norm, ffn_norm, mla_w_in, mla_q_norm, mla_kv_norm, mla_w_q_up, mla_w_kv_up, mla_q_head_norm, mla_k_head_norm, mla_w_out, ssd_w_in, ssd_conv_w, ssd_conv_b, ssd_dt_bias, ssd_a_log, ssd_d, ssd_norm, ssd_w_out, ffn_w_gate, ffn_w_up, ffn_w_down, final_norm):
    assert x_prompt.shape[1:] == x_sample.shape[1:] and x_prompt.shape[2] == D_MODEL
    assert x_prompt.shape[1] % CHUNK == 0
    depth = mix_norm.shape[0]
    n_prompt = x_prompt.shape[0]
    B = n_prompt + x_sample.shape[0]
    S = x_prompt.shape[1]
    Tp = FRONT_PAD + N_META + S
    head = jnp.concatenate([jnp.zeros((FRONT_PAD, D_MODEL), F32), meta_tokens.astype(F32)], axis=0)
    h = jnp.concatenate([jnp.broadcast_to(head[None], (B, CHUNK, D_MODEL)),
                         jnp.concatenate([x_prompt, x_sample], axis=0)], axis=1)
    tables = _attention_tables(Tp)
    attn_p = (mla_w_in, mla_q_norm, mla_kv_norm, mla_w_q_up, mla_w_kv_up,
              mla_q_head_norm, mla_k_head_norm, mla_w_out)
    ssd_p = (ssd_w_in, ssd_conv_w, ssd_conv_b, ssd_dt_bias, ssd_a_log, ssd_d, ssd_norm, ssd_w_out)
    flat = lambda t: t.reshape(B * Tp, t.shape[-1])
    ia = ib = 0
    for layer in range(depth):
        nw = _row(mix_norm[layer])
        if layer % 2 == 0:
            p = _prep_mla(*[t[ia] for t in attn_p])
            ia += 1
            q, k, v = _mla_proj(h, nw, p, tables)
            o = _flash(p['bound'], q, k, v, p, tables)
            h2 = flat(_attn_out(o, p['w_out'], h))
        else:
            p = _prep_ssd(*[t[ib] for t in ssd_p])
            ib += 1
            z, xbc, *decays = _ssd_inproj(h, nw, p['w_in'], p['dt_bias'], p['a_log'])
            xbc = _ssd_conv(xbc, p['conv_w'], p['conv_b'])
            yf, yb = _ssd_scan(xbc, *decays)
            h2 = _ssd_out(flat(yf), flat(yb), flat(xbc), flat(z), p['d_e'], p['norm'], p['w_out'], flat(h))
        ffn_args = (_row(ffn_norm[layer]), ffn_w_gate[layer].astype(BF16), ffn_w_up[layer].astype(BF16),
                    ffn_w_down[layer].astype(BF16), _row(final_norm))
        if layer < depth - 1:
            h = _ffn(h2, *ffn_args).reshape(B, Tp, D_MODEL)
    h = h2.reshape(B, Tp, D_MODEL)
    return (_ffn_final(h, 0, n_prompt, *ffn_args), _ffn_final(h, n_prompt, B - n_prompt, *ffn_args))
```

```python
import functools
import math

import jax
import jax.numpy as jnp
from jax import lax
from jax.experimental import pallas as pl
from jax.experimental.pallas import tpu as pltpu

F32 = jnp.float32
BF16 = jnp.bfloat16

D_MODEL = 1024
N_META = 16
EPS = 1e-6
MLA_HEADS = 16
QK_NOPE_DIM = 64
QK_ROPE_DIM = 32
QK_HEAD_DIM = QK_NOPE_DIM + QK_ROPE_DIM
V_HEAD_DIM = 64
Q_LORA_RANK = 384
KV_LORA_RANK = 256
ROPE_BASE = 10000.0
D_INNER = 2 * D_MODEL
SSD_HEAD_DIM = 64
SSD_HEADS = D_INNER // SSD_HEAD_DIM
SSD_GROUPS = 4
HEADS_PER_GROUP = SSD_HEADS // SSD_GROUPS
D_STATE = 128
D_CONV = 5
CONV_DIM = D_INNER + 2 * SSD_GROUPS * D_STATE
CHUNK = 128
FFN_HIDDEN = -(-8 * D_MODEL // (3 * 256)) * 256

LANES = 128
FRONT_PAD = CHUNK - N_META
HEAD_PAD = LANES
ROPE_LANE0 = QK_NOPE_DIM
MASK_LANE = QK_HEAD_DIM
MASK_VALUE = -1e30
LOG2E = 1.0 / math.log(2.0)
SOFTMAX_BOUND_LIMIT = 40.0
HEAD_GROUP = 4
VMEM_LIMIT = 56 * 1024 * 1024


def _pick_tile(n, candidates):
    for c in candidates:
        if n % c == 0:
            return c
    return n


def _params(sem, vmem=VMEM_LIMIT):
    return pltpu.CompilerParams(dimension_semantics=sem, vmem_limit_bytes=vmem)


def _const_spec(shape):
    nd = len(shape)
    return pl.BlockSpec(shape, lambda *_: (0,) * nd, pipeline_mode=pl.Buffered(1))


def _rms(x, w):
    return x * lax.rsqrt(jnp.mean(x * x, axis=-1, keepdims=True) + EPS) * w


def _silu(x):
    half = 0.5 * x
    return half + half * jnp.tanh(half)


def _rope(x, cos, sin):
    lane = lax.broadcasted_iota(jnp.int32, x.shape, 1)
    half = QK_ROPE_DIM // 2
    first_half = (lane >= ROPE_LANE0) & (lane < ROPE_LANE0 + half)
    swapped = jnp.where(first_half, pltpu.roll(x, HEAD_PAD - half, 1), pltpu.roll(x, half, 1))
    return x * cos + swapped * sin


def _mla_proj_kernel(h_ref, nw_ref, win_ref, qn_ref, kvn_ref, wq_ref, wk_ref, wv_ref,
                     gkn_ref, gkr_ref, cos_ref, sin_ref, bias_ref, q_ref, k_ref, v_ref):
    u = _rms(h_ref[0], nw_ref[...]).astype(BF16)
    c = jnp.dot(u, win_ref[...], preferred_element_type=F32)
    cq = _rms(c[:, :Q_LORA_RANK], qn_ref[...]).astype(BF16)
    ckv = _rms(c[:, Q_LORA_RANK:Q_LORA_RANK + KV_LORA_RANK], kvn_ref[...]).astype(BF16)
    k_rope = c[:, Q_LORA_RANK + KV_LORA_RANK:]
    q_ref[0] = jnp.dot(cq, wq_ref[...], preferred_element_type=F32)
    k_nope = jnp.dot(ckv, wk_ref[...], preferred_element_type=F32)
    v_ref[0] = lax.dot_general(wv_ref[...], ckv, (((1,), (1,)), ((), ())),
                               preferred_element_type=F32).astype(BF16)

    cos = cos_ref[...]
    sin = sin_ref[...]
    inv_d = 1.0 / QK_HEAD_DIM
    kr = _rope(k_rope * gkr_ref[...], cos, sin)
    kr_ss = jnp.sum(k_rope * k_rope, axis=-1, keepdims=True)
    gkn = gkn_ref[...]
    bias = bias_ref[...]
    for hd in range(MLA_HEADS):
        kh = k_nope[:, hd * HEAD_PAD:(hd + 1) * HEAD_PAD]
        rk = lax.rsqrt((jnp.sum(kh * kh, axis=-1, keepdims=True) + kr_ss) * inv_d + EPS)
        k_ref[0, hd] = ((kh * gkn + kr) * rk + bias).astype(BF16)


def _mla_proj(h, nw, p, tables):
    B, Tp, _ = h.shape
    tm = _pick_tile(Tp, (384, 128))
    row = lambda b, i: (b, i, 0)
    tab = lambda b, i: (i, 0)
    wide = MLA_HEADS * HEAD_PAD
    consts = (nw, p['w_in'], p['q_norm'], p['kv_norm'], p['w_q'], p['w_k'], p['w_v'], p['gkn'], p['gkr'])
    return pl.pallas_call(
        _mla_proj_kernel,
        grid=(B, Tp // tm),
        in_specs=[pl.BlockSpec((1, tm, D_MODEL), row)]
        + [_const_spec(a.shape) for a in consts]
        + [pl.BlockSpec((tm, HEAD_PAD), tab)] * 3,
        out_specs=[pl.BlockSpec((1, tm, wide), row),
                   pl.BlockSpec((1, MLA_HEADS, tm, HEAD_PAD), lambda b, i: (b, 0, i, 0)),
                   pl.BlockSpec((1, MLA_HEADS * V_HEAD_DIM, tm), lambda b, i: (b, 0, i))],
        out_shape=[jax.ShapeDtypeStruct((B, Tp, wide), F32),
                   jax.ShapeDtypeStruct((B, MLA_HEADS, Tp, HEAD_PAD), BF16),
                   jax.ShapeDtypeStruct((B, MLA_HEADS * V_HEAD_DIM, Tp), BF16)],
        compiler_params=_params(("parallel", "parallel")),
        name="mla_proj",
    )(h, *consts, tables['cos'], tables['sin'], _key_bias(tables, p['bound']))


def _flash_kernel(bound_ref, q_ref, k_ref, v_ref, cos_ref, sin_ref, gq_ref, one_ref, o_ref, *, key_chunk):
    Tp = k_ref.shape[2]
    tq = q_ref.shape[1]
    kq = lambda k, q: lax.dot_general(k, q, (((1,), (1,)), ((), ())), preferred_element_type=F32)

    chunks = range(0, Tp, key_chunk)

    def queries(j):
        lane0 = j * HEAD_PAD
        qh = q_ref[0, :, pl.ds(lane0 if isinstance(j, int) else pl.multiple_of(lane0, HEAD_PAD), HEAD_PAD)]
        rq = lax.rsqrt(jnp.sum(qh * qh, axis=-1, keepdims=True) * (1.0 / QK_HEAD_DIM) + EPS)
        return (_rope(qh * rq * gq_ref[...], cos_ref[...], sin_ref[...]) + one_ref[...]).astype(BF16)

    def head(j, q, bounded):
        r0 = j * V_HEAD_DIM
        rows = pl.ds(r0 if isinstance(j, int) else pl.multiple_of(r0, V_HEAD_DIM), V_HEAD_DIM)
        scores = lambda c0: kq(k_ref[0, j, c0:c0 + key_chunk, :], q)
        if bounded:
            shift = None
        else:
            shift = jnp.full((1, tq), -jnp.inf, F32)
            for c0 in chunks:
                shift = jnp.maximum(shift, jnp.max(scores(c0), axis=0, keepdims=True))
        l = jnp.zeros((1, tq), F32)
        o = jnp.zeros((V_HEAD_DIM, tq), F32)
        for c0 in chunks:
            s = scores(c0)
            p = jnp.exp2(s if bounded else s - shift)
            l = l + jnp.sum(p, axis=0, keepdims=True)
            o = o + jnp.dot(v_ref[0, rows, c0:c0 + key_chunk], p.astype(BF16), preferred_element_type=F32)
        o_ref[0, rows, :] = (o * (1.0 / l)).astype(BF16)

    bounded = bound_ref[0] <= SOFTMAX_BOUND_LIMIT

    @pl.when(bounded)
    def _():
        qs = [queries(j) for j in range(HEAD_GROUP)]
        for j in range(HEAD_GROUP):
            head(j, qs[j], True)

    @pl.when(jnp.logical_not(bounded))
    def _():
        def body(j, carry):
            head(j, queries(j), False)
            return carry
        lax.fori_loop(0, HEAD_GROUP, body, 0)


def _flash(bound, q_raw, k, v, p, tables):
    B, Tp, _ = q_raw.shape
    tq = _pick_tile(Tp, (1408, 384, 128))
    gw = HEAD_GROUP * V_HEAD_DIM
    tab = pl.BlockSpec((tq, HEAD_PAD), lambda b, g, i: (i, 0))
    return pl.pallas_call(
        functools.partial(_flash_kernel, key_chunk=_pick_tile(Tp, (1408, 384, 128))),
        grid=(B, MLA_HEADS // HEAD_GROUP, Tp // tq),
        in_specs=[pl.BlockSpec(memory_space=pltpu.SMEM),
                  pl.BlockSpec((1, tq, HEAD_GROUP * HEAD_PAD), lambda b, g, i: (b, i, g)),
                  pl.BlockSpec((1, HEAD_GROUP, Tp, HEAD_PAD), lambda b, g, i: (b, g, 0, 0)),
                  pl.BlockSpec((1, gw, Tp), lambda b, g, i: (b, g, 0)),
                  tab, tab, _const_spec(p['gq'].shape), _const_spec(tables['one'].shape)],
        out_specs=pl.BlockSpec((1, gw, tq), lambda b, g, i: (b, g, i)),
        out_shape=jax.ShapeDtypeStruct((B, MLA_HEADS * V_HEAD_DIM, Tp), BF16),
        compiler_params=_params(("parallel", "parallel", "parallel")),
        name="flash",
    )(bound, q_raw, k, v, tables['cos'], tables['sin'], p['gq'], tables['one'])


def _attn_out_kernel(ot_ref, w_ref, h_ref, o_ref):
    proj = lax.dot_general(ot_ref[0], w_ref[...], (((0,), (0,)), ((), ())), preferred_element_type=F32)
    o_ref[0] = h_ref[0] + proj


def _attn_out(o_t, w, h):
    B, Tp, _ = h.shape
    tm = _pick_tile(Tp, (1408, 384, 128))
    row = pl.BlockSpec((1, tm, D_MODEL), lambda b, i: (b, i, 0))
    return pl.pallas_call(
        _attn_out_kernel,
        grid=(B, Tp // tm),
        in_specs=[pl.BlockSpec((1, o_t.shape[1], tm), lambda b, i: (b, 0, i)), _const_spec(w.shape), row],
        out_specs=row,
        out_shape=jax.ShapeDtypeStruct((B, Tp, D_MODEL), F32),
        compiler_params=_params(("parallel", "parallel")),
        name="attn_out",
    )(o_t, w, h)


def _hidden_chunks():
    bounds, start = [], 0
    while start < FFN_HIDDEN:
        stop = min(start + 1536, FFN_HIDDEN)
        bounds.append((start, stop))
        start = stop
    return bounds


def _ffn_kernel(h_ref, nw_ref, wg_ref, wu_ref, wd_ref, fw_ref, o_ref, *, final):
    h = h_ref[...].reshape(-1, D_MODEL)
    u = _rms(h, nw_ref[...]).astype(BF16)
    acc = h
    for lo, hi in _hidden_chunks():
        g = jnp.dot(u, wg_ref[:, lo:hi], preferred_element_type=F32)
        up = jnp.dot(u, wu_ref[:, lo:hi], preferred_element_type=F32)
        a = (_silu(g) * up).astype(BF16)
        acc = acc + jnp.dot(a, wd_ref[lo:hi, :], preferred_element_type=F32)
    if final:
        acc = _rms(acc, fw_ref[...])
    o_ref[...] = acc.reshape(o_ref.shape)


def _ffn(h, nw, wg, wu, wd, fw):
    R = h.shape[0]
    tm = _pick_tile(R, (768, 512, 384, 256, 128))
    row = pl.BlockSpec((tm, D_MODEL), lambda i: (i, 0))
    return pl.pallas_call(
        functools.partial(_ffn_kernel, final=False),
        grid=(R // tm,),
        in_specs=[row] + [_const_spec(a.shape) for a in (nw, wg, wu, wd, fw)],
        out_specs=row,
        out_shape=jax.ShapeDtypeStruct((R, D_MODEL), F32),
        compiler_params=_params(("parallel",)),
        name="ffn",
    )(h, nw, wg, wu, wd, fw)


def _attn_ffn_kernel(ot_ref, wo_ref, h_ref, nw_ref, wg_ref, wu_ref, wd_ref, o_ref):
    proj = lax.dot_general(ot_ref[0], wo_ref[...], (((0,), (0,)), ((), ())), preferred_element_type=F32)
    h = h_ref[0] + proj
    u = _rms(h, nw_ref[...]).astype(BF16)
    acc = h
    for lo, hi in _hidden_chunks():
        g = jnp.dot(u, wg_ref[:, lo:hi], preferred_element_type=F32)
        up = jnp.dot(u, wu_ref[:, lo:hi], preferred_element_type=F32)
        a = (_silu(g) * up).astype(BF16)
        acc = acc + jnp.dot(a, wd_ref[lo:hi, :], preferred_element_type=F32)
    o_ref[0] = acc


def _attn_ffn(o_t, wo, h, nw, wg, wu, wd):
    B, Tp, _ = h.shape
    tm = _pick_tile(Tp, (384, 128))
    row = pl.BlockSpec((1, tm, D_MODEL), lambda b, i: (b, i, 0))
    return pl.pallas_call(
        _attn_ffn_kernel,
        grid=(B, Tp // tm),
        in_specs=[pl.BlockSpec((1, o_t.shape[1], tm), lambda b, i: (b, 0, i)), _const_spec(wo.shape), row]
        + [_const_spec(a.shape) for a in (nw, wg, wu, wd)],
        out_specs=row,
        out_shape=jax.ShapeDtypeStruct((B, Tp, D_MODEL), F32),
        compiler_params=_params(("parallel", "parallel")),
        name="attn_ffn",
    )(o_t, wo, h, nw, wg, wu, wd)


def _ffn_final(h, first, count, nw, wg, wu, wd, fw):
    _, Tp, _ = h.shape
    S = Tp - CHUNK
    tm = _pick_tile(S, (512, 384, 256, 128))
    return pl.pallas_call(
        functools.partial(_ffn_kernel, final=True),
        grid=(count, S // tm),
        in_specs=[pl.BlockSpec((pl.Element(1), pl.Element(tm), pl.Element(D_MODEL)),
                               lambda b, i: (b + first, (i * (tm // CHUNK) + 1) * CHUNK, 0))]
        + [_const_spec(a.shape) for a in (nw, wg, wu, wd, fw)],
        out_specs=pl.BlockSpec((1, tm, D_MODEL), lambda b, i: (b, i, 0)),
        out_shape=jax.ShapeDtypeStruct((count, S, D_MODEL), F32),
        compiler_params=_params(("parallel", "parallel")),
        name="ffn_final",
    )(h, nw, wg, wu, wd, fw)


SUBLANES = 8


def _ssd_inproj_kernel(h_ref, nw_ref, w_ref, dtb_ref, alog_ref, z_ref, xbc_ref, cs_ref, srct_ref, outwt_ref,
                       etot_ref):
    tm = h_ref.shape[1]
    u = _rms(h_ref[0], nw_ref[...]).astype(BF16)
    y = jnp.dot(u, w_ref[...], preferred_element_type=F32)
    row = pl.program_id(1) * tm + lax.broadcasted_iota(jnp.int32, (tm, 1), 0)
    valid = row >= FRONT_PAD
    z_ref[0] = _silu(y[:, :D_INNER]).astype(BF16)
    xbc_ref[0] = jnp.where(valid, y[:, D_INNER:D_INNER + CONV_DIM], 0.0).astype(BF16)
    dt_raw = y[:, D_INNER + CONV_DIM:] + dtb_ref[...]
    dt_all = jnp.where(valid, jnp.maximum(dt_raw, 0.0) + jnp.log(1.0 + jnp.exp(-jnp.abs(dt_raw))), 0.0)

    neg_a = -jnp.exp(alog_ref[...]) * LOG2E
    forward = lax.broadcasted_iota(jnp.int32, (1, LANES), 1) < SSD_HEADS
    r = lax.broadcasted_iota(jnp.int32, (CHUNK, CHUNK), 0)
    c = lax.broadcasted_iota(jnp.int32, (CHUNK, CHUNK), 1)
    tri_f = (c <= r).astype(BF16)
    tri_b = (c >= r).astype(BF16)
    for ch in range(tm // CHUNK):
        rows = slice(ch * CHUNK, (ch + 1) * CHUNK)
        dt = dt_all[rows]
        rest = dt * neg_a
        cs_f = jnp.zeros((CHUNK, LANES), F32)
        cs_b = jnp.zeros((CHUNK, LANES), F32)
        for _ in range(3):
            part = rest.astype(BF16)
            cs_f = cs_f + jnp.dot(tri_f, part, preferred_element_type=F32)
            cs_b = cs_b + jnp.dot(tri_b, part, preferred_element_type=F32)
            rest = rest - part.astype(F32)
        cs = jnp.where(forward, cs_f, cs_b)
        total = jnp.where(forward, cs_f[CHUNK - 1:CHUNK, :], cs_b[0:1, :])
        cs_ref[0, rows, :] = cs
        srct_ref[0, :, rows] = (cs - jnp.log2(dt)).T
        outwt_ref[0, :, rows] = (jnp.exp2(total - cs) * dt).T
        etot_ref[0, ch * SUBLANES:(ch + 1) * SUBLANES, :] = jnp.broadcast_to(jnp.exp2(total), (SUBLANES, LANES))


def _ssd_inproj(h, nw, w, dtb, a_log):
    B, Tp, _ = h.shape
    tm = _pick_tile(Tp, (384, 128))
    row = lambda b, i: (b, i, 0)
    col = lambda b, i: (b, 0, i)
    per_chunk = SUBLANES * tm // CHUNK
    return pl.pallas_call(
        _ssd_inproj_kernel,
        grid=(B, Tp // tm),
        in_specs=[pl.BlockSpec((1, tm, D_MODEL), row)] + [_const_spec(a.shape) for a in (nw, w, dtb, a_log)],
        out_specs=[pl.BlockSpec((1, tm, D_INNER), row), pl.BlockSpec((1, tm, CONV_DIM), row),
                   pl.BlockSpec((1, tm, LANES), row), pl.BlockSpec((1, LANES, tm), col),
                   pl.BlockSpec((1, LANES, tm), col), pl.BlockSpec((1, per_chunk, LANES), row)],
        out_shape=[jax.ShapeDtypeStruct((B, Tp, D_INNER), BF16),
                   jax.ShapeDtypeStruct((B, Tp, CONV_DIM), BF16),
                   jax.ShapeDtypeStruct((B, Tp, LANES), F32),
                   jax.ShapeDtypeStruct((B, LANES, Tp), F32),
                   jax.ShapeDtypeStruct((B, LANES, Tp), F32),
                   jax.ShapeDtypeStruct((B, SUBLANES * Tp // CHUNK, LANES), F32)],
        compiler_params=_params(("parallel", "parallel")),
        name="ssd_inproj",
    )(h, nw, w, dtb, a_log)


CONV_MARGIN = 16
CONV_ROWS = 128
CONV_SHIFTS = tuple(t - D_CONV // 2 for t in range(D_CONV) if t != D_CONV // 2)


def _conv_kernel(x_ref, w_ref, b_ref, o_ref, xs_ref):
    Tp, cw = x_ref.shape[1], x_ref.shape[2]
    win = CONV_ROWS + 2 * CONV_MARGIN
    zeros = jnp.zeros((CONV_MARGIN, cw), BF16)
    xs_ref[0:CONV_MARGIN, :] = zeros
    xs_ref[CONV_MARGIN + Tp:, :] = zeros
    xs_ref[CONV_MARGIN:CONV_MARGIN + Tp, :] = x_ref[0]
    r = lax.broadcasted_iota(jnp.int32, (len(CONV_SHIFTS) * CONV_ROWS, win), 0)
    j = lax.broadcasted_iota(jnp.int32, (len(CONV_SHIFTS) * CONV_ROWS, win), 1)
    select = jnp.zeros(r.shape, F32)
    for k, shift in enumerate(CONV_SHIFTS):
        hit = (r >= k * CONV_ROWS) & (r < (k + 1) * CONV_ROWS) & (j == r - k * CONV_ROWS + CONV_MARGIN + shift)
        select = jnp.where(hit, 1.0, select)
    select = select.astype(BF16)
    w = [jnp.broadcast_to(w_ref[t:t + 1, :], (CONV_ROWS, cw)) for t in range(D_CONV)]
    b = jnp.broadcast_to(b_ref[...], (CONV_ROWS, cw))
    centre = D_CONV // 2

    def body(i, carry):
        r0 = pl.multiple_of(i * CONV_ROWS, CONV_ROWS)
        shifted = jnp.dot(select, xs_ref[pl.ds(r0, win), :], preferred_element_type=F32)
        acc = b + w[centre] * xs_ref[pl.ds(r0 + CONV_MARGIN, CONV_ROWS), :].astype(F32)
        for k, shift in enumerate(CONV_SHIFTS):
            acc = acc + w[centre + shift] * shifted[k * CONV_ROWS:(k + 1) * CONV_ROWS, :]
        o_ref[0, pl.ds(r0, CONV_ROWS), :] = _silu(acc).astype(BF16)
        return carry

    n_blocks = Tp // CONV_ROWS
    lax.fori_loop(0, n_blocks, body, 0, unroll=next(u for u in (33, 11, 3, 1) if n_blocks % u == 0))


def _ssd_conv(xbc, w, b):
    B, Tp, C = xbc.shape
    cw = 512
    blk = pl.BlockSpec((1, Tp, cw), lambda bi, ci: (bi, 0, ci))
    return pl.pallas_call(
        _conv_kernel,
        grid=(B, C // cw),
        in_specs=[blk, pl.BlockSpec((D_CONV, cw), lambda bi, ci: (0, ci)),
                  pl.BlockSpec((1, cw), lambda bi, ci: (0, ci))],
        out_specs=blk,
        out_shape=jax.ShapeDtypeStruct((B, Tp, C), BF16),
        scratch_shapes=[pltpu.VMEM((Tp + 2 * CONV_MARGIN, cw), BF16)],
        compiler_params=_params(("parallel", "parallel")),
        name="ssd_conv",
    )(xbc, w, b)


def _ssd_decays(cs_ref, srct_ref, outwt_ref, etot_ref, k, reverse):
    row = lax.broadcasted_iota(jnp.int32, (CHUNK, CHUNK), 0)
    col = lax.broadcasted_iota(jnp.int32, (CHUNK, CHUNK), 1)
    causal = (col >= row) if reverse else (col <= row)
    span = slice(k * CHUNK, (k + 1) * CHUNK)
    return dict(
        hide=jnp.where(causal, 0.0, -jnp.inf),
        cs=cs_ref[0, span, :], src_t=srct_ref[0, :, span], out_w_t=outwt_ref[0, :, span],
        e_tot=etot_ref[0, k * SUBLANES:k * SUBLANES + 1, :],
        rows=span, head0=SSD_HEADS if reverse else 0)


def _ssd_group(b_ref, c_ref, g, rows):
    bg = b_ref[0, rows, g * D_STATE:(g + 1) * D_STATE]
    cg = c_ref[0, rows, g * D_STATE:(g + 1) * D_STATE]
    gram = lax.dot_general(cg, bg, (((1,), (1,)), ((), ())), preferred_element_type=F32)
    return gram, cg.astype(F32), bg.astype(F32).T


def _ssd_pair(dec, grp, x_ref, y_ref, st_ref, pair, low):
    gram, cg_f, bg_t = grp
    rows = dec['rows']
    lanes = slice(pair * LANES, (pair + 1) * LANES)
    xp = x_ref[0, rows, lanes]
    state = st_ref[pair]
    rhs = jnp.concatenate([xp, state.astype(BF16)], axis=0)
    hd0 = dec['head0'] + 2 * pair
    ys, ups = [], []
    for hd in (hd0, hd0 + 1):
        cs_l = jnp.broadcast_to(dec['cs'][:, hd:hd + 1], (CHUNK, CHUNK))
        seg = cs_l - dec['src_t'][hd:hd + 1, :]
        within = gram * jnp.exp2(seg + dec['hide'])
        from_state = cg_f * jnp.exp2(cs_l)
        lhs = jnp.concatenate([within, from_state], axis=1).astype(BF16)
        ys.append(jnp.dot(lhs, rhs, preferred_element_type=F32))
        ups.append(jnp.dot((bg_t * dec['out_w_t'][hd:hd + 1, :]).astype(BF16), xp, preferred_element_type=F32))
    y_ref[0, rows, lanes] = jnp.where(low, ys[0], ys[1]).astype(y_ref.dtype)
    e_tot = dec['e_tot']
    keep = jnp.where(low[0:1, :], jnp.broadcast_to(e_tot[:, hd0:hd0 + 1], (1, LANES)),
                     jnp.broadcast_to(e_tot[:, hd0 + 1:hd0 + 2], (1, LANES)))
    st_ref[pair] = state * keep + jnp.where(low, ups[0], ups[1])


def _ssd_scan_kernel(xf_ref, bf_ref, cf_ref, csf_ref, srcf_ref, outf_ref, etotf_ref,
                     xb_ref, bb_ref, cb_ref, csb_ref, srcb_ref, outb_ref, etotb_ref,
                     yf_ref, yb_ref, stf_ref, stb_ref):
    @pl.when(pl.program_id(1) == 0)
    def _():
        stf_ref[...] = jnp.zeros(stf_ref.shape, F32)
        stb_ref[...] = jnp.zeros(stb_ref.shape, F32)

    low = lax.broadcasted_iota(jnp.int32, (CHUNK, LANES), 1) < SSD_HEAD_DIM
    per_step = xf_ref.shape[1] // CHUNK
    pairs_per_group = HEADS_PER_GROUP // 2
    for k in range(per_step):
        dec_f = _ssd_decays(csf_ref, srcf_ref, outf_ref, etotf_ref, k, reverse=False)
        dec_b = _ssd_decays(csb_ref, srcb_ref, outb_ref, etotb_ref, per_step - 1 - k, reverse=True)
        for g in range(SSD_GROUPS):
            grp_f = _ssd_group(bf_ref, cf_ref, g, dec_f['rows'])
            grp_b = _ssd_group(bb_ref, cb_ref, g, dec_b['rows'])
            for pr in range(pairs_per_group):
                pair = g * pairs_per_group + pr
                _ssd_pair(dec_f, grp_f, xf_ref, yf_ref, stf_ref, pair, low)
                _ssd_pair(dec_b, grp_b, xb_ref, yb_ref, stb_ref, pair, low)


def _ssd_scan(xbc, cs, src_t, out_w_t, e_tot):
    B, Tp, _ = xbc.shape
    nc = Tp // CHUNK
    per_step = 3 if nc % 3 == 0 else 1
    nb = nc // per_step
    rows = per_step * CHUNK
    bc_w = SSD_GROUPS * D_STATE
    b_blk = D_INNER // bc_w

    def specs(pos):
        return [pl.BlockSpec((1, rows, D_INNER), lambda b, c: (b, pos(c), 0)),
                pl.BlockSpec((1, rows, bc_w), lambda b, c: (b, pos(c), b_blk)),
                pl.BlockSpec((1, rows, bc_w), lambda b, c: (b, pos(c), b_blk + 1)),
                pl.BlockSpec((1, rows, LANES), lambda b, c: (b, pos(c), 0)),
                pl.BlockSpec((1, LANES, rows), lambda b, c: (b, 0, pos(c))),
                pl.BlockSpec((1, LANES, rows), lambda b, c: (b, 0, pos(c))),
                pl.BlockSpec((1, per_step * SUBLANES, LANES), lambda b, c: (b, pos(c), 0))]

    fwd = lambda c: c
    bwd = lambda c: nb - 1 - c
    y_shape = jax.ShapeDtypeStruct((B, Tp, D_INNER), BF16)
    state = pltpu.VMEM((SSD_HEADS // 2, D_STATE, LANES), F32)
    return pl.pallas_call(
        _ssd_scan_kernel,
        grid=(B, nb),
        in_specs=specs(fwd) + specs(bwd),
        out_specs=[pl.BlockSpec((1, rows, D_INNER), lambda b, c: (b, fwd(c), 0)),
                   pl.BlockSpec((1, rows, D_INNER), lambda b, c: (b, bwd(c), 0))],
        out_shape=[y_shape, y_shape],
        scratch_shapes=[state, state],
        compiler_params=_params(("parallel", "arbitrary")),
        name="ssd_scan",
    )(*(2 * (xbc, xbc, xbc, cs, src_t, out_w_t, e_tot)))


def _ssd_out_kernel(yf_ref, yb_ref, x_ref, z_ref, d_ref, nw_ref, w_ref, h_ref, o_ref):
    y = yf_ref[...].astype(F32) + yb_ref[...].astype(F32) + d_ref[...] * x_ref[...].astype(F32)
    y = y * z_ref[...].astype(F32)
    r = lax.rsqrt(jnp.mean(y * y, axis=-1, keepdims=True) + EPS)
    proj = jnp.dot((y * nw_ref[...]).astype(BF16), w_ref[...], preferred_element_type=F32)
    o_ref[...] = h_ref[...] + r * proj


def _ssd_out(yf, yb, xbc, z, d_e, nw, w, h):
    R = h.shape[0]
    tm = _pick_tile(R, (512, 384, 256, 128))
    wide = pl.BlockSpec((tm, D_INNER), lambda i: (i, 0))
    row = pl.BlockSpec((tm, D_MODEL), lambda i: (i, 0))
    return pl.pallas_call(
        _ssd_out_kernel,
        grid=(R // tm,),
        in_specs=[wide, wide, wide, wide] + [_const_spec(a.shape) for a in (d_e, nw, w)] + [row],
        out_specs=row,
        out_shape=jax.ShapeDtypeStruct((R, D_MODEL), F32),
        compiler_params=_params(("parallel",)),
        name="ssd_out",
    )(yf, yb, xbc, z, d_e, nw, w, h)


def _row(v, width=None):
    v = v.astype(F32).reshape(1, -1)
    if width is not None and v.shape[1] < width:
        v = jnp.pad(v, ((0, 0), (0, width - v.shape[1])))
    return v


def _prep_mla(w_in, q_norm, kv_norm, w_q_up, w_kv_up, q_head_norm, k_head_norm, w_out):
    lat = Q_LORA_RANK + KV_LORA_RANK
    rope_cols = jnp.pad(w_in[:, lat:], ((0, 0), (ROPE_LANE0, HEAD_PAD - QK_HEAD_DIM)))
    w_in_p = jnp.concatenate([w_in[:, :lat], rope_cols], axis=1).astype(BF16)
    w_q = w_q_up.reshape(Q_LORA_RANK, MLA_HEADS, QK_HEAD_DIM)
    w_q = jnp.pad(w_q, ((0, 0), (0, 0), (0, HEAD_PAD - QK_HEAD_DIM))).reshape(Q_LORA_RANK, -1).astype(BF16)
    w_kv = w_kv_up.reshape(KV_LORA_RANK, MLA_HEADS, QK_NOPE_DIM + V_HEAD_DIM)
    w_k = jnp.pad(w_kv[:, :, :QK_NOPE_DIM], ((0, 0), (0, 0), (0, HEAD_PAD - QK_NOPE_DIM)))
    w_k = w_k.reshape(KV_LORA_RANK, -1).astype(BF16)
    w_v = w_kv[:, :, QK_NOPE_DIM:].reshape(KV_LORA_RANK, -1).T.astype(BF16)
    scale = LOG2E / math.sqrt(QK_HEAD_DIM)
    gq = q_head_norm.astype(F32)
    gk = k_head_norm.astype(F32)
    bound = 1.02 * scale * QK_HEAD_DIM * jnp.max(jnp.abs(gq)) * jnp.max(jnp.abs(gk))
    return dict(
        w_in=w_in_p, q_norm=_row(q_norm), kv_norm=_row(kv_norm), w_q=w_q, w_k=w_k, w_v=w_v,
        bound=bound.reshape(1),
        gq=_row(gq * scale, HEAD_PAD),
        gkn=_row(gk[:QK_NOPE_DIM], HEAD_PAD),
        gkr=_row(jnp.pad(gk[QK_NOPE_DIM:], (ROPE_LANE0, 0)), HEAD_PAD),
        w_out=w_out.astype(BF16))


def _attention_tables(Tp):
    half = QK_ROPE_DIM // 2
    pos = jnp.arange(Tp, dtype=F32) - FRONT_PAD
    inv_freq = ROPE_BASE ** (-jnp.arange(half, dtype=F32) / half)
    ang = pos[:, None] * inv_freq[None, :]
    cos, sin = jnp.cos(ang), jnp.sin(ang)
    tail = HEAD_PAD - QK_HEAD_DIM
    cos_t = jnp.concatenate([jnp.ones((Tp, ROPE_LANE0), F32), cos, cos, jnp.ones((Tp, tail), F32)], axis=1)
    sin_t = jnp.concatenate([jnp.zeros((Tp, ROPE_LANE0), F32), -sin, sin, jnp.zeros((Tp, tail), F32)], axis=1)
    is_pad = (jnp.arange(Tp) < FRONT_PAD)[:, None]
    lane = jnp.arange(HEAD_PAD)[None, :]
    one = (lane == MASK_LANE).astype(F32)
    return dict(cos=cos_t, sin=sin_t, pad_mask=jnp.where(is_pad, MASK_VALUE, 0.0).astype(F32) * one, one=one)


def _key_bias(tables, bound):
    return jnp.where(tables['pad_mask'] < 0, tables['pad_mask'], -bound * tables['one'])


def _prep_ssd(w_in, conv_w, conv_b, dt_bias, a_log, d_skip, norm_w, w_out):
    w_in_p = jnp.pad(w_in, ((0, 0), (0, LANES - 2 * SSD_HEADS))).astype(BF16)
    return dict(
        w_in=w_in_p, conv_w=conv_w.astype(F32), conv_b=_row(conv_b),
        dt_bias=_row(dt_bias, LANES), a_log=_row(a_log, LANES),
        d_e=_row(jnp.repeat(d_skip.astype(F32), SSD_HEAD_DIM)),
        norm=_row(norm_w), w_out=w_out.astype(BF16))


def kernel(x_prompt, x_sample, meta_tokens, mix_norm, ffn_norm, mla_w_in, mla_q_norm, mla_kv_norm, mla_w_q_up, mla_w_kv_up, mla_q_head_norm, mla_k_head_norm, mla_w_out, ssd_w_in, ssd_conv_w, ssd_conv_b, ssd_dt_bias, ssd_a_log, ssd_d, ssd_norm, ssd_w_out, ffn_w_gate, ffn_w_up, ffn_w_down, final_norm):
    assert x_prompt.shape[1:] == x_sample.shape[1:] and x_prompt.shape[2] == D_MODEL
    assert x_prompt.shape[1] % CHUNK == 0
    depth = mix_norm.shape[0]
    n_prompt = x_prompt.shape[0]
    B = n_prompt + x_sample.shape[0]
    S = x_prompt.shape[1]
    Tp = FRONT_PAD + N_META + S
    head = jnp.concatenate([jnp.zeros((FRONT_PAD, D_MODEL), F32), meta_tokens.astype(F32)], axis=0)
    h = jnp.concatenate([jnp.broadcast_to(head[None], (B, CHUNK, D_MODEL)),
                         jnp.concatenate([x_prompt, x_sample], axis=0)], axis=1)
    tables = _attention_tables(Tp)
    attn_p = (mla_w_in, mla_q_norm, mla_kv_norm, mla_w_q_up, mla_w_kv_up,
              mla_q_head_norm, mla_k_head_norm, mla_w_out)
    ssd_p = (ssd_w_in, ssd_conv_w, ssd_conv_b, ssd_dt_bias, ssd_a_log, ssd_d, ssd_norm, ssd_w_out)
    flat = lambda t: t.reshape(B * Tp, t.shape[-1])
    ia = ib = 0
    for layer in range(depth):
        nw = _row(mix_norm[layer])
        ffn_args = (_row(ffn_norm[layer]), ffn_w_gate[layer].astype(BF16), ffn_w_up[layer].astype(BF16),
                    ffn_w_down[layer].astype(BF16), _row(final_norm))
        last = layer == depth - 1
        if layer % 2 == 0:
            p = _prep_mla(*[t[ia] for t in attn_p])
            ia += 1
            q, k, v = _mla_proj(h, nw, p, tables)
            o = _flash(p['bound'], q, k, v, p, tables)
            if not last:
                h = _attn_ffn(o, p['w_out'], h, *ffn_args[:4])
                continue
            h2 = flat(_attn_out(o, p['w_out'], h))
        else:
            p = _prep_ssd(*[t[ib] for t in ssd_p])
            ib += 1
            z, xbc, *decays = _ssd_inproj(h, nw, p['w_in'], p['dt_bias'], p['a_log'])
            xbc = _ssd_conv(xbc, p['conv_w'], p['conv_b'])
            yf, yb = _ssd_scan(xbc, *decays)
            h2 = _ssd_out(flat(yf), flat(yb), flat(xbc), flat(z), p['d_e'], p['norm'], p['w_out'], flat(h))
        if not last:
            h = _ffn(h2, *ffn_args).reshape(B, Tp, D_MODEL)
    h = h2.reshape(B, Tp, D_MODEL)
    return (_ffn_final(h, 0, n_prompt, *ffn_args), _ffn_final(h, n_prompt, B - n_prompt, *ffn_args))
```
